```python
import math
import jax, jax.numpy as jnp
from jax import lax
import numpy as np

D_MODEL = 2048
BATCH = 2
SEQ = 4096
DEPTH = 1

N_META = 16
EPS = 1e-6
MLA_HEADS = 16
Q_LORA = 512
KV_LORA = 512
QK_NOPE = 128
QK_ROPE = 64
V_HEAD = 128
ROPE_THETA = 10000.0
Q_BLOCK = 128
SSM_WIDTH = 1024
SSM_GROUP = 16
SSM_GROUPS = SSM_WIDTH // SSM_GROUP
SSM_STATE = 64
DT_MIN = 1e-3
DT_MAX = 1e-1
N_BRANCH = 2
D_FF = -(-8 * D_MODEL // (3 * 256)) * 256
OFF_Q = Q_LORA
OFF_KV = OFF_Q + KV_LORA
OFF_KR = OFF_KV + QK_ROPE
OFF_U = OFF_KR + SSM_WIDTH
IN_WIDTH = OFF_U + N_BRANCH * D_MODEL

kernel_name = "hybrid_mla_s5_gated_block"


def rms_norm(x, g):
    xf = x.astype(jnp.float32)
    y = xf * lax.rsqrt(jnp.mean(xf * xf, axis=-1, keepdims=True) + EPS)
    return y.astype(x.dtype) * g.astype(x.dtype)


def rope_tables(length):
    half = QK_ROPE // 2
    freqs = ROPE_THETA ** (-jnp.arange(half, dtype=jnp.float32) / half)
    ang = jnp.arange(length, dtype=jnp.float32)[:, None] * freqs[None, :]
    return jnp.cos(ang), jnp.sin(ang)


def apply_rope(x, cos, sin):
    half = QK_ROPE // 2
    x1, x2 = x[..., :half], x[..., half:]
    c, s = cos.astype(x.dtype), sin.astype(x.dtype)
    return jnp.concatenate([x1 * c - x2 * s, x2 * c + x1 * s], axis=-1)


def mla_attention(q_nope, q_rope, k_nope, k_rope, v):
    length = q_nope.shape[1]
    scale = (QK_NOPE + QK_ROPE) ** -0.5
    outs = []
    for start in range(0, length, Q_BLOCK):
        end = min(start + Q_BLOCK, length)
        s = (jnp.einsum('bqhd,bkhd->bhqk', q_nope[:, start:end], k_nope[:, :end])
             + jnp.einsum('bqhr,bkr->bhqk', q_rope[:, start:end], k_rope[:, :end]))
        s = s.astype(jnp.float32) * scale
        q_pos = jnp.arange(start, end)[:, None]
        k_pos = jnp.arange(end)[None, :]
        s = jnp.where(k_pos <= q_pos, s, -jnp.inf)
        p = jax.nn.softmax(s, axis=-1).astype(v.dtype)
        outs.append(jnp.einsum('bhqk,bkhd->bqhd', p, v[:, :end]))
    return jnp.concatenate(outs, axis=1)


def s5_scan(u, lam_re, lam_im, log_step, b_re, b_im, c_re, c_im, d):
    f32 = jnp.float32
    step = jnp.exp(log_step.astype(f32))[:, None]
    lr, li = lam_re.astype(f32), lam_im.astype(f32)
    mag = jnp.exp(lr * step)
    ab_re, ab_im = mag * jnp.cos(li * step), mag * jnp.sin(li * step)
    den = lr * lr + li * li
    nr, ni = ab_re - 1.0, ab_im
    coef_re = (nr * lr + ni * li) / den
    coef_im = (ni * lr - nr * li) / den
    br, bi = b_re.astype(f32), b_im.astype(f32)
    bb_re = coef_re[..., None] * br - coef_im[..., None] * bi
    bb_im = coef_re[..., None] * bi + coef_im[..., None] * br
    uf = u.astype(f32)
    bu_re = jnp.einsum('blgp,gnp->blgn', uf, bb_re)
    bu_im = jnp.einsum('blgp,gnp->blgn', uf, bb_im)
    a_re = jnp.broadcast_to(ab_re, bu_re.shape)
    a_im = jnp.broadcast_to(ab_im, bu_im.shape)

    def combine(e1, e2):
        a1r, a1i, b1r, b1i = e1
        a2r, a2i, b2r, b2i = e2
        return (a2r * a1r - a2i * a1i,
                a2r * a1i + a2i * a1r,
                a2r * b1r - a2i * b1i + b2r,
                a2r * b1i + a2i * b1r + b2i)

    _, _, x_re, x_im = lax.associative_scan(combine, (a_re, a_im, bu_re, bu_im), axis=1)
    y = (jnp.einsum('blgn,gpn->blgp', x_re, c_re.astype(f32))
         - jnp.einsum('blgn,gpn->blgp', x_im, c_im.astype(f32))
         + d.astype(f32) * uf)
    return y


def setup_inputs(seed: int = 0) -> dict:
    key = jax.random.key(seed)
    ks = jax.random.split(key, 32)
    f32 = jnp.float32
    nrm = lambda k, shape, scale: jax.random.normal(k, shape, f32) * scale
    gain = lambda k, shape: 1.0 + 0.02 * jax.random.normal(k, shape, f32)
    G, N, P = SSM_GROUPS, SSM_STATE, SSM_GROUP
    lam_re = -0.5 * jnp.exp(0.05 * jax.random.normal(ks[9], (DEPTH, G, N), f32))
    lam_im = jnp.broadcast_to(math.pi * jnp.arange(N, dtype=f32), (DEPTH, G, N))
    log_step = jax.random.uniform(ks[10], (DEPTH, G), f32, math.log(DT_MIN), math.log(DT_MAX))
    return {
        "x": nrm(ks[0], (BATCH, SEQ, D_MODEL), 1.0),
        "meta_tokens": nrm(ks[1], (N_META, D_MODEL), 1.0),
        "norm_mix": gain(ks[2], (DEPTH, D_MODEL)),
        "w_in": nrm(ks[3], (DEPTH, D_MODEL, IN_WIDTH), D_MODEL ** -0.5),
        "norm_q": gain(ks[4], (DEPTH, Q_LORA)),
        "w_q_up": nrm(ks[5], (DEPTH, Q_LORA, MLA_HEADS * (QK_NOPE + QK_ROPE)), Q_LORA ** -0.5),
        "norm_kv": gain(ks[6], (DEPTH, KV_LORA)),
        "w_kv_up": nrm(ks[7], (DEPTH, KV_LORA, MLA_HEADS * (QK_NOPE + V_HEAD)), KV_LORA ** -0.5),
        "w_attn_proj": nrm(ks[8], (DEPTH, MLA_HEADS * V_HEAD, D_MODEL), (MLA_HEADS * V_HEAD) ** -0.5),
        "ssm_lambda_re": lam_re,
        "ssm_lambda_im": lam_im,
        "ssm_log_step": log_step,
        "ssm_b_re": nrm(ks[11], (DEPTH, G, N, P), (2 * P) ** -0.5),
        "ssm_b_im": nrm(ks[12], (DEPTH, G, N, P), (2 * P) ** -0.5),
        "ssm_c_re": nrm(ks[13], (DEPTH, G, P, N), N ** -0.5),
        "ssm_c_im": nrm(ks[14], (DEPTH, G, P, N), N ** -0.5),
        "ssm_d": nrm(ks[15], (DEPTH, G, P), 1.0),
        "w_glu_val": nrm(ks[16], (DEPTH, SSM_WIDTH, D_MODEL), SSM_WIDTH ** -0.5),
        "w_glu_gate": nrm(ks[17], (DEPTH, SSM_WIDTH, D_MODEL), SSM_WIDTH ** -0.5),
        "w_out": nrm(ks[18], (DEPTH, D_MODEL, D_MODEL), D_MODEL ** -0.5),
        "norm_ffn": gain(ks[19], (DEPTH, D_MODEL)),
        "w_ffn_gate": nrm(ks[20], (DEPTH, D_MODEL, D_FF), D_MODEL ** -0.5),
        "w_ffn_up": nrm(ks[21], (DEPTH, D_MODEL, D_FF), D_MODEL ** -0.5),
        "w_ffn_down": nrm(ks[22], (DEPTH, D_FF, D_MODEL), D_FF ** -0.5),
        "norm_final": gain(ks[23], (D_MODEL,)),
    }


def reference(x, meta_tokens, norm_mix, w_in, norm_q, w_q_up, norm_kv, w_kv_up, w_attn_proj,
              ssm_lambda_re, ssm_lambda_im, ssm_log_step, ssm_b_re, ssm_b_im, ssm_c_re, ssm_c_im,
              ssm_d, w_glu_val, w_glu_gate, w_out, norm_ffn, w_ffn_gate, w_ffn_up, w_ffn_down,
              norm_final):
    bsz = x.shape[0]
    meta = jnp.broadcast_to(meta_tokens[None].astype(x.dtype), (bsz, N_META, D_MODEL))
    h = jnp.concatenate([meta, x], axis=1)
    length = h.shape[1]
    cos, sin = rope_tables(length)
    for l in range(DEPTH):
        n = rms_norm(h, norm_mix[l])
        z = n @ w_in[l]
        q_lat, kv_lat, k_r, u, gates = jnp.split(z, [OFF_Q, OFF_KV, OFF_KR, OFF_U], axis=-1)
        q = (rms_norm(q_lat, norm_q[l]) @ w_q_up[l]).reshape(bsz, length, MLA_HEADS, QK_NOPE + QK_ROPE)
        kv = (rms_norm(kv_lat, norm_kv[l]) @ w_kv_up[l]).reshape(bsz, length, MLA_HEADS, QK_NOPE + V_HEAD)
        q_nope = q[..., :QK_NOPE]
        q_rope = apply_rope(q[..., QK_NOPE:], cos[:, None, :], sin[:, None, :])
        k_nope, v = kv[..., :QK_NOPE], kv[..., QK_NOPE:]
        k_rope = apply_rope(k_r, cos, sin)
        attn = mla_attention(q_nope, q_rope, k_nope, k_rope, v).reshape(bsz, length, MLA_HEADS * V_HEAD)
        attn_out = attn @ w_attn_proj[l]
        y = s5_scan(u.reshape(bsz, length, SSM_GROUPS, SSM_GROUP),
                    ssm_lambda_re[l], ssm_lambda_im[l], ssm_log_step[l],
                    ssm_b_re[l], ssm_b_im[l], ssm_c_re[l], ssm_c_im[l], ssm_d[l])
        y = jax.nn.gelu(y.reshape(bsz, length, SSM_WIDTH).astype(h.dtype))
        ssm_out = (y @ w_glu_val[l]) * jax.nn.sigmoid(y @ w_glu_gate[l])
        g = gates.reshape(bsz, length, N_BRANCH, D_MODEL)
        mixed = jax.nn.sigmoid(g[:, :, 0]) * attn_out + jax.nn.sigmoid(g[:, :, 1]) * ssm_out
        h = h + mixed @ w_out[l]
        m = rms_norm(h, norm_ffn[l])
        h = h + (jax.nn.silu(m @ w_ffn_gate[l]) * (m @ w_ffn_up[l])) @ w_ffn_down[l]
    out = rms_norm(h, norm_final)
    return out[:, N_META:]
```

```python
import functools
import math

import jax
import jax.numpy as jnp
from jax import lax
from jax.experimental import pallas as pl
from jax.experimental.pallas import tpu as pltpu

F32 = jnp.float32
BF16 = jnp.bfloat16

D_MODEL = 2048
N_META = 16
EPS = 1e-6
HEADS = 16
Q_LORA = 512
KV_LORA = 512
QK_NOPE = 128
QK_ROPE = 64
V_HEAD = 128
ROPE_THETA = 10000.0
SSM_WIDTH = 1024
SSM_P = 16
SSM_G = 64
SSM_N = 64
D_FF = 5632
LANES = 128
HEAD_PAD = 2 * LANES
LAT_W = Q_LORA + KV_LORA + LANES
CHUNK = 16
CP = CHUNK * SSM_P
VMEM_LIMIT = 56 * 1024 * 1024


def _cparams(*sem):
    return pltpu.CompilerParams(dimension_semantics=sem, vmem_limit_bytes=VMEM_LIMIT)


def _sigmoid(x):
    return 1.0 / (1.0 + jnp.exp(-x))


def _gelu_tanh(x):
    return x * (0.5 * (1.0 + jnp.tanh(math.sqrt(2.0 / math.pi) * (x + 0.044715 * (x * x * x)))))


def _rms(x, g):
    ms = jnp.mean(x * x, axis=-1, keepdims=True)
    return x * lax.rsqrt(ms + EPS) * g


def _norm_matmul_kernel(x_ref, g_ref, w_ref, o_ref, n_ref):
    @pl.when(pl.program_id(1) == 0)
    def _():
        n_ref[...] = _rms(x_ref[...], g_ref[...]).astype(BF16)

    o_ref[...] = jnp.dot(n_ref[...], w_ref[...], preferred_element_type=F32).astype(o_ref.dtype)


def _norm_matmul(x, g, w, out_dtype, tm, tn):
    m, k = x.shape
    n = w.shape[1]
    return pl.pallas_call(
        _norm_matmul_kernel,
        grid=(m // tm, n // tn),
        in_specs=[
            pl.BlockSpec((tm, k), lambda i, j: (i, 0)),
            pl.BlockSpec((1, k), lambda i, j: (0, 0)),
            pl.BlockSpec((k, tn), lambda i, j: (0, j)),
        ],
        out_specs=pl.BlockSpec((tm, tn), lambda i, j: (i, j)),
        out_shape=jax.ShapeDtypeStruct((m, n), out_dtype),
        scratch_shapes=[pltpu.VMEM((tm, k), BF16)],
        compiler_params=_cparams("parallel", "arbitrary"),
        name="norm_matmul",
    )(x, g, w)


def _rope_slab(x, cosf, sin1, sin2):
    return (x * cosf + pltpu.roll(x, LANES - QK_ROPE // 2, 1) * sin1
            + pltpu.roll(x, QK_ROPE // 2, 1) * sin2)


def _up_rope_kernel(ql_ref, kvl_ref, kr_ref, gq_ref, gkv_ref, wq_ref, wk_ref, wv_ref,
                    cos_ref, sin1_ref, sin2_ref, q_ref, k_ref, v_ref):
    cosf, sin1, sin2 = cos_ref[...], sin1_ref[...], sin2_ref[...]
    qn = _rms(ql_ref[...], gq_ref[...]).astype(BF16)
    kvn = _rms(kvl_ref[...], gkv_ref[...]).astype(BF16)
    k_rope = _rope_slab(kr_ref[...], cosf, sin1, sin2).astype(BF16)
    for h in range(HEADS):
        acc = jnp.dot(qn, wq_ref[:, h * HEAD_PAD:(h + 1) * HEAD_PAD], preferred_element_type=F32)
        q_ref[h, :, :LANES] = acc[:, :LANES].astype(BF16)
        q_ref[h, :, LANES:] = _rope_slab(acc[:, LANES:], cosf, sin1, sin2).astype(BF16)
    for hp in range(HEADS // 2):
        acc_k = jnp.dot(kvn, wk_ref[:, hp * 256:(hp + 1) * 256], preferred_element_type=F32)
        acc_v = jnp.dot(kvn, wv_ref[:, hp * 256:(hp + 1) * 256], preferred_element_type=F32)
        for s in range(2):
            h = 2 * hp + s
            k_ref[h, :, :LANES] = acc_k[:, s * LANES:(s + 1) * LANES].astype(BF16)
            k_ref[h, :, LANES:] = k_rope
            v_ref[h] = acc_v[:, s * LANES:(s + 1) * LANES].astype(BF16)


def _up_rope(lat, gq, gkv, wq, wk, wv, cosf, sin1, sin2, tm):
    m = lat.shape[0]
    n_pos_blocks = cosf.shape[0] // tm
    row = lambda i: (i, 0)
    const = lambda i: (0, 0)
    pos = lambda i: (i % n_pos_blocks, 0)
    head_major = lambda i: (0, i, 0)
    return pl.pallas_call(
        _up_rope_kernel,
        grid=(m // tm,),
        in_specs=[
            pl.BlockSpec((tm, Q_LORA), row),
            pl.BlockSpec((tm, KV_LORA), lambda i: (i, 1)),
            pl.BlockSpec((tm, LANES), lambda i: (i, (Q_LORA + KV_LORA) // LANES)),
            pl.BlockSpec((1, Q_LORA), const),
            pl.BlockSpec((1, KV_LORA), const),
            pl.BlockSpec(wq.shape, const),
            pl.BlockSpec(wk.shape, const),
            pl.BlockSpec(wv.shape, const),
            pl.BlockSpec((tm, LANES), pos),
            pl.BlockSpec((tm, LANES), pos),
            pl.BlockSpec((tm, LANES), pos),
        ],
        out_specs=[
            pl.BlockSpec((HEADS, tm, HEAD_PAD), head_major),
            pl.BlockSpec((HEADS, tm, HEAD_PAD), head_major),
            pl.BlockSpec((HEADS, tm, V_HEAD), head_major),
        ],
        out_shape=[
            jax.ShapeDtypeStruct((HEADS, m, HEAD_PAD), BF16),
            jax.ShapeDtypeStruct((HEADS, m, HEAD_PAD), BF16),
            jax.ShapeDtypeStruct((HEADS, m, V_HEAD), BF16),
        ],
        compiler_params=_cparams("parallel"),
        name="up_rope",
    )(lat, lat, lat, gq, gkv, wq, wk, wv, cosf, sin1, sin2)


def _attn_kernel(qi_ref, ki_ref, q_ref, k_ref, v_ref, km_ref, vm_ref, o_ref, m_ref, l_ref, acc_ref):
    t = pl.program_id(1)
    qi = qi_ref[t]
    ki = ki_ref[t]
    tq = q_ref.shape[1]
    nt = (((1,), (1,)), ((), ()))

    @pl.when(ki == 0)
    def _init():
        def body(h, c):
            s = lax.dot_general(q_ref[h], km_ref[h], nt, preferred_element_type=F32)
            col = lax.broadcasted_iota(jnp.int32, s.shape, 1)
            s = jnp.where(col < N_META, s, -jnp.inf)
            m = jnp.max(s, axis=-1, keepdims=True)
            p = jnp.exp(s - m)
            m_ref[h] = m
            l_ref[h] = jnp.sum(p, axis=-1, keepdims=True)
            acc_ref[h] = jnp.dot(p.astype(BF16), vm_ref[h], preferred_element_type=F32)
            return c
        lax.fori_loop(0, HEADS, body, 0)

    def tile(masked):
        def body(h, c):
            s = lax.dot_general(q_ref[h], k_ref[h], nt, preferred_element_type=F32)
            if masked:
                row = lax.broadcasted_iota(jnp.int32, s.shape, 0)
                col = lax.broadcasted_iota(jnp.int32, s.shape, 1)
                s = jnp.where(col <= row, s, -jnp.inf)
            m_prev = m_ref[h]
            m_new = jnp.maximum(m_prev, jnp.max(s, axis=-1, keepdims=True))
            alpha = jnp.exp(m_prev - m_new)
            p = jnp.exp(s - m_new)
            l_ref[h] = alpha * l_ref[h] + jnp.sum(p, axis=-1, keepdims=True)
            acc_ref[h] = alpha * acc_ref[h] + jnp.dot(p.astype(BF16), v_ref[h],
                                                      preferred_element_type=F32)
            m_ref[h] = m_new
            return c
        lax.fori_loop(0, HEADS, body, 0)

    @pl.when(ki < qi)
    def _full():
        tile(False)

    @pl.when(ki == qi)
    def _diag():
        tile(True)
        for h in range(HEADS):
            o_ref[:, h * V_HEAD:(h + 1) * V_HEAD] = (acc_ref[h] / l_ref[h]).astype(o_ref.dtype)


def _attention(q, k, v, km, vm, bsz, seq, tq):
    nq = seq // tq
    pairs = [(a, b) for a in range(nq) for b in range(a + 1)]
    qi_arr = jnp.asarray([p[0] for p in pairs], jnp.int32)
    ki_arr = jnp.asarray([p[1] for p in pairs], jnp.int32)
    grid_spec = pltpu.PrefetchScalarGridSpec(
        num_scalar_prefetch=2,
        grid=(bsz, len(pairs)),
        in_specs=[
            pl.BlockSpec((HEADS, tq, HEAD_PAD), lambda b, t, qi, ki: (0, b * nq + qi[t], 0)),
            pl.BlockSpec((HEADS, tq, HEAD_PAD), lambda b, t, qi, ki: (0, b * nq + ki[t], 0)),
            pl.BlockSpec((HEADS, tq, V_HEAD), lambda b, t, qi, ki: (0, b * nq + ki[t], 0)),
            pl.BlockSpec(km.shape, lambda b, t, qi, ki: (0, 0, 0)),
            pl.BlockSpec(vm.shape, lambda b, t, qi, ki: (0, 0, 0)),
        ],
        out_specs=pl.BlockSpec((tq, HEADS * V_HEAD), lambda b, t, qi, ki: (b * nq + qi[t], 0)),
        scratch_shapes=[
            pltpu.VMEM((HEADS, tq, 1), F32),
            pltpu.VMEM((HEADS, tq, 1), F32),
            pltpu.VMEM((HEADS, tq, V_HEAD), F32),
        ],
    )
    return pl.pallas_call(
        _attn_kernel,
        grid_spec=grid_spec,
        out_shape=jax.ShapeDtypeStruct((bsz * seq, HEADS * V_HEAD), BF16),
        compiler_params=_cparams("parallel", "arbitrary"),
        name="flash_attn",
    )(qi_arr, ki_arr, q, k, v, km, vm)


def _ssm_param_kernel(lr_ref, li_ref, ls_ref, bre_ref, bim_ref, cre_ref, cim_ref,
                      e_re_ref, e_im_ref, ce_re_ref, ce_im_ref, eb_re_ref, eb_im_ref, t_ref):
    nt = (((1,), (1,)), ((), ()))
    for g in range(lr_ref.shape[0]):
        lr, li = lr_ref[g], li_ref[g]
        step = jnp.exp(ls_ref[g])
        kf = lax.broadcasted_iota(jnp.int32, (CHUNK + 1, SSM_N), 0).astype(F32)
        mag = jnp.exp(kf * (lr * step))
        ang = kf * (li * step)
        e_re, e_im = mag * jnp.cos(ang), mag * jnp.sin(ang)
        e_re_ref[g], e_im_ref[g] = e_re, e_im
        nr, ni = e_re[1:2] - 1.0, e_im[1:2]
        den = lr * lr + li * li
        coef_re = (nr * lr + ni * li) / den
        coef_im = (ni * lr - nr * li) / den
        bre, bim = bre_ref[g], bim_ref[g]
        bb_re = coef_re * bre - coef_im * bim
        bb_im = coef_re * bim + coef_im * bre
        cre, cim = cre_ref[g], cim_ref[g]
        ce_re_lags, ce_im_lags = [], []
        for k in range(CHUNK + 1):
            er, ei = e_re[k:k + 1], e_im[k:k + 1]
            ce_re = cre * er - cim * ei
            ce_im = cre * ei + cim * er
            ce_re_ref[g, k], ce_im_ref[g, k] = ce_re, ce_im
            if k < CHUNK:
                ce_re_lags.append(ce_re)
                ce_im_lags.append(ce_im)
                eb_re_ref[g, k] = er * bb_re - ei * bb_im
                eb_im_ref[g, k] = er * bb_im + ei * bb_re
        t_ref[g] = (
            lax.dot_general(jnp.concatenate(ce_re_lags, axis=0), bb_re, nt,
                            precision=lax.Precision.HIGHEST, preferred_element_type=F32)
            - lax.dot_general(jnp.concatenate(ce_im_lags, axis=0), bb_im, nt,
                              precision=lax.Precision.HIGHEST, preferred_element_type=F32))


def _ssm_params(lam_re, lam_im, log_step, b_re, b_im, c_re, c_im, gb):
    g = SSM_G
    kp1 = CHUNK + 1
    row = lambda shape: pl.BlockSpec((gb,) + shape, lambda i: (i,) + (0,) * len(shape))
    outs = [
        ((g, kp1, SSM_N), (kp1, SSM_N)), ((g, kp1, SSM_N), (kp1, SSM_N)),
        ((g, kp1, SSM_P, SSM_N), (kp1, SSM_P, SSM_N)), ((g, kp1, SSM_P, SSM_N), (kp1, SSM_P, SSM_N)),
        ((g, CHUNK, SSM_P, SSM_N), (CHUNK, SSM_P, SSM_N)), ((g, CHUNK, SSM_P, SSM_N), (CHUNK, SSM_P, SSM_N)),
        ((g, CP, SSM_P), (CP, SSM_P)),
    ]
    return pl.pallas_call(
        _ssm_param_kernel,
        grid=(g // gb,),
        in_specs=[row((1, SSM_N)), row((1, SSM_N)), row((1, 1)),
                  row((SSM_P, SSM_N)), row((SSM_P, SSM_N)), row((SSM_P, SSM_N)), row((SSM_P, SSM_N))],
        out_specs=[row(blk) for _, blk in outs],
        out_shape=[jax.ShapeDtypeStruct(full, F32) for full, _ in outs],
        compiler_params=_cparams("parallel"),
        name="ssm_params",
    )(lam_re.reshape(g, 1, SSM_N), lam_im.reshape(g, 1, SSM_N), log_step.reshape(g, 1, 1),
      jnp.swapaxes(b_re, 1, 2), jnp.swapaxes(b_im, 1, 2), c_re, c_im)


def _group_matmul_kernel(u_ref, w_ref, o_ref):
    for g in range(u_ref.shape[0]):
        o_ref[g] = jnp.dot(u_ref[g], w_ref[g], preferred_element_type=F32).astype(o_ref.dtype)


def _group_matmul(u, w, out_dtype, gb):
    g, m, k = u.shape
    n = w.shape[2]
    blk = lambda r, c: pl.BlockSpec((gb, r, c), lambda i: (i, 0, 0))
    return pl.pallas_call(
        _group_matmul_kernel,
        grid=(g // gb,),
        in_specs=[blk(m, k), blk(k, n)],
        out_specs=blk(m, n),
        out_shape=jax.ShapeDtypeStruct((g, m, n), out_dtype),
        compiler_params=_cparams("parallel"),
        name="ssm_group_matmul",
    )(u, w)


def _chunk_scan_kernel(v_ref, s0_ref, a1_ref, a2_ref, s_ref):
    a1, a2 = a1_ref[...], a2_ref[...]

    def body(j, s):
        s_ref[j] = s
        return a1 * s + a2 * pltpu.roll(s, SSM_N, 1) + v_ref[j]

    lax.fori_loop(0, v_ref.shape[0], body, s0_ref[...])


def _chunk_scan(v, s0, a1, a2, rb):
    j, r, n = v.shape
    seq = pl.BlockSpec((j, rb, n), lambda i: (0, i, 0))
    rows = pl.BlockSpec((rb, n), lambda i: (i, 0))
    return pl.pallas_call(
        _chunk_scan_kernel,
        grid=(r // rb,),
        in_specs=[seq, rows, rows, rows],
        out_specs=seq,
        out_shape=jax.ShapeDtypeStruct((j, r, n), F32),
        compiler_params=_cparams("parallel"),
        name="ssm_chunk_scan",
    )(v, s0, a1, a2)


def _ssm_out_kernel(u_ref, s_ref, wt_ref, ws_ref, y_ref):
    for g in range(u_ref.shape[0]):
        y = (jnp.dot(u_ref[g], wt_ref[g], preferred_element_type=F32)
             + jnp.dot(s_ref[g], ws_ref[g], preferred_element_type=F32))
        y_ref[g] = _gelu_tanh(y).astype(y_ref.dtype)


def _ssm_out(u, s, wt, ws, gb):
    g, m, k = u.shape
    blk = lambda r, c: pl.BlockSpec((gb, r, c), lambda i: (i, 0, 0))
    return pl.pallas_call(
        _ssm_out_kernel,
        grid=(g // gb,),
        in_specs=[blk(m, k), blk(m, s.shape[2]), blk(k, k), blk(s.shape[2], k)],
        out_specs=blk(m, k),
        out_shape=jax.ShapeDtypeStruct((g, m, k), BF16),
        compiler_params=_cparams("parallel"),
        name="ssm_out",
    )(u, s, wt, ws)


def _merge_kernel(attn_ref, y_ref, wap_ref, wv_ref, wg_ref, g0_ref, g1_ref, o_ref):
    a = jnp.dot(attn_ref[...], wap_ref[...], preferred_element_type=F32)
    y = y_ref[...]
    ssm = (jnp.dot(y, wv_ref[...], preferred_element_type=F32)
           * _sigmoid(jnp.dot(y, wg_ref[...], preferred_element_type=F32)))
    mixed = _sigmoid(g0_ref[...].astype(F32)) * a + _sigmoid(g1_ref[...].astype(F32)) * ssm
    o_ref[...] = mixed.astype(o_ref.dtype)


def _merge(attn, y, wap, wv, wg, zug, tm, tn):
    m = attn.shape[0]
    g0_off = SSM_WIDTH // tn
    g1_off = (SSM_WIDTH + D_MODEL) // tn
    return pl.pallas_call(
        _merge_kernel,
        grid=(m // tm, D_MODEL // tn),
        in_specs=[
            pl.BlockSpec((tm, attn.shape[1]), lambda i, j: (i, 0)),
            pl.BlockSpec((tm, SSM_WIDTH), lambda i, j: (i, 0)),
            pl.BlockSpec((attn.shape[1], tn), lambda i, j: (0, j)),
            pl.BlockSpec((SSM_WIDTH, tn), lambda i, j: (0, j)),
            pl.BlockSpec((SSM_WIDTH, tn), lambda i, j: (0, j)),
            pl.BlockSpec((tm, tn), lambda i, j: (i, g0_off + j)),
            pl.BlockSpec((tm, tn), lambda i, j: (i, g1_off + j)),
        ],
        out_specs=pl.BlockSpec((tm, tn), lambda i, j: (i, j)),
        out_shape=jax.ShapeDtypeStruct((m, D_MODEL), BF16),
        compiler_params=_cparams("parallel", "arbitrary"),
        name="gated_merge",
    )(attn, y, wap, wv, wg, zug, zug)


def _out_proj_kernel(x_ref, mx_ref, w_ref, g_ref, h_ref, n_ref):
    h = x_ref[...] + jnp.dot(mx_ref[...], w_ref[...], preferred_element_type=F32)
    h_ref[...] = h
    n_ref[...] = _rms(h, g_ref[...]).astype(n_ref.dtype)


def _out_proj(x, mixed, w, g, tm):
    m, d = x.shape
    row = pl.BlockSpec((tm, d), lambda i: (i, 0))
    return pl.pallas_call(
        _out_proj_kernel,
        grid=(m // tm,),
        in_specs=[row, row, pl.BlockSpec((d, d), lambda i: (0, 0)), pl.BlockSpec((1, d), lambda i: (0, 0))],
        out_specs=[row, row],
        out_shape=[jax.ShapeDtypeStruct((m, d), F32), jax.ShapeDtypeStruct((m, d), BF16)],
        compiler_params=_cparams("parallel"),
        name="out_proj",
    )(x, mixed, w, g)


def _ffn_kernel(n_ref, h_ref, wg_ref, wu_ref, wd_ref, gf_ref, o_ref, acc_ref):
    f = pl.program_id(1)
    n = n_ref[...]
    gate = jnp.dot(n, wg_ref[...], preferred_element_type=F32)
    up = jnp.dot(n, wu_ref[...], preferred_element_type=F32)
    act = (gate * _sigmoid(gate) * up).astype(BF16)
    part = jnp.dot(act, wd_ref[...], preferred_element_type=F32)

    @pl.when(f == 0)
    def _():
        acc_ref[...] = part

    @pl.when(f > 0)
    def _():
        acc_ref[...] += part

    @pl.when(f == pl.num_programs(1) - 1)
    def _():
        o_ref[...] = _rms(h_ref[...] + acc_ref[...], gf_ref[...])


def _ffn(n, h, wg, wu, wd, gf, tm, tf):
    m, d = h.shape
    row = lambda i, f: (i, 0)
    return pl.pallas_call(
        _ffn_kernel,
        grid=(m // tm, D_FF // tf),
        in_specs=[
            pl.BlockSpec((tm, d), row),
            pl.BlockSpec((tm, d), row),
            pl.BlockSpec((d, tf), lambda i, f: (0, f)),
            pl.BlockSpec((d, tf), lambda i, f: (0, f)),
            pl.BlockSpec((tf, d), lambda i, f: (f, 0)),
            pl.BlockSpec((1, d), lambda i, f: (0, 0)),
        ],
        out_specs=pl.BlockSpec((tm, d), row),
        out_shape=jax.ShapeDtypeStruct((m, d), F32),
        scratch_shapes=[pltpu.VMEM((tm, d), F32)],
        compiler_params=_cparams("parallel", "arbitrary"),
        name="ffn",
    )(n, h, wg, wu, wd, gf)


def _rope_tables(length):
    half = QK_ROPE // 2
    freqs = ROPE_THETA ** (-jnp.arange(half, dtype=F32) / half)
    ang = jnp.arange(length, dtype=F32)[:, None] * freqs[None, :]
    c, s = jnp.cos(ang), jnp.sin(ang)
    z = jnp.zeros_like(c)
    cosf = jnp.concatenate([c, c, z, z], axis=1)
    sin1 = jnp.concatenate([-s, z, z, z], axis=1)
    sin2 = jnp.concatenate([z, s, z, z], axis=1)
    return cosf, sin1, sin2


def _chunk_operators(e_re, e_im, ce_re, ce_im, eb_re, eb_im, t, d):
    g = SSM_G
    w_in = jnp.concatenate([eb_re[:, ::-1], eb_im[:, ::-1]], axis=-1).reshape(g, CP, 2 * SSM_N)
    st = jnp.concatenate([ce_re[:, 1:], -ce_im[:, 1:]], axis=-1)
    w_st = st.transpose(0, 3, 1, 2).reshape(g, 2 * SSM_N, CP)
    t = t.reshape(g, CHUNK, SSM_P, SSM_P).at[:, 0].add(jax.vmap(jnp.diag)(d))
    sig = jnp.arange(CHUNK)[:, None]
    tau = jnp.arange(CHUNK)[None, :]
    blocks = t[:, jnp.clip(tau - sig, 0, CHUNK - 1)]
    blocks = jnp.where((tau >= sig)[None, :, :, None, None], blocks, 0.0)
    w_t = blocks.transpose(0, 1, 4, 2, 3).reshape(g, CP, CP)
    a1 = jnp.concatenate([e_re[:, CHUNK], e_re[:, CHUNK]], axis=-1)
    a2 = jnp.concatenate([-e_im[:, CHUNK], e_im[:, CHUNK]], axis=-1)
    return w_in.astype(BF16), w_st.astype(BF16), w_t.astype(BF16), a1, a2


def kernel(x, meta_tokens, norm_mix, w_in, norm_q, w_q_up, norm_kv, w_kv_up, w_attn_proj, ssm_lambda_re, ssm_lambda_im, ssm_log_step, ssm_b_re, ssm_b_im, ssm_c_re, ssm_c_im, ssm_d, w_glu_val, w_glu_gate, w_out, norm_ffn, w_ffn_gate, w_ffn_up, w_ffn_down, norm_final):
    bsz, seq, d = x.shape
    assert d == D_MODEL and w_in.shape[0] == 1 and seq % 512 == 0 and seq % CHUNK == 0
    t_rows = bsz * seq
    n_chunks = seq // CHUNK
    x2 = x.reshape(t_rows, d)
    meta = meta_tokens.astype(x.dtype)

    off_u = Q_LORA + KV_LORA + QK_ROPE
    w_lat = jnp.pad(w_in[0][:, :off_u], ((0, 0), (0, LAT_W - off_u))).astype(BF16)
    w_ug = w_in[0][:, off_u:].astype(BF16)
    scale = (QK_NOPE + QK_ROPE) ** -0.5
    wq = jnp.pad((w_q_up[0] * scale).reshape(Q_LORA, HEADS, QK_NOPE + QK_ROPE),
                 ((0, 0), (0, 0), (0, HEAD_PAD - QK_NOPE - QK_ROPE))).reshape(Q_LORA, HEADS * HEAD_PAD).astype(BF16)
    wkv = w_kv_up[0].reshape(KV_LORA, HEADS, QK_NOPE + V_HEAD)
    wk = wkv[:, :, :QK_NOPE].reshape(KV_LORA, HEADS * QK_NOPE).astype(BF16)
    wv = wkv[:, :, QK_NOPE:].reshape(KV_LORA, HEADS * V_HEAD).astype(BF16)
    g_mix = norm_mix[0].reshape(1, d)
    g_q = norm_q[0].reshape(1, Q_LORA)
    g_kv = norm_kv[0].reshape(1, KV_LORA)
    cosf, sin1, sin2 = _rope_tables(N_META + seq)

    lat = _norm_matmul(x2, g_mix, w_lat, F32, 512, LAT_W)
    zug = _norm_matmul(x2, g_mix, w_ug, BF16, 1024, 1024)
    lat_m = _norm_matmul(meta, g_mix, w_lat, F32, N_META, LAT_W)
    u_m = _norm_matmul(meta, g_mix, w_ug[:, :SSM_WIDTH], BF16, N_META, SSM_WIDTH)

    q, k, v = _up_rope(lat, g_q, g_kv, wq, wk, wv, cosf[N_META:], sin1[N_META:], sin2[N_META:], 512)
    _, k_m, v_m = _up_rope(lat_m, g_q, g_kv, wq, wk, wv, cosf[:N_META], sin1[:N_META], sin2[:N_META], N_META)
    k_m = jnp.pad(k_m, ((0, 0), (0, LANES - N_META), (0, 0)))
    v_m = jnp.pad(v_m, ((0, 0), (0, LANES - N_META), (0, 0)))
    attn = _attention(q, k, v, k_m, v_m, bsz, seq, 512)

    e_re, e_im, ce_re, ce_im, eb_re, eb_im, t_blk = _ssm_params(
        ssm_lambda_re[0], ssm_lambda_im[0], ssm_log_step[0], ssm_b_re[0], ssm_b_im[0],
        ssm_c_re[0], ssm_c_im[0], 8)
    w_sin, w_sst, w_toe, a1, a2 = _chunk_operators(e_re, e_im, ce_re, ce_im, eb_re, eb_im, t_blk, ssm_d[0])
    u = zug[:, :SSM_WIDTH].reshape(bsz, n_chunks, CHUNK, SSM_G, SSM_P)
    u = u.transpose(3, 0, 1, 2, 4).reshape(SSM_G, bsz * n_chunks, CP)
    u_meta = jnp.pad(u_m.reshape(CHUNK, SSM_G, SSM_P).transpose(1, 0, 2).reshape(SSM_G, 1, CP),
                     ((0, 0), (0, 7), (0, 0)))
    s_init = _group_matmul(u_meta, w_sin, F32, 8)[:, 0]
    vin = _group_matmul(u, w_sin, F32, 8)
    vin = vin.reshape(SSM_G, bsz, n_chunks, 2 * SSM_N).transpose(2, 1, 0, 3).reshape(n_chunks, bsz * SSM_G, 2 * SSM_N)
    tile_b = lambda a: jnp.tile(a, (bsz, 1))
    states = _chunk_scan(vin, tile_b(s_init), tile_b(a1), tile_b(a2), 64)
    states = states.reshape(n_chunks, bsz, SSM_G, 2 * SSM_N).transpose(2, 1, 0, 3)
    states = states.reshape(SSM_G, bsz * n_chunks, 2 * SSM_N).astype(BF16)
    yg = _ssm_out(u, states, w_toe, w_sst, 8)
    y = yg.reshape(SSM_G, bsz, n_chunks, CHUNK, SSM_P).transpose(1, 2, 3, 0, 4).reshape(t_rows, SSM_WIDTH)

    mixed = _merge(attn, y, w_attn_proj[0].astype(BF16), w_glu_val[0].astype(BF16),
                   w_glu_gate[0].astype(BF16), zug, 1024, 512)
    h1, n_ffn = _out_proj(x2, mixed, w_out[0].astype(BF16), norm_ffn[0].reshape(1, d), 512)
    out = _ffn(n_ffn, h1, w_ffn_gate[0].astype(BF16), w_ffn_up[0].astype(BF16),
               w_ffn_down[0].astype(BF16), norm_final.reshape(1, d), 512, 512)
    return out.reshape(bsz, seq, d)
```

```python
import math

import jax
import jax.numpy as jnp
from jax import lax
from jax.experimental import pallas as pl
from jax.experimental.pallas import tpu as pltpu

F32 = jnp.float32
BF16 = jnp.bfloat16

D_MODEL = 2048
N_META = 16
EPS = 1e-6
HEADS = 16
Q_LORA = 512
KV_LORA = 512
QK_NOPE = 128
QK_ROPE = 64
V_HEAD = 128
ROPE_THETA = 10000.0
SSM_WIDTH = 1024
SSM_P = 16
SSM_G = 64
SSM_N = 64
D_FF = 5632
LANES = 128
HEAD_PAD = 2 * LANES
LAT_W = Q_LORA + KV_LORA + LANES
CHUNK = 16
CP = CHUNK * SSM_P
GPT = LANES // SSM_P
N_TILE = SSM_G // GPT
CW = CHUNK * LANES
SW = GPT * 2 * SSM_N
META_ROWS = 128
VMEM_LIMIT = 56 * 1024 * 1024


def _cparams(*sem):
    return pltpu.CompilerParams(dimension_semantics=sem, vmem_limit_bytes=VMEM_LIMIT)


def _sigmoid(x):
    return 1.0 / (1.0 + jnp.exp(-x))


def _gelu_tanh(x):
    return x * (0.5 * (1.0 + jnp.tanh(math.sqrt(2.0 / math.pi) * (x + 0.044715 * (x * x * x)))))


def _rms(x, g):
    ms = jnp.mean(x * x, axis=-1, keepdims=True)
    return x * lax.rsqrt(ms + EPS) * g


def _norm_matmul_kernel(x_ref, g_ref, w_ref, o_ref, n_ref):
    @pl.when(pl.program_id(1) == 0)
    def _():
        n_ref[...] = _rms(x_ref[...], g_ref[...]).astype(BF16)

    o_ref[...] = jnp.dot(n_ref[...], w_ref[...], preferred_element_type=F32).astype(o_ref.dtype)


def _norm_matmul(x, g, w, out_dtype, tm, tn):
    m, k = x.shape
    n = w.shape[1]
    return pl.pallas_call(
        _norm_matmul_kernel,
        grid=(m // tm, n // tn),
        in_specs=[
            pl.BlockSpec((tm, k), lambda i, j: (i, 0)),
            pl.BlockSpec((1, k), lambda i, j: (0, 0)),
            pl.BlockSpec((k, tn), lambda i, j: (0, j)),
        ],
        out_specs=pl.BlockSpec((tm, tn), lambda i, j: (i, j)),
        out_shape=jax.ShapeDtypeStruct((m, n), out_dtype),
        scratch_shapes=[pltpu.VMEM((tm, k), BF16)],
        compiler_params=_cparams("parallel", "arbitrary"),
        name="norm_matmul",
    )(x, g, w)


def _rope_slab(x, cosf, sin1, sin2):
    return (x * cosf + pltpu.roll(x, LANES - QK_ROPE // 2, 1) * sin1
            + pltpu.roll(x, QK_ROPE // 2, 1) * sin2)


def _up_rope_kernel(ql_ref, kvl_ref, kr_ref, gq_ref, gkv_ref, wq_ref, wk_ref, wv_ref,
                    cos_ref, sin1_ref, sin2_ref, q_ref, k_ref, vt_ref):
    cosf, sin1, sin2 = cos_ref[...], sin1_ref[...], sin2_ref[...]
    qn = _rms(ql_ref[...], gq_ref[...]).astype(BF16)
    kvn = _rms(kvl_ref[...], gkv_ref[...]).astype(BF16)
    k_rope = _rope_slab(kr_ref[...], cosf, sin1, sin2).astype(BF16)
    for h in range(HEADS):
        acc = jnp.dot(qn, wq_ref[:, h * HEAD_PAD:(h + 1) * HEAD_PAD], preferred_element_type=F32)
        q_ref[h, :, :LANES] = acc[:, :LANES].astype(BF16)
        q_ref[h, :, LANES:] = _rope_slab(acc[:, LANES:], cosf, sin1, sin2).astype(BF16)
    for hp in range(HEADS // 2):
        acc_k = jnp.dot(kvn, wk_ref[:, hp * 256:(hp + 1) * 256], preferred_element_type=F32)
        acc_v = jnp.dot(kvn, wv_ref[:, hp * 256:(hp + 1) * 256], preferred_element_type=F32)
        acc_vt = acc_v.T
        for s in range(2):
            h = 2 * hp + s
            k_ref[h, :, :LANES] = acc_k[:, s * LANES:(s + 1) * LANES].astype(BF16)
            k_ref[h, :, LANES:] = k_rope
            vt_ref[h] = acc_vt[s * V_HEAD:(s + 1) * V_HEAD].astype(BF16)


def _up_rope(lat, gq, gkv, wq, wk, wv, cosf, sin1, sin2, tm):
    m = lat.shape[0]
    n_pos_blocks = cosf.shape[0] // tm
    row = lambda i: (i, 0)
    const = lambda i: (0, 0)
    pos = lambda i: (i % n_pos_blocks, 0)
    head_major = lambda i: (0, i, 0)
    return pl.pallas_call(
        _up_rope_kernel,
        grid=(m // tm,),
        in_specs=[
            pl.BlockSpec((tm, Q_LORA), row),
            pl.BlockSpec((tm, KV_LORA), lambda i: (i, 1)),
            pl.BlockSpec((tm, LANES), lambda i: (i, (Q_LORA + KV_LORA) // LANES)),
            pl.BlockSpec((1, Q_LORA), const),
            pl.BlockSpec((1, KV_LORA), const),
            pl.BlockSpec(wq.shape, const),
            pl.BlockSpec(wk.shape, const),
            pl.BlockSpec(wv.shape, const),
            pl.BlockSpec((tm, LANES), pos),
            pl.BlockSpec((tm, LANES), pos),
            pl.BlockSpec((tm, LANES), pos),
        ],
        out_specs=[
            pl.BlockSpec((HEADS, tm, HEAD_PAD), head_major),
            pl.BlockSpec((HEADS, tm, HEAD_PAD), head_major),
            pl.BlockSpec((HEADS, V_HEAD, tm), lambda i: (0, 0, i)),
        ],
        out_shape=[
            jax.ShapeDtypeStruct((HEADS, m, HEAD_PAD), BF16),
            jax.ShapeDtypeStruct((HEADS, m, HEAD_PAD), BF16),
            jax.ShapeDtypeStruct((HEADS, V_HEAD, m), BF16),
        ],
        compiler_params=_cparams("parallel"),
        name="up_rope",
    )(lat, lat, lat, gq, gkv, wq, wk, wv, cosf, sin1, sin2)


def _attn_kernel(qi_ref, ki_ref, q_ref, k_ref, vt_ref, km_ref, vmt_ref, o_ref, m_ref, l_ref, acc_ref):
    t = pl.program_id(1)
    qi = qi_ref[t]
    ki = ki_ref[t]
    nt = (((1,), (1,)), ((), ()))

    @pl.when(ki == 0)
    def _init():
        def body(h, c):
            s = lax.dot_general(km_ref[h], q_ref[h], nt, preferred_element_type=F32)
            key = lax.broadcasted_iota(jnp.int32, s.shape, 0)
            s = jnp.where(key < N_META, s, -jnp.inf)
            m = jnp.max(s, axis=0, keepdims=True)
            p = jnp.exp(s - m)
            m_ref[h] = m
            l_ref[h] = jnp.sum(p, axis=0, keepdims=True)
            acc_ref[h] = jnp.dot(vmt_ref[h], p.astype(BF16), preferred_element_type=F32)
            return c
        lax.fori_loop(0, HEADS, body, 0, unroll=2)

    def tile(masked):
        def body(h, c):
            s = lax.dot_general(k_ref[h], q_ref[h], nt, preferred_element_type=F32)
            if masked:
                key = lax.broadcasted_iota(jnp.int32, s.shape, 0)
                qry = lax.broadcasted_iota(jnp.int32, s.shape, 1)
                s = jnp.where(key <= qry, s, -jnp.inf)
            m_prev = m_ref[h]
            m_new = jnp.maximum(m_prev, jnp.max(s, axis=0, keepdims=True))
            alpha = jnp.exp(m_prev - m_new)
            p = jnp.exp(s - m_new)
            l_ref[h] = alpha * l_ref[h] + jnp.sum(p, axis=0, keepdims=True)
            acc_ref[h] = alpha * acc_ref[h] + jnp.dot(vt_ref[h], p.astype(BF16),
                                                      preferred_element_type=F32)
            m_ref[h] = m_new
            return c
        lax.fori_loop(0, HEADS, body, 0, unroll=2)

    @pl.when(ki < qi)
    def _full():
        tile(False)

    @pl.when(ki == qi)
    def _diag():
        tile(True)
        for h in range(HEADS):
            o_ref[:, h * V_HEAD:(h + 1) * V_HEAD] = (acc_ref[h] / l_ref[h]).T.astype(o_ref.dtype)


def _attention(q, k, vt, km, vmt, bsz, seq, tq):
    nq = seq // tq
    pairs = [(a, b) for a in range(nq) for b in range(a + 1)]
    qi_arr = jnp.asarray([p[0] for p in pairs], jnp.int32)
    ki_arr = jnp.asarray([p[1] for p in pairs], jnp.int32)
    grid_spec = pltpu.PrefetchScalarGridSpec(
        num_scalar_prefetch=2,
        grid=(bsz, len(pairs)),
        in_specs=[
            pl.BlockSpec((HEADS, tq, HEAD_PAD), lambda b, t, qi, ki: (0, b * nq + qi[t], 0)),
            pl.BlockSpec((HEADS, tq, HEAD_PAD), lambda b, t, qi, ki: (0, b * nq + ki[t], 0)),
            pl.BlockSpec((HEADS, V_HEAD, tq), lambda b, t, qi, ki: (0, 0, b * nq + ki[t])),
            pl.BlockSpec(km.shape, lambda b, t, qi, ki: (0, 0, 0)),
            pl.BlockSpec(vmt.shape, lambda b, t, qi, ki: (0, 0, 0)),
        ],
        out_specs=pl.BlockSpec((tq, HEADS * V_HEAD), lambda b, t, qi, ki: (b * nq + qi[t], 0)),
        scratch_shapes=[
            pltpu.VMEM((HEADS, 1, tq), F32),
            pltpu.VMEM((HEADS, 1, tq), F32),
            pltpu.VMEM((HEADS, V_HEAD, tq), F32),
        ],
    )
    return pl.pallas_call(
        _attn_kernel,
        grid_spec=grid_spec,
        out_shape=jax.ShapeDtypeStruct((bsz * seq, HEADS * V_HEAD), BF16),
        compiler_params=_cparams("parallel", "arbitrary"),
        name="flash_attn",
    )(qi_arr, ki_arr, q, k, vt, km, vmt)


def _ssm_param_kernel(lr_ref, li_ref, ls_ref, bre_ref, bim_ref, cre_ref, cim_ref,
                      e_re_ref, e_im_ref, ce_re_ref, ce_im_ref, eb_re_ref, eb_im_ref, t_ref):
    nt = (((1,), (1,)), ((), ()))
    for g in range(lr_ref.shape[0]):
        lr, li = lr_ref[g], li_ref[g]
        step = jnp.exp(ls_ref[g])
        kf = lax.broadcasted_iota(jnp.int32, (CHUNK + 1, SSM_N), 0).astype(F32)
        mag = jnp.exp(kf * (lr * step))
        ang = kf * (li * step)
        e_re, e_im = mag * jnp.cos(ang), mag * jnp.sin(ang)
        e_re_ref[g], e_im_ref[g] = e_re, e_im
        nr, ni = e_re[1:2] - 1.0, e_im[1:2]
        den = lr * lr + li * li
        coef_re = (nr * lr + ni * li) / den
        coef_im = (ni * lr - nr * li) / den
        bre, bim = bre_ref[g], bim_ref[g]
        bb_re = coef_re * bre - coef_im * bim
        bb_im = coef_re * bim + coef_im * bre
        cre, cim = cre_ref[g], cim_ref[g]
        ce_re_lags, ce_im_lags = [], []
        for k in range(CHUNK + 1):
            er, ei = e_re[k:k + 1], e_im[k:k + 1]
            ce_re = cre * er - cim * ei
            ce_im = cre * ei + cim * er
            ce_re_ref[g, k], ce_im_ref[g, k] = ce_re, ce_im
            if k < CHUNK:
                ce_re_lags.append(ce_re)
                ce_im_lags.append(ce_im)
                eb_re_ref[g, k] = er * bb_re - ei * bb_im
                eb_im_ref[g, k] = er * bb_im + ei * bb_re
        t_ref[g] = (
            lax.dot_general(jnp.concatenate(ce_re_lags, axis=0), bb_re, nt,
                            precision=lax.Precision.HIGHEST, preferred_element_type=F32)
            - lax.dot_general(jnp.concatenate(ce_im_lags, axis=0), bb_im, nt,
                              precision=lax.Precision.HIGHEST, preferred_element_type=F32))


def _ssm_params(lam_re, lam_im, log_step, b_re, b_im, c_re, c_im, gb):
    g = SSM_G
    kp1 = CHUNK + 1
    row = lambda shape: pl.BlockSpec((gb,) + shape, lambda i: (i,) + (0,) * len(shape))
    outs = [
        ((g, kp1, SSM_N), (kp1, SSM_N)), ((g, kp1, SSM_N), (kp1, SSM_N)),
        ((g, kp1, SSM_P, SSM_N), (kp1, SSM_P, SSM_N)), ((g, kp1, SSM_P, SSM_N), (kp1, SSM_P, SSM_N)),
        ((g, CHUNK, SSM_P, SSM_N), (CHUNK, SSM_P, SSM_N)), ((g, CHUNK, SSM_P, SSM_N), (CHUNK, SSM_P, SSM_N)),
        ((g, CP, SSM_P), (CP, SSM_P)),
    ]
    return pl.pallas_call(
        _ssm_param_kernel,
        grid=(g // gb,),
        in_specs=[row((1, SSM_N)), row((1, SSM_N)), row((1, 1)),
                  row((SSM_P, SSM_N)), row((SSM_P, SSM_N)), row((SSM_P, SSM_N)), row((SSM_P, SSM_N))],
        out_specs=[row(blk) for _, blk in outs],
        out_shape=[jax.ShapeDtypeStruct(full, F32) for full, _ in outs],
        compiler_params=_cparams("parallel"),
        name="ssm_params",
    )(lam_re.reshape(g, 1, SSM_N), lam_im.reshape(g, 1, SSM_N), log_step.reshape(g, 1, 1),
      jnp.swapaxes(b_re, 1, 2), jnp.swapaxes(b_im, 1, 2), c_re, c_im)


def _ssm_in_kernel(u_ref, wc_ref, v_ref, w_ref):
    w_ref[...] = jnp.zeros(w_ref.shape, w_ref.dtype)
    for sig in range(CHUNK):
        for g in range(GPT):
            r0 = sig * LANES + g * SSM_P
            w_ref[r0:r0 + SSM_P, g * 2 * SSM_N:(g + 1) * 2 * SSM_N] = wc_ref[g, sig * SSM_P:(sig + 1) * SSM_P, :]
    v_ref[0] = jnp.dot(u_ref[0], w_ref[...], preferred_element_type=F32)


def _ssm_in(u, w_compact):
    nt, m, k = u.shape
    return pl.pallas_call(
        _ssm_in_kernel,
        grid=(nt,),
        in_specs=[pl.BlockSpec((1, m, k), lambda i: (i, 0, 0)),
                  pl.BlockSpec((GPT, CP, 2 * SSM_N), lambda i: (i, 0, 0))],
        out_specs=pl.BlockSpec((1, m, SW), lambda i: (i, 0, 0)),
        out_shape=jax.ShapeDtypeStruct((nt, m, SW), F32),
        scratch_shapes=[pltpu.VMEM((CW, SW), BF16)],
        compiler_params=_cparams("parallel"),
        name="ssm_in",
    )(u, w_compact)


def _chunk_scan_kernel(v_ref, s0_ref, a1_ref, a2_ref, s_ref):
    a1, a2 = a1_ref[...], a2_ref[...]

    def body(j, s):
        s_ref[j] = s
        return a1 * s + a2 * pltpu.roll(s, SSM_N, 1) + v_ref[j]

    lax.fori_loop(0, v_ref.shape[0], body, s0_ref[...])


def _chunk_scan(v, s0, a1, a2, rb):
    j, r, n = v.shape
    seq = pl.BlockSpec((j, rb, n), lambda i: (0, i, 0))
    rows = pl.BlockSpec((rb, n), lambda i: (i, 0))
    return pl.pallas_call(
        _chunk_scan_kernel,
        grid=(r // rb,),
        in_specs=[seq, rows, rows, rows],
        out_specs=seq,
        out_shape=jax.ShapeDtypeStruct((j, r, n), F32),
        compiler_params=_cparams("parallel"),
        name="ssm_chunk_scan",
    )(v, s0, a1, a2)


def _ssm_out_kernel(u_ref, s_ref, tc_ref, ce_ref, y_ref, wt_ref, ws_ref):
    row_g = lax.shift_right_logical(lax.broadcasted_iota(jnp.int32, (LANES, LANES), 0), 4)
    col_g = lax.shift_right_logical(lax.broadcasted_iota(jnp.int32, (LANES, LANES), 1), 4)
    same_group = row_g == col_g
    zero_blk = jnp.zeros((LANES, LANES), BF16)
    lag_blocks = []
    for k in range(CHUNK):
        rep = jnp.concatenate([tc_ref[0, k]] * GPT, axis=0)
        lag_blocks.append(jnp.where(same_group, rep, 0.0).astype(BF16))
    for sig in range(CHUNK):
        for tau in range(CHUNK):
            blk = lag_blocks[tau - sig] if tau >= sig else zero_blk
            wt_ref[sig * LANES:(sig + 1) * LANES, tau * LANES:(tau + 1) * LANES] = blk
    for g in range(GPT):
        own = col_g == g
        for tau in range(CHUNK):
            ws_ref[g * LANES:(g + 1) * LANES, tau * LANES:(tau + 1) * LANES] = (
                jnp.where(own, ce_ref[0, tau], 0.0).astype(BF16))
    y = (jnp.dot(u_ref[0], wt_ref[...], preferred_element_type=F32)
         + jnp.dot(s_ref[0], ws_ref[...], preferred_element_type=F32))
    y_ref[0] = _gelu_tanh(y).astype(y_ref.dtype)


def _ssm_out(u, s, t_compact, ce_compact, rows):
    nt = u.shape[0]
    return pl.pallas_call(
        _ssm_out_kernel,
        grid=(nt,),
        in_specs=[pl.BlockSpec((1, rows, CW), lambda i: (i, 0, 0)),
                  pl.BlockSpec((1, rows, SW), lambda i: (i, 0, 0)),
                  pl.BlockSpec((1, CHUNK, SSM_P, LANES), lambda i: (i, 0, 0, 0)),
                  pl.BlockSpec((1, CHUNK, 2 * SSM_N, LANES), lambda i: (i, 0, 0, 0))],
        out_specs=pl.BlockSpec((1, rows, CW), lambda i: (i, 0, 0)),
        out_shape=jax.ShapeDtypeStruct((nt, rows, CW), BF16),
        scratch_shapes=[pltpu.VMEM((CW, CW), BF16), pltpu.VMEM((SW, CW), BF16)],
        compiler_params=_cparams("parallel"),
        name="ssm_out",
    )(u, s, t_compact, ce_compact)


def _merge_kernel(attn_ref, y_ref, wap_ref, wv_ref, wg_ref, g0_ref, g1_ref, o_ref):
    a = jnp.dot(attn_ref[...], wap_ref[...], preferred_element_type=F32)
    y = y_ref[...]
    ssm = (jnp.dot(y, wv_ref[...], preferred_element_type=F32)
           * _sigmoid(jnp.dot(y, wg_ref[...], preferred_element_type=F32)))
    mixed = _sigmoid(g0_ref[...].astype(F32)) * a + _sigmoid(g1_ref[...].astype(F32)) * ssm
    o_ref[...] = mixed.astype(o_ref.dtype)


def _merge(attn, y, wap, wv, wg, zug, tm, tn):
    m = attn.shape[0]
    g0_off = SSM_WIDTH // tn
    g1_off = (SSM_WIDTH + D_MODEL) // tn
    return pl.pallas_call(
        _merge_kernel,
        grid=(m // tm, D_MODEL // tn),
        in_specs=[
            pl.BlockSpec((tm, attn.shape[1]), lambda i, j: (i, 0)),
            pl.BlockSpec((tm, SSM_WIDTH), lambda i, j: (i, 0)),
            pl.BlockSpec((attn.shape[1], tn), lambda i, j: (0, j)),
            pl.BlockSpec((SSM_WIDTH, tn), lambda i, j: (0, j)),
            pl.BlockSpec((SSM_WIDTH, tn), lambda i, j: (0, j)),
            pl.BlockSpec((tm, tn), lambda i, j: (i, g0_off + j)),
            pl.BlockSpec((tm, tn), lambda i, j: (i, g1_off + j)),
        ],
        out_specs=pl.BlockSpec((tm, tn), lambda i, j: (i, j)),
        out_shape=jax.ShapeDtypeStruct((m, D_MODEL), BF16),
        compiler_params=_cparams("parallel", "arbitrary"),
        name="gated_merge",
    )(attn, y, wap, wv, wg, zug, zug)


def _out_proj_kernel(x_ref, mx_ref, w_ref, g_ref, h_ref, n_ref):
    h = x_ref[...] + jnp.dot(mx_ref[...], w_ref[...], preferred_element_type=F32)
    h_ref[...] = h
    n_ref[...] = _rms(h, g_ref[...]).astype(n_ref.dtype)


def _out_proj(x, mixed, w, g, tm):
    m, d = x.shape
    row = pl.BlockSpec((tm, d), lambda i: (i, 0))
    return pl.pallas_call(
        _out_proj_kernel,
        grid=(m // tm,),
        in_specs=[row, row, pl.BlockSpec((d, d), lambda i: (0, 0)), pl.BlockSpec((1, d), lambda i: (0, 0))],
        out_specs=[row, row],
        out_shape=[jax.ShapeDtypeStruct((m, d), F32), jax.ShapeDtypeStruct((m, d), BF16)],
        compiler_params=_cparams("parallel"),
        name="out_proj",
    )(x, mixed, w, g)


def _ffn_kernel(n_ref, h_ref, wg_ref, wu_ref, wd_ref, gf_ref, o_ref, acc_ref):
    f = pl.program_id(1)
    n = n_ref[...]
    gate = jnp.dot(n, wg_ref[...], preferred_element_type=F32)
    up = jnp.dot(n, wu_ref[...], preferred_element_type=F32)
    act = (gate * _sigmoid(gate) * up).astype(BF16)
    part = jnp.dot(act, wd_ref[...], preferred_element_type=F32)

    @pl.when(f == 0)
    def _():
        acc_ref[...] = part

    @pl.when(f > 0)
    def _():
        acc_ref[...] += part

    @pl.when(f == pl.num_programs(1) - 1)
    def _():
        o_ref[...] = _rms(h_ref[...] + acc_ref[...], gf_ref[...])


def _ffn(n, h, wg, wu, wd, gf, tm, tf):
    m, d = h.shape
    row = lambda i, f: (i, 0)
    return pl.pallas_call(
        _ffn_kernel,
        grid=(m // tm, D_FF // tf),
        in_specs=[
            pl.BlockSpec((tm, d), row),
            pl.BlockSpec((tm, d), row),
            pl.BlockSpec((d, tf), lambda i, f: (0, f)),
            pl.BlockSpec((d, tf), lambda i, f: (0, f)),
            pl.BlockSpec((tf, d), lambda i, f: (f, 0)),
            pl.BlockSpec((1, d), lambda i, f: (0, 0)),
        ],
        out_specs=pl.BlockSpec((tm, d), row),
        out_shape=jax.ShapeDtypeStruct((m, d), F32),
        scratch_shapes=[pltpu.VMEM((tm, d), F32)],
        compiler_params=_cparams("parallel", "arbitrary"),
        name="ffn",
    )(n, h, wg, wu, wd, gf)


def _rope_tables(length):
    half = QK_ROPE // 2
    freqs = ROPE_THETA ** (-jnp.arange(half, dtype=F32) / half)
    ang = jnp.arange(length, dtype=F32)[:, None] * freqs[None, :]
    c, s = jnp.cos(ang), jnp.sin(ang)
    z = jnp.zeros_like(c)
    cosf = jnp.concatenate([c, c, z, z], axis=1)
    sin1 = jnp.concatenate([-s, z, z, z], axis=1)
    sin2 = jnp.concatenate([z, s, z, z], axis=1)
    return cosf, sin1, sin2


def _chunk_operators(e_re, e_im, ce_re, ce_im, eb_re, eb_im, t, d):
    g = SSM_G
    w_in = jnp.concatenate([eb_re[:, ::-1], eb_im[:, ::-1]], axis=-1).reshape(g, CP, 2 * SSM_N)
    st = jnp.concatenate([ce_re[:, 1:], -ce_im[:, 1:]], axis=-1)
    ce_t = st.reshape(N_TILE, GPT, CHUNK, SSM_P, 2 * SSM_N).transpose(0, 2, 4, 1, 3)
    ce_t = ce_t.reshape(N_TILE, CHUNK, 2 * SSM_N, LANES)
    t = t.reshape(g, CHUNK, SSM_P, SSM_P).at[:, 0].add(jax.vmap(jnp.diag)(d))
    t_c = t.reshape(N_TILE, GPT, CHUNK, SSM_P, SSM_P).transpose(0, 2, 4, 1, 3)
    t_c = t_c.reshape(N_TILE, CHUNK, SSM_P, LANES)
    a1 = jnp.concatenate([e_re[:, CHUNK], e_re[:, CHUNK]], axis=-1)
    a2 = jnp.concatenate([-e_im[:, CHUNK], e_im[:, CHUNK]], axis=-1)
    return w_in.astype(BF16), ce_t, t_c, a1, a2


def kernel(x, meta_tokens, norm_mix, w_in, norm_q, w_q_up, norm_kv, w_kv_up, w_attn_proj, ssm_lambda_re, ssm_lambda_im, ssm_log_step, ssm_b_re, ssm_b_im, ssm_c_re, ssm_c_im, ssm_d, w_glu_val, w_glu_gate, w_out, norm_ffn, w_ffn_gate, w_ffn_up, w_ffn_down, norm_final):
    bsz, seq, d = x.shape
    assert d == D_MODEL and w_in.shape[0] == 1 and seq % 512 == 0 and N_META == CHUNK
    t_rows = bsz * seq
    n_chunks = seq // CHUNK
    bj = bsz * n_chunks
    x2 = x.reshape(t_rows, d)
    meta = jnp.pad(meta_tokens.astype(x.dtype), ((0, META_ROWS - N_META), (0, 0)))

    off_u = Q_LORA + KV_LORA + QK_ROPE
    w_lat = jnp.pad(w_in[0][:, :off_u], ((0, 0), (0, LAT_W - off_u))).astype(BF16)
    w_ug = w_in[0][:, off_u:].astype(BF16)
    scale = (QK_NOPE + QK_ROPE) ** -0.5
    wq = jnp.pad((w_q_up[0] * scale).reshape(Q_LORA, HEADS, QK_NOPE + QK_ROPE),
                 ((0, 0), (0, 0), (0, HEAD_PAD - QK_NOPE - QK_ROPE))).reshape(Q_LORA, HEADS * HEAD_PAD).astype(BF16)
    wkv = w_kv_up[0].reshape(KV_LORA, HEADS, QK_NOPE + V_HEAD)
    wk = wkv[:, :, :QK_NOPE].reshape(KV_LORA, HEADS * QK_NOPE).astype(BF16)
    wv = wkv[:, :, QK_NOPE:].reshape(KV_LORA, HEADS * V_HEAD).astype(BF16)
    g_mix = norm_mix[0].reshape(1, d)
    g_q = norm_q[0].reshape(1, Q_LORA)
    g_kv = norm_kv[0].reshape(1, KV_LORA)
    cosf, sin1, sin2 = _rope_tables(N_META + seq)

    lat = _norm_matmul(x2, g_mix, w_lat, F32, 512, LAT_W)
    zug = _norm_matmul(x2, g_mix, w_ug, BF16, 1024, 1024)
    lat_m = _norm_matmul(meta, g_mix, w_lat, F32, META_ROWS, LAT_W)
    u_m = _norm_matmul(meta, g_mix, w_ug[:, :SSM_WIDTH], BF16, META_ROWS, SSM_WIDTH)

    q, k, vt = _up_rope(lat, g_q, g_kv, wq, wk, wv, cosf[N_META:], sin1[N_META:], sin2[N_META:], 512)
    _, k_m, vt_m = _up_rope(lat_m, g_q, g_kv, wq, wk, wv, cosf[:META_ROWS], sin1[:META_ROWS],
                            sin2[:META_ROWS], META_ROWS)
    attn = _attention(q, k, vt, k_m, vt_m, bsz, seq, 512)

    e_re, e_im, ce_re, ce_im, eb_re, eb_im, t_blk = _ssm_params(
        ssm_lambda_re[0], ssm_lambda_im[0], ssm_log_step[0], ssm_b_re[0], ssm_b_im[0],
        ssm_c_re[0], ssm_c_im[0], 8)
    w_sin, ce_t, t_c, a1, a2 = _chunk_operators(e_re, e_im, ce_re, ce_im, eb_re, eb_im, t_blk, ssm_d[0])
    u_seq = zug[:, :SSM_WIDTH].reshape(bj, CHUNK, N_TILE, LANES)
    u_meta = jnp.pad(u_m[:N_META].reshape(1, CHUNK, N_TILE, LANES), ((0, 15), (0, 0), (0, 0), (0, 0)))
    u_all = jnp.concatenate([u_seq, u_meta], axis=0).transpose(2, 0, 1, 3).reshape(N_TILE, bj + 16, CW)
    vin = _ssm_in(u_all, w_sin)
    s_init = vin[:, bj].reshape(SSM_G, 2 * SSM_N)
    v_seq = vin[:, :bj].reshape(N_TILE, bsz, n_chunks, GPT, 2 * SSM_N)
    v_seq = v_seq.transpose(2, 1, 0, 3, 4).reshape(n_chunks, bsz * SSM_G, 2 * SSM_N)
    tile_b = lambda a: jnp.tile(a, (bsz, 1))
    states = _chunk_scan(v_seq, tile_b(s_init), tile_b(a1), tile_b(a2), 64)
    states = states.reshape(n_chunks, bsz, N_TILE, SW).transpose(2, 1, 0, 3).reshape(N_TILE, bj, SW)
    yt = _ssm_out(u_all, states.astype(BF16), t_c, ce_t, bj)
    y = yt.reshape(N_TILE, bj, CHUNK, LANES).transpose(1, 2, 0, 3).reshape(t_rows, SSM_WIDTH)

    mixed = _merge(attn, y, w_attn_proj[0].astype(BF16), w_glu_val[0].astype(BF16),
                   w_glu_gate[0].astype(BF16), zug, 1024, 512)
    h1, n_ffn = _out_proj(x2, mixed, w_out[0].astype(BF16), norm_ffn[0].reshape(1, d), 512)
    out = _ffn(n_ffn, h1, w_ffn_gate[0].astype(BF16), w_ffn_up[0].astype(BF16),
               w_ffn_down[0].astype(BF16), norm_final.reshape(1, d), 512, 512)
    return out.reshape(bsz, seq, d)
```

```python
import functools
import math

import numpy as np
import jax
import jax.numpy as jnp
from jax import lax
from jax.experimental import pallas as pl
from jax.experimental.pallas import tpu as pltpu

F32 = jnp.float32
BF16 = jnp.bfloat16

D_MODEL = 2048
N_META = 16
EPS = 1e-6
HEADS = 16
Q_LORA = 512
KV_LORA = 512
QK_NOPE = 128
QK_ROPE = 64
V_HEAD = 128
ROPE_THETA = 10000.0
SSM_WIDTH = 1024
SSM_P = 16
SSM_G = 64
SSM_N = 64
D_FF = 5632
LANES = 128
HEAD_PAD = 2 * LANES
LAT_W = Q_LORA + KV_LORA + LANES
CHUNK = 16
CP = CHUNK * SSM_P
GPT = LANES // SSM_P
N_TILE = SSM_G // GPT
CW = CHUNK * LANES
SW = GPT * 2 * SSM_N
UG_COL0 = 2048
META_ROWS = 128
VT_ROWS = V_HEAD + 16
VMEM_LIMIT = 56 * 1024 * 1024


def _cparams(*sem):
    return pltpu.CompilerParams(dimension_semantics=sem, vmem_limit_bytes=VMEM_LIMIT)


def _sigmoid(x):
    return 1.0 / (1.0 + jnp.exp(-x))


def _gelu_tanh(x):
    return x * (0.5 * (1.0 + jnp.tanh(math.sqrt(2.0 / math.pi) * (x + 0.044715 * (x * x * x)))))


def _rms(x, g):
    ms = jnp.mean(x * x, axis=-1, keepdims=True)
    return x * lax.rsqrt(ms + EPS) * g


def _norm_matmul_kernel(x_ref, g_ref, w_ref, o_ref, n_ref):
    @pl.when(pl.program_id(1) == 0)
    def _():
        n_ref[...] = _rms(x_ref[...], g_ref[...]).astype(BF16)

    o_ref[...] = jnp.dot(n_ref[...], w_ref[...], preferred_element_type=F32).astype(o_ref.dtype)


def _norm_matmul(x, g, w, out_dtype, tm, tn, n, col_block0):
    m, k = x.shape
    return pl.pallas_call(
        _norm_matmul_kernel,
        grid=(m // tm, n // tn),
        in_specs=[
            pl.BlockSpec((tm, k), lambda i, j: (i, 0)),
            pl.BlockSpec((1, k), lambda i, j: (0, 0)),
            pl.BlockSpec((k, tn), lambda i, j: (0, col_block0 + j)),
        ],
        out_specs=pl.BlockSpec((tm, tn), lambda i, j: (i, j)),
        out_shape=jax.ShapeDtypeStruct((m, n), out_dtype),
        scratch_shapes=[pltpu.VMEM((tm, k), BF16)],
        compiler_params=_cparams("parallel", "arbitrary"),
        name="norm_matmul",
    )(x, g, w)


def _rope_slab(x, cosf, sin1, sin2):
    return (x * cosf + pltpu.roll(x, LANES - QK_ROPE // 2, 1) * sin1
            + pltpu.roll(x, QK_ROPE // 2, 1) * sin2)


def _up_rope_kernel(ql_ref, kvl_ref, kr_ref, gq_ref, gkv_ref, wq_ref, wk_ref, wv_ref,
                    cos_ref, sin1_ref, sin2_ref, q_ref, k_ref, vt_ref):
    cosf, sin1, sin2 = cos_ref[...], sin1_ref[...], sin2_ref[...]
    qn = _rms(ql_ref[...], gq_ref[...]).astype(BF16)
    kvn = _rms(kvl_ref[...], gkv_ref[...]).astype(BF16)
    k_rope = _rope_slab(kr_ref[...], cosf, sin1, sin2).astype(BF16)
    ones_rows = (lax.broadcasted_iota(jnp.int32, (VT_ROWS - V_HEAD, ql_ref.shape[0]), 0) == 0
                 ).astype(F32).astype(BF16)
    for h in range(HEADS):
        acc = jnp.dot(qn, wq_ref[:, h * HEAD_PAD:(h + 1) * HEAD_PAD], preferred_element_type=F32)
        q_ref[h, :, :LANES] = acc[:, :LANES].astype(BF16)
        q_ref[h, :, LANES:] = _rope_slab(acc[:, LANES:], cosf, sin1, sin2).astype(BF16)
    for hp in range(HEADS // 2):
        acc_k = jnp.dot(kvn, wk_ref[:, hp * 256:(hp + 1) * 256], preferred_element_type=F32)
        acc_v = jnp.dot(kvn, wv_ref[:, hp * 256:(hp + 1) * 256], preferred_element_type=F32)
        acc_vt = acc_v.T
        for s in range(2):
            h = 2 * hp + s
            k_ref[h, :, :LANES] = acc_k[:, s * LANES:(s + 1) * LANES].astype(BF16)
            k_ref[h, :, LANES:] = k_rope
            vt_ref[h, :V_HEAD] = acc_vt[s * V_HEAD:(s + 1) * V_HEAD].astype(BF16)
            vt_ref[h, V_HEAD:] = ones_rows


def _up_rope(lat, gq, gkv, wq, wk, wv, cosf, sin1, sin2, tm):
    m = lat.shape[0]
    n_pos_blocks = cosf.shape[0] // tm
    row = lambda i: (i, 0)
    const = lambda i: (0, 0)
    pos = lambda i: (i % n_pos_blocks, 0)
    head_major = lambda i: (0, i, 0)
    return pl.pallas_call(
        _up_rope_kernel,
        grid=(m // tm,),
        in_specs=[
            pl.BlockSpec((tm, Q_LORA), row),
            pl.BlockSpec((tm, KV_LORA), lambda i: (i, 1)),
            pl.BlockSpec((tm, LANES), lambda i: (i, (Q_LORA + KV_LORA) // LANES)),
            pl.BlockSpec((1, Q_LORA), const),
            pl.BlockSpec((1, KV_LORA), const),
            pl.BlockSpec(wq.shape, const),
            pl.BlockSpec(wk.shape, const),
            pl.BlockSpec(wv.shape, const),
            pl.BlockSpec((tm, LANES), pos),
            pl.BlockSpec((tm, LANES), pos),
            pl.BlockSpec((tm, LANES), pos),
        ],
        out_specs=[
            pl.BlockSpec((HEADS, tm, HEAD_PAD), head_major),
            pl.BlockSpec((HEADS, tm, HEAD_PAD), head_major),
            pl.BlockSpec((HEADS, VT_ROWS, tm), lambda i: (0, 0, i)),
        ],
        out_shape=[
            jax.ShapeDtypeStruct((HEADS, m, HEAD_PAD), BF16),
            jax.ShapeDtypeStruct((HEADS, m, HEAD_PAD), BF16),
            jax.ShapeDtypeStruct((HEADS, VT_ROWS, m), BF16),
        ],
        compiler_params=_cparams("parallel"),
        name="up_rope",
    )(lat, lat, lat, gq, gkv, wq, wk, wv, cosf, sin1, sin2)


def _attn_kernel(qi_ref, ki_ref, q_ref, k_ref, vt_ref, km_ref, vmt_ref, o_ref, m_ref, acc_ref):
    t = pl.program_id(1)
    qi = qi_ref[t]
    ki = ki_ref[t]
    nt = (((1,), (1,)), ((), ()))

    @pl.when(ki == 0)
    def _init():
        def body(h, c):
            s = lax.dot_general(km_ref[h], q_ref[h], nt, preferred_element_type=F32)
            key = lax.broadcasted_iota(jnp.int32, s.shape, 0)
            s = jnp.where(key < N_META, s, -jnp.inf)
            m = jnp.max(s, axis=0, keepdims=True)
            p = jnp.exp2(s - m)
            m_ref[h] = m
            acc_ref[h] = jnp.dot(vmt_ref[h], p.astype(BF16), preferred_element_type=F32)
            return c
        lax.fori_loop(0, HEADS, body, 0, unroll=2)

    def tile(masked):
        def scores(h):
            return lax.dot_general(k_ref[h], q_ref[h], nt, preferred_element_type=F32)

        def body(h, s):
            if masked:
                key = lax.broadcasted_iota(jnp.int32, s.shape, 0)
                qry = lax.broadcasted_iota(jnp.int32, s.shape, 1)
                s = jnp.where(key <= qry, s, -jnp.inf)
            m_prev = m_ref[h]
            m_new = jnp.maximum(m_prev, jnp.max(s, axis=0, keepdims=True))
            alpha = jnp.exp2(m_prev - m_new)
            p = jnp.exp2(s - m_new)
            acc_ref[h] = alpha * acc_ref[h] + jnp.dot(vt_ref[h], p.astype(BF16),
                                                      preferred_element_type=F32)
            m_ref[h] = m_new

        s_next = scores(0)
        for h in range(HEADS):
            s_cur = s_next
            if h + 1 < HEADS:
                s_next = scores(h + 1)
            body(h, s_cur)

    @pl.when(ki < qi)
    def _full():
        tile(False)

    @pl.when(ki == qi)
    def _diag():
        tile(True)
        for h in range(HEADS):
            out = acc_ref[h, :V_HEAD] / acc_ref[h, V_HEAD:V_HEAD + 1]
            o_ref[:, h * V_HEAD:(h + 1) * V_HEAD] = out.T.astype(o_ref.dtype)


def _attention(q, k, vt, km, vmt, bsz, seq, tq):
    nq = seq // tq
    pairs = [(a, b) for a in range(nq) for b in range(a + 1)]
    qi_arr = jnp.asarray([p[0] for p in pairs], jnp.int32)
    ki_arr = jnp.asarray([p[1] for p in pairs], jnp.int32)
    grid_spec = pltpu.PrefetchScalarGridSpec(
        num_scalar_prefetch=2,
        grid=(bsz, len(pairs)),
        in_specs=[
            pl.BlockSpec((HEADS, tq, HEAD_PAD), lambda b, t, qi, ki: (0, b * nq + qi[t], 0)),
            pl.BlockSpec((HEADS, tq, HEAD_PAD), lambda b, t, qi, ki: (0, b * nq + ki[t], 0)),
            pl.BlockSpec((HEADS, VT_ROWS, tq), lambda b, t, qi, ki: (0, 0, b * nq + ki[t])),
            pl.BlockSpec(km.shape, lambda b, t, qi, ki: (0, 0, 0)),
            pl.BlockSpec(vmt.shape, lambda b, t, qi, ki: (0, 0, 0)),
        ],
        out_specs=pl.BlockSpec((tq, HEADS * V_HEAD), lambda b, t, qi, ki: (b * nq + qi[t], 0)),
        scratch_shapes=[
            pltpu.VMEM((HEADS, 1, tq), F32),
            pltpu.VMEM((HEADS, VT_ROWS, tq), F32),
        ],
    )
    return pl.pallas_call(
        _attn_kernel,
        grid_spec=grid_spec,
        out_shape=jax.ShapeDtypeStruct((bsz * seq, HEADS * V_HEAD), BF16),
        compiler_params=_cparams("parallel", "arbitrary"),
        name="flash_attn",
    )(qi_arr, ki_arr, q, k, vt, km, vmt)


def _ssm_param_kernel(lr_ref, li_ref, ls_ref, bre_ref, bim_ref, cre_ref, cim_ref,
                      e_re_ref, e_im_ref, ce_re_ref, ce_im_ref, eb_re_ref, eb_im_ref, t_ref):
    nt = (((1,), (1,)), ((), ()))
    for g in range(lr_ref.shape[0]):
        lr, li = lr_ref[g], li_ref[g]
        step = jnp.exp(ls_ref[g])
        kf = lax.broadcasted_iota(jnp.int32, (CHUNK + 1, SSM_N), 0).astype(F32)
        mag = jnp.exp(kf * (lr * step))
        ang = kf * (li * step)
        e_re, e_im = mag * jnp.cos(ang), mag * jnp.sin(ang)
        e_re_ref[g], e_im_ref[g] = e_re, e_im
        nr, ni = e_re[1:2] - 1.0, e_im[1:2]
        den = lr * lr + li * li
        coef_re = (nr * lr + ni * li) / den
        coef_im = (ni * lr - nr * li) / den
        bre, bim = bre_ref[g], bim_ref[g]
        bb_re = coef_re * bre - coef_im * bim
        bb_im = coef_re * bim + coef_im * bre
        cre, cim = cre_ref[g], cim_ref[g]
        ce_re_lags, ce_im_lags = [], []
        for k in range(CHUNK + 1):
            er, ei = e_re[k:k + 1], e_im[k:k + 1]
            ce_re = cre * er - cim * ei
            ce_im = cre * ei + cim * er
            ce_re_ref[g, k], ce_im_ref[g, k] = ce_re, ce_im
            if k < CHUNK:
                ce_re_lags.append(ce_re)
                ce_im_lags.append(ce_im)
                eb_re_ref[g, k] = er * bb_re - ei * bb_im
                eb_im_ref[g, k] = er * bb_im + ei * bb_re
        t_ref[g] = (
            lax.dot_general(jnp.concatenate(ce_re_lags, axis=0), bb_re, nt,
                            precision=lax.Precision.HIGHEST, preferred_element_type=F32)
            - lax.dot_general(jnp.concatenate(ce_im_lags, axis=0), bb_im, nt,
                              precision=lax.Precision.HIGHEST, preferred_element_type=F32))


def _ssm_params(lam_re, lam_im, log_step, b_re, b_im, c_re, c_im, gb):
    g = SSM_G
    kp1 = CHUNK + 1
    row = lambda shape: pl.BlockSpec((gb,) + shape, lambda i: (i,) + (0,) * len(shape))
    outs = [
        ((g, kp1, SSM_N), (kp1, SSM_N)), ((g, kp1, SSM_N), (kp1, SSM_N)),
        ((g, kp1, SSM_P, SSM_N), (kp1, SSM_P, SSM_N)), ((g, kp1, SSM_P, SSM_N), (kp1, SSM_P, SSM_N)),
        ((g, CHUNK, SSM_P, SSM_N), (CHUNK, SSM_P, SSM_N)), ((g, CHUNK, SSM_P, SSM_N), (CHUNK, SSM_P, SSM_N)),
        ((g, CP, SSM_P), (CP, SSM_P)),
    ]
    return pl.pallas_call(
        _ssm_param_kernel,
        grid=(g // gb,),
        in_specs=[row((1, SSM_N)), row((1, SSM_N)), row((1, 1)),
                  row((SSM_P, SSM_N)), row((SSM_P, SSM_N)), row((SSM_P, SSM_N)), row((SSM_P, SSM_N))],
        out_specs=[row(blk) for _, blk in outs],
        out_shape=[jax.ShapeDtypeStruct(full, F32) for full, _ in outs],
        compiler_params=_cparams("parallel"),
        name="ssm_params",
    )(lam_re.reshape(g, 1, SSM_N), lam_im.reshape(g, 1, SSM_N), log_step.reshape(g, 1, 1),
      jnp.swapaxes(b_re, 1, 2), jnp.swapaxes(b_im, 1, 2), c_re, c_im)


def _ssm_kernel(u_ref, um_ref, wc_ref, a1_ref, a2_ref, tc_ref, ce_ref, y_ref,
                s_ref, s0_ref, win_ref, wt_ref, ws_ref, *, bsz, n_chunks):
    i = pl.program_id(0)
    rows = bsz * n_chunks

    @pl.when(i < N_TILE)
    def _state_inputs():
        win_ref[...] = jnp.zeros(win_ref.shape, win_ref.dtype)
        for sig in range(CHUNK):
            for g in range(GPT):
                r0 = sig * LANES + g * SSM_P
                win_ref[r0:r0 + SSM_P, g * 2 * SSM_N:(g + 1) * 2 * SSM_N] = (
                    wc_ref[g, sig * SSM_P:(sig + 1) * SSM_P, :])
        v = jnp.dot(u_ref[0], win_ref[...], preferred_element_type=F32)
        v_meta = jnp.dot(um_ref[0], win_ref[...], preferred_element_type=F32)
        for g in range(GPT):
            s_ref[g, pl.ds(i, rows, stride=N_TILE), :] = v[:, g * 2 * SSM_N:(g + 1) * 2 * SSM_N]
            s0_ref[g, pl.ds(i, 1), :] = v_meta[0:1, g * 2 * SSM_N:(g + 1) * 2 * SSM_N]

    @pl.when(i == N_TILE - 1)
    def _recurrence():
        a1 = [a1_ref[:, g * 2 * SSM_N:(g + 1) * 2 * SSM_N] for g in range(GPT)]
        a2 = [a2_ref[:, g * 2 * SSM_N:(g + 1) * 2 * SSM_N] for g in range(GPT)]
        for b in range(bsz):
            def body(j, state):
                r = pl.multiple_of((b * n_chunks + j) * N_TILE, N_TILE)
                nxt = []
                for g in range(GPT):
                    v = s_ref[g, pl.ds(r, N_TILE), :]
                    s_ref[g, pl.ds(r, N_TILE), :] = state[g]
                    nxt.append(a1[g] * state[g] + a2[g] * pltpu.roll(state[g], SSM_N, 1) + v)
                return tuple(nxt)
            lax.fori_loop(0, n_chunks, body, tuple(s0_ref[g] for g in range(GPT)))

    @pl.when(i >= N_TILE)
    def _outputs():
        tile = i - N_TILE
        row_g = lax.shift_right_logical(lax.broadcasted_iota(jnp.int32, (LANES, LANES), 0), 4)
        col_g = lax.shift_right_logical(lax.broadcasted_iota(jnp.int32, (LANES, LANES), 1), 4)
        same_group = row_g == col_g
        zero_blk = jnp.zeros((LANES, LANES), BF16)
        lag_blocks = []
        for k in range(CHUNK):
            rep = jnp.concatenate([tc_ref[0, k]] * GPT, axis=0)
            lag_blocks.append(jnp.where(same_group, rep, 0.0).astype(BF16))
        for sig in range(CHUNK):
            for tau in range(CHUNK):
                blk = lag_blocks[tau - sig] if tau >= sig else zero_blk
                wt_ref[sig * LANES:(sig + 1) * LANES, tau * LANES:(tau + 1) * LANES] = blk
        for g in range(GPT):
            own = col_g == g
            for tau in range(CHUNK):
                ws_ref[g * LANES:(g + 1) * LANES, tau * LANES:(tau + 1) * LANES] = (
                    jnp.where(own, ce_ref[0, tau], 0.0).astype(BF16))
        states = jnp.concatenate(
            [s_ref[g, pl.ds(tile, rows, stride=N_TILE), :] for g in range(GPT)], axis=1).astype(BF16)
        y = (jnp.dot(u_ref[0], wt_ref[...], preferred_element_type=F32)
             + jnp.dot(states, ws_ref[...], preferred_element_type=F32))
        y_ref[0] = _gelu_tanh(y).astype(y_ref.dtype)


def _ssm(u, u_meta, w_compact, a1, a2, t_compact, ce_compact, bsz, n_chunks):
    rows = bsz * n_chunks
    in_tile = lambda i: (lax.rem(i, N_TILE), 0, 0)
    out_tile = lambda i: (jnp.maximum(i - N_TILE, 0), 0, 0)
    out_tile4 = lambda i: (jnp.maximum(i - N_TILE, 0), 0, 0, 0)
    whole = lambda i: (0, 0)
    return pl.pallas_call(
        functools.partial(_ssm_kernel, bsz=bsz, n_chunks=n_chunks),
        grid=(2 * N_TILE,),
        in_specs=[pl.BlockSpec((1, rows, CW), in_tile),
                  pl.BlockSpec((1,) + u_meta.shape[1:], in_tile),
                  pl.BlockSpec((GPT, CP, 2 * SSM_N), in_tile),
                  pl.BlockSpec((N_TILE, SW), whole),
                  pl.BlockSpec((N_TILE, SW), whole),
                  pl.BlockSpec((1, CHUNK, SSM_P, LANES), out_tile4),
                  pl.BlockSpec((1, CHUNK, 2 * SSM_N, LANES), out_tile4)],
        out_specs=pl.BlockSpec((1, rows, CW), out_tile),
        out_shape=jax.ShapeDtypeStruct((N_TILE, rows, CW), BF16),
        scratch_shapes=[pltpu.VMEM((GPT, rows * N_TILE, 2 * SSM_N), F32),
                        pltpu.VMEM((GPT, N_TILE, 2 * SSM_N), F32),
                        pltpu.VMEM((CW, SW), BF16),
                        pltpu.VMEM((CW, CW), BF16),
                        pltpu.VMEM((SW, CW), BF16)],
        compiler_params=_cparams("arbitrary"),
        name="ssm",
    )(u, u_meta, w_compact, a1, a2, t_compact, ce_compact)


def _merge_kernel(attn_ref, y_ref, wap_ref, wv_ref, wg_ref, g0_ref, g1_ref, o_ref):
    a = jnp.dot(attn_ref[...], wap_ref[...], preferred_element_type=F32)
    y = y_ref[...]
    ssm = (jnp.dot(y, wv_ref[...], preferred_element_type=F32)
           * _sigmoid(jnp.dot(y, wg_ref[...], preferred_element_type=F32)))
    mixed = _sigmoid(g0_ref[...].astype(F32)) * a + _sigmoid(g1_ref[...].astype(F32)) * ssm
    o_ref[...] = mixed.astype(o_ref.dtype)


def _merge(attn, y, wap, wv, wg, zug, tm, tn):
    m = attn.shape[0]
    g0_off = SSM_WIDTH // tn
    g1_off = (SSM_WIDTH + D_MODEL) // tn
    return pl.pallas_call(
        _merge_kernel,
        grid=(m // tm, D_MODEL // tn),
        in_specs=[
            pl.BlockSpec((tm, attn.shape[1]), lambda i, j: (i, 0)),
            pl.BlockSpec((tm, SSM_WIDTH), lambda i, j: (i, 0)),
            pl.BlockSpec((attn.shape[1], tn), lambda i, j: (0, j)),
            pl.BlockSpec((SSM_WIDTH, tn), lambda i, j: (0, j)),
            pl.BlockSpec((SSM_WIDTH, tn), lambda i, j: (0, j)),
            pl.BlockSpec((tm, tn), lambda i, j: (i, g0_off + j)),
            pl.BlockSpec((tm, tn), lambda i, j: (i, g1_off + j)),
        ],
        out_specs=pl.BlockSpec((tm, tn), lambda i, j: (i, j)),
        out_shape=jax.ShapeDtypeStruct((m, D_MODEL), BF16),
        compiler_params=_cparams("parallel", "arbitrary"),
        name="gated_merge",
    )(attn, y, wap, wv, wg, zug, zug)


def _out_proj_kernel(x_ref, mx_ref, w_ref, g_ref, h_ref, n_ref):
    h = x_ref[...] + jnp.dot(mx_ref[...], w_ref[...], preferred_element_type=F32)
    h_ref[...] = h
    n_ref[...] = _rms(h, g_ref[...]).astype(n_ref.dtype)


def _out_proj(x, mixed, w, g, tm):
    m, d = x.shape
    row = pl.BlockSpec((tm, d), lambda i: (i, 0))
    return pl.pallas_call(
        _out_proj_kernel,
        grid=(m // tm,),
        in_specs=[row, row, pl.BlockSpec((d, d), lambda i: (0, 0)), pl.BlockSpec((1, d), lambda i: (0, 0))],
        out_specs=[row, row],
        out_shape=[jax.ShapeDtypeStruct((m, d), F32), jax.ShapeDtypeStruct((m, d), BF16)],
        compiler_params=_cparams("parallel"),
        name="out_proj",
    )(x, mixed, w, g)


def _ffn_kernel(n_ref, h_ref, wg_ref, wu_ref, wd_ref, gf_ref, o_ref, acc_ref):
    f = pl.program_id(1)
    n = n_ref[...]
    gate = jnp.dot(n, wg_ref[...], preferred_element_type=F32)
    up = jnp.dot(n, wu_ref[...], preferred_element_type=F32)
    act = (gate * _sigmoid(gate) * up).astype(BF16)
    part = jnp.dot(act, wd_ref[...], preferred_element_type=F32)

    @pl.when(f == 0)
    def _():
        acc_ref[...] = part

    @pl.when(f > 0)
    def _():
        acc_ref[...] += part

    @pl.when(f == pl.num_programs(1) - 1)
    def _():
        o_ref[...] = _rms(h_ref[...] + acc_ref[...], gf_ref[...])


def _ffn(n, h, wg, wu, wd, gf, tm, tf):
    m, d = h.shape
    row = lambda i, f: (i, 0)
    return pl.pallas_call(
        _ffn_kernel,
        grid=(m // tm, D_FF // tf),
        in_specs=[
            pl.BlockSpec((tm, d), row),
            pl.BlockSpec((tm, d), row),
            pl.BlockSpec((d, tf), lambda i, f: (0, f)),
            pl.BlockSpec((d, tf), lambda i, f: (0, f)),
            pl.BlockSpec((tf, d), lambda i, f: (f, 0)),
            pl.BlockSpec((1, d), lambda i, f: (0, 0)),
        ],
        out_specs=pl.BlockSpec((tm, d), row),
        out_shape=jax.ShapeDtypeStruct((m, d), F32),
        scratch_shapes=[pltpu.VMEM((tm, d), F32)],
        compiler_params=_cparams("parallel", "arbitrary"),
        name="ffn",
    )(n, h, wg, wu, wd, gf)


def _rope_tables(length):
    half = QK_ROPE // 2
    freqs = np.float32(ROPE_THETA) ** (-np.arange(half, dtype=np.float32) / np.float32(half))
    ang = (np.arange(length, dtype=np.float32)[:, None] * freqs[None, :]).astype(np.float32)
    c, s = np.cos(ang).astype(np.float32), np.sin(ang).astype(np.float32)
    z = np.zeros_like(c)
    cosf = np.concatenate([c, c, z, z], axis=1)
    sin1 = np.concatenate([-s, z, z, z], axis=1)
    sin2 = np.concatenate([z, s, z, z], axis=1)
    return cosf, sin1, sin2


def _chunk_operators(e_re, e_im, ce_re, ce_im, eb_re, eb_im, t, d):
    g = SSM_G
    w_in = jnp.concatenate([eb_re[:, ::-1], eb_im[:, ::-1]], axis=-1).reshape(g, CP, 2 * SSM_N)
    st = jnp.concatenate([ce_re[:, 1:], -ce_im[:, 1:]], axis=-1)
    ce_t = st.reshape(N_TILE, GPT, CHUNK, SSM_P, 2 * SSM_N).transpose(0, 2, 4, 1, 3)
    ce_t = ce_t.reshape(N_TILE, CHUNK, 2 * SSM_N, LANES)
    t = t.reshape(g, CHUNK, SSM_P, SSM_P).at[:, 0].add(jax.vmap(jnp.diag)(d))
    t_c = t.reshape(N_TILE, GPT, CHUNK, SSM_P, SSM_P).transpose(0, 2, 4, 1, 3)
    t_c = t_c.reshape(N_TILE, CHUNK, SSM_P, LANES)
    a1 = jnp.concatenate([e_re[:, CHUNK], e_re[:, CHUNK]], axis=-1)
    a2 = jnp.concatenate([-e_im[:, CHUNK], e_im[:, CHUNK]], axis=-1)
    return w_in.astype(BF16), ce_t, t_c, a1, a2


def kernel(x, meta_tokens, norm_mix, w_in, norm_q, w_q_up, norm_kv, w_kv_up, w_attn_proj, ssm_lambda_re, ssm_lambda_im, ssm_log_step, ssm_b_re, ssm_b_im, ssm_c_re, ssm_c_im, ssm_d, w_glu_val, w_glu_gate, w_out, norm_ffn, w_ffn_gate, w_ffn_up, w_ffn_down, norm_final):
    bsz, seq, d = x.shape
    assert d == D_MODEL and w_in.shape[0] == 1 and seq % 512 == 0 and N_META == CHUNK
    t_rows = bsz * seq
    n_chunks = seq // CHUNK
    bj = bsz * n_chunks
    x2 = x.reshape(t_rows, d)
    meta = jnp.pad(meta_tokens.astype(x.dtype), ((0, META_ROWS - N_META), (0, 0)))

    off_u = Q_LORA + KV_LORA + QK_ROPE
    w_all = jnp.concatenate([w_in[0][:, :off_u], jnp.zeros((d, UG_COL0 - off_u), w_in.dtype),
                             w_in[0][:, off_u:]], axis=1).astype(BF16)
    scale = (QK_NOPE + QK_ROPE) ** -0.5 * math.log2(math.e)
    wq =jnp.pad((w_q_up[0] * scale).reshape(Q_LORA, HEADS, QK_NOPE + QK_ROPE),
                 ((0, 0), (0, 0), (0, HEAD_PAD - QK_NOPE - QK_ROPE))).reshape(Q_LORA, HEADS * HEAD_PAD).astype(BF16)
    wkv = w_kv_up[0].reshape(KV_LORA, HEADS, QK_NOPE + V_HEAD)
    wk = wkv[:, :, :QK_NOPE].reshape(KV_LORA, HEADS * QK_NOPE).astype(BF16)
    wv = wkv[:, :, QK_NOPE:].reshape(KV_LORA, HEADS * V_HEAD).astype(BF16)
    g_mix = norm_mix[0].reshape(1, d)
    g_q = norm_q[0].reshape(1, Q_LORA)
    g_kv = norm_kv[0].reshape(1, KV_LORA)
    cosf, sin1, sin2 = _rope_tables(N_META + seq)

    ug_w = SSM_WIDTH + 2 * D_MODEL
    lat = _norm_matmul(x2, g_mix, w_all, F32, 512, LAT_W, LAT_W, 0)
    zug = _norm_matmul(x2, g_mix, w_all, BF16, 1024, 1024, ug_w, UG_COL0 // 1024)
    lat_m = _norm_matmul(meta, g_mix, w_all, F32, META_ROWS, LAT_W, LAT_W, 0)
    u_m = _norm_matmul(meta, g_mix, w_all, BF16, META_ROWS, SSM_WIDTH, SSM_WIDTH, UG_COL0 // SSM_WIDTH)

    q, k, vt = _up_rope(lat, g_q, g_kv, wq, wk, wv, cosf[N_META:], sin1[N_META:], sin2[N_META:], 512)
    _, k_m, vt_m = _up_rope(lat_m, g_q, g_kv, wq, wk, wv, cosf[:META_ROWS], sin1[:META_ROWS],
                            sin2[:META_ROWS], META_ROWS)
    attn = _attention(q, k, vt, k_m, vt_m, bsz, seq, 512)

    e_re, e_im, ce_re, ce_im, eb_re, eb_im, t_blk = _ssm_params(
        ssm_lambda_re[0], ssm_lambda_im[0], ssm_log_step[0], ssm_b_re[0], ssm_b_im[0],
        ssm_c_re[0], ssm_c_im[0], 8)
    w_sin, ce_t, t_c, a1, a2 = _chunk_operators(e_re, e_im, ce_re, ce_im, eb_re, eb_im, t_blk, ssm_d[0])
    u_seq = zug[:, :SSM_WIDTH].reshape(bj, CHUNK, N_TILE, LANES).transpose(2, 0, 1, 3).reshape(N_TILE, bj, CW)
    u_meta = jnp.pad(u_m[:N_META].reshape(1, CHUNK, N_TILE, LANES).transpose(2, 0, 1, 3).reshape(N_TILE, 1, CW),
                     ((0, 0), (0, 15), (0, 0)))
    yt = _ssm(u_seq, u_meta, w_sin, a1.reshape(N_TILE, SW), a2.reshape(N_TILE, SW), t_c, ce_t, bsz, n_chunks)
    y = yt.reshape(N_TILE, bj, CHUNK, LANES).transpose(1, 2, 0, 3).reshape(t_rows, SSM_WIDTH)

    mixed = _merge(attn, y, w_attn_proj[0].astype(BF16), w_glu_val[0].astype(BF16),
                   w_glu_gate[0].astype(BF16), zug, 1024, 512)
    h1, n_ffn = _out_proj(x2, mixed, w_out[0].astype(BF16), norm_ffn[0].reshape(1, d), 512)
    out = _ffn(n_ffn, h1, w_ffn_gate[0].astype(BF16), w_ffn_up[0].astype(BF16),
               w_ffn_down[0].astype(BF16), norm_final.reshape(1, d), 512, 512)
    return out.reshape(bsz, seq, d)
```

```python
import functools
import math

import numpy as np
import jax
import jax.numpy as jnp
from jax import lax
from jax.experimental import pallas as pl
from jax.experimental.pallas import tpu as pltpu

F32 = jnp.float32
BF16 = jnp.bfloat16

D_MODEL = 2048
N_META = 16
EPS = 1e-6
HEADS = 16
Q_LORA = 512
KV_LORA = 512
QK_NOPE = 128
QK_ROPE = 64
V_HEAD = 128
ROPE_THETA = 10000.0
SSM_WIDTH = 1024
SSM_P = 16
SSM_G = 64
SSM_N = 64
D_FF = 5632
LANES = 128
MXU_N = 256
FFN_ACC_COLS = 512
HEAD_PAD = 2 * LANES
LAT_W = Q_LORA + KV_LORA + LANES
CHUNK = 16
CP = CHUNK * SSM_P
GPT = LANES // SSM_P
N_TILE = SSM_G // GPT
CW = CHUNK * LANES
SW = GPT * 2 * SSM_N
UG_COL0 = 2048
META_ROWS = 128
VT_ROWS = V_HEAD + 16
VMEM_LIMIT = 56 * 1024 * 1024


def _cparams(*sem):
    return pltpu.CompilerParams(dimension_semantics=sem, vmem_limit_bytes=VMEM_LIMIT)


def _sigmoid(x):
    return 1.0 / (1.0 + jnp.exp(-x))


def _gelu_tanh(x):
    return x * (0.5 * (1.0 + jnp.tanh(math.sqrt(2.0 / math.pi) * (x + 0.044715 * (x * x * x)))))


def _rms(x, g):
    ms = jnp.mean(x * x, axis=-1, keepdims=True)
    return x * lax.rsqrt(ms + EPS) * g


def _norm_matmul_kernel(x_ref, g_ref, w_ref, o_ref, n_ref):
    @pl.when(pl.program_id(1) == 0)
    def _():
        n_ref[...] = _rms(x_ref[...], g_ref[...]).astype(BF16)

    n = n_ref[...]
    tn = o_ref.shape[1]
    for c0 in range(0, tn, 2 * MXU_N):
        cols = slice(c0, min(c0 + 2 * MXU_N, tn))
        o_ref[:, cols] = jnp.dot(n, w_ref[:, cols], preferred_element_type=F32).astype(o_ref.dtype)


def _norm_matmul(x, g, w, out_dtype, tm, tn, n, col_block0):
    m, k = x.shape
    return pl.pallas_call(
        _norm_matmul_kernel,
        grid=(m // tm, n // tn),
        in_specs=[
            pl.BlockSpec((tm, k), lambda i, j: (i, 0)),
            pl.BlockSpec((1, k), lambda i, j: (0, 0)),
            pl.BlockSpec((k, tn), lambda i, j: (0, col_block0 + j)),
        ],
        out_specs=pl.BlockSpec((tm, tn), lambda i, j: (i, j)),
        out_shape=jax.ShapeDtypeStruct((m, n), out_dtype),
        scratch_shapes=[pltpu.VMEM((tm, k), BF16)],
        compiler_params=_cparams("parallel", "arbitrary"),
        name="norm_matmul",
    )(x, g, w)


def _ug_proj_kernel(x_ref, g_ref, w_ref, u_ref, gate_ref, n_ref, stage_ref):
    j = pl.program_id(1)
    tm = x_ref.shape[0]

    @pl.when(j == 0)
    def _():
        n_ref[...] = _rms(x_ref[...], g_ref[...]).astype(BF16)
        n = n_ref[...]
        for c in range(SSM_WIDTH // MXU_N):
            acc = jnp.dot(n, w_ref[:, c * MXU_N:(c + 1) * MXU_N], preferred_element_type=F32)
            for s in range(MXU_N // LANES):
                stage_ref[c * (MXU_N // LANES) + s] = acc[:, s * LANES:(s + 1) * LANES]
        for t in range(N_TILE):
            for sig in range(CHUNK):
                u_ref[t, :, sig * LANES:(sig + 1) * LANES] = (
                    stage_ref[t, pl.ds(sig, tm // CHUNK, stride=CHUNK), :].astype(u_ref.dtype))

    @pl.when(j > 0)
    def _():
        n = n_ref[...]
        tn = gate_ref.shape[1]
        for c0 in range(0, tn, 2 * MXU_N):
            cols = slice(c0, c0 + 2 * MXU_N)
            gate_ref[:, cols] = jnp.dot(n, w_ref[:, cols], preferred_element_type=F32).astype(gate_ref.dtype)


def _ug_proj(x, g, w, tm, tn, col_block0):
    m, k = x.shape
    n_gate = 2 * D_MODEL
    assert tn == SSM_WIDTH
    return pl.pallas_call(
        _ug_proj_kernel,
        grid=(m // tm, 1 + n_gate // tn),
        in_specs=[
            pl.BlockSpec((tm, k), lambda i, j: (i, 0)),
            pl.BlockSpec((1, k), lambda i, j: (0, 0)),
            pl.BlockSpec((k, tn), lambda i, j: (0, col_block0 + j)),
        ],
        out_specs=[
            pl.BlockSpec((N_TILE, tm // CHUNK, CW), lambda i, j: (0, i, 0)),
            pl.BlockSpec((tm, tn), lambda i, j: (i, jnp.maximum(j - 1, 0))),
        ],
        out_shape=[jax.ShapeDtypeStruct((N_TILE, m // CHUNK, CW), BF16),
                   jax.ShapeDtypeStruct((m, n_gate), BF16)],
        scratch_shapes=[pltpu.VMEM((tm, k), BF16), pltpu.VMEM((N_TILE, tm, LANES), F32)],
        compiler_params=_cparams("parallel", "arbitrary"),
        name="ug_proj",
    )(x, g, w)


def _rope_slab(x, cosf, sin1, sin2):
    return (x * cosf + pltpu.roll(x, LANES - QK_ROPE // 2, 1) * sin1
            + pltpu.roll(x, QK_ROPE // 2, 1) * sin2)


def _up_rope_kernel(ql_ref, kvl_ref, kr_ref, gq_ref, gkv_ref, wq_ref, wkv_ref,
                    cos_ref, sin1_ref, sin2_ref, q_ref, k_ref, vt_ref):
    cosf, sin1, sin2 = cos_ref[...], sin1_ref[...], sin2_ref[...]
    qn = _rms(ql_ref[...], gq_ref[...]).astype(BF16)
    kvn = _rms(kvl_ref[...], gkv_ref[...]).astype(BF16)
    k_rope = _rope_slab(kr_ref[...], cosf, sin1, sin2).astype(BF16)
    ones_rows = (lax.broadcasted_iota(jnp.int32, (VT_ROWS - V_HEAD, ql_ref.shape[0]), 0) == 0
                 ).astype(F32).astype(BF16)
    for h in range(HEADS):
        acc = jnp.dot(qn, wq_ref[:, h * HEAD_PAD:(h + 1) * HEAD_PAD], preferred_element_type=F32)
        q_ref[h, :, :LANES] = acc[:, :LANES].astype(BF16)
        q_ref[h, :, LANES:] = _rope_slab(acc[:, LANES:], cosf, sin1, sin2).astype(BF16)
    kv_w = QK_NOPE + V_HEAD
    for h in range(HEADS):
        acc = jnp.dot(kvn, wkv_ref[:, h * kv_w:(h + 1) * kv_w], preferred_element_type=F32)
        k_ref[h, :, :LANES] = acc[:, :QK_NOPE].astype(BF16)
        k_ref[h, :, LANES:] = k_rope
        vt_ref[h, :V_HEAD] = acc[:, QK_NOPE:].T.astype(BF16)
        vt_ref[h, V_HEAD:] = ones_rows


def _up_rope(lat, gq, gkv, wq, wkv, cosf, sin1, sin2, tm):
    m = lat.shape[0]
    n_pos_blocks = cosf.shape[0] // tm
    row = lambda i: (i, 0)
    const = lambda i: (0, 0)
    pos = lambda i: (i % n_pos_blocks, 0)
    head_major = lambda i: (0, i, 0)
    return pl.pallas_call(
        _up_rope_kernel,
        grid=(m // tm,),
        in_specs=[
            pl.BlockSpec((tm, Q_LORA), row),
            pl.BlockSpec((tm, KV_LORA), lambda i: (i, 1)),
            pl.BlockSpec((tm, LANES), lambda i: (i, (Q_LORA + KV_LORA) // LANES)),
            pl.BlockSpec((1, Q_LORA), const),
            pl.BlockSpec((1, KV_LORA), const),
            pl.BlockSpec(wq.shape, const),
            pl.BlockSpec(wkv.shape, const),
            pl.BlockSpec((tm, LANES), pos),
            pl.BlockSpec((tm, LANES), pos),
            pl.BlockSpec((tm, LANES), pos),
        ],
        out_specs=[
            pl.BlockSpec((HEADS, tm, HEAD_PAD), head_major),
            pl.BlockSpec((HEADS, tm, HEAD_PAD), head_major),
            pl.BlockSpec((HEADS, VT_ROWS, tm), lambda i: (0, 0, i)),
        ],
        out_shape=[
            jax.ShapeDtypeStruct((HEADS, m, HEAD_PAD), BF16),
            jax.ShapeDtypeStruct((HEADS, m, HEAD_PAD), BF16),
            jax.ShapeDtypeStruct((HEADS, VT_ROWS, m), BF16),
        ],
        compiler_params=_cparams("parallel"),
        name="up_rope",
    )(lat, lat, lat, gq, gkv, wq, wkv, cosf, sin1, sin2)


def _attn_kernel(qi_ref, ki_ref, q_ref, k_ref, vt_ref, km_ref, vmt_ref, o_ref, m_ref, acc_ref):
    t = pl.program_id(1)
    qi = qi_ref[t]
    ki = ki_ref[t]
    nt = (((1,), (1,)), ((), ()))

    @pl.when(ki == 0)
    def _init():
        def body(h, c):
            s = lax.dot_general(km_ref[h], q_ref[h], nt, preferred_element_type=F32)
            key = lax.broadcasted_iota(jnp.int32, s.shape, 0)
            s = jnp.where(key < N_META, s, -jnp.inf)
            m = jnp.max(s, axis=0, keepdims=True)
            p = jnp.exp2(s - m)
            m_ref[h] = m
            acc_ref[h] = jnp.dot(vmt_ref[h], p.astype(BF16), preferred_element_type=F32)
            return c
        lax.fori_loop(0, HEADS, body, 0, unroll=2)

    def tile(masked):
        def scores(h):
            return lax.dot_general(k_ref[h], q_ref[h], nt, preferred_element_type=F32)

        def body(h, s):
            if masked:
                key = lax.broadcasted_iota(jnp.int32, s.shape, 0)
                qry = lax.broadcasted_iota(jnp.int32, s.shape, 1)
                s = jnp.where(key <= qry, s, -jnp.inf)
            m_prev = m_ref[h]
            m_new = jnp.maximum(m_prev, jnp.max(s, axis=0, keepdims=True))
            alpha = jnp.exp2(m_prev - m_new)
            p = jnp.exp2(s - m_new)
            acc_ref[h] = alpha * acc_ref[h] + jnp.dot(vt_ref[h], p.astype(BF16),
                                                      preferred_element_type=F32)
            m_ref[h] = m_new

        s_next = scores(0)
        for h in range(HEADS):
            s_cur = s_next
            if h + 1 < HEADS:
                s_next = scores(h + 1)
            body(h, s_cur)

    @pl.when(ki < qi)
    def _full():
        tile(False)

    @pl.when(ki == qi)
    def _diag():
        tile(True)
        for h in range(HEADS):
            out = acc_ref[h, :V_HEAD] / acc_ref[h, V_HEAD:V_HEAD + 1]
            o_ref[:, h * V_HEAD:(h + 1) * V_HEAD] = out.T.astype(o_ref.dtype)


def _attention(q, k, vt, km, vmt, bsz, seq, tq):
    nq = seq // tq
    pairs = [(a, b) for a in range(nq) for b in range(a + 1)]
    qi_arr = jnp.asarray([p[0] for p in pairs], jnp.int32)
    ki_arr = jnp.asarray([p[1] for p in pairs], jnp.int32)
    grid_spec = pltpu.PrefetchScalarGridSpec(
        num_scalar_prefetch=2,
        grid=(bsz, len(pairs)),
        in_specs=[
            pl.BlockSpec((HEADS, tq, HEAD_PAD), lambda b, t, qi, ki: (0, b * nq + qi[t], 0)),
            pl.BlockSpec((HEADS, tq, HEAD_PAD), lambda b, t, qi, ki: (0, b * nq + ki[t], 0)),
            pl.BlockSpec((HEADS, VT_ROWS, tq), lambda b, t, qi, ki: (0, 0, b * nq + ki[t])),
            pl.BlockSpec(km.shape, lambda b, t, qi, ki: (0, 0, 0)),
            pl.BlockSpec(vmt.shape, lambda b, t, qi, ki: (0, 0, 0)),
        ],
        out_specs=pl.BlockSpec((tq, HEADS * V_HEAD), lambda b, t, qi, ki: (b * nq + qi[t], 0)),
        scratch_shapes=[
            pltpu.VMEM((HEADS, 1, tq), F32),
            pltpu.VMEM((HEADS, VT_ROWS, tq), F32),
        ],
    )
    return pl.pallas_call(
        _attn_kernel,
        grid_spec=grid_spec,
        out_shape=jax.ShapeDtypeStruct((bsz * seq, HEADS * V_HEAD), BF16),
        compiler_params=_cparams("parallel", "arbitrary"),
        name="flash_attn",
    )(qi_arr, ki_arr, q, k, vt, km, vmt)


def _ssm_param_kernel(lr_ref, li_ref, ls_ref, bre_ref, bim_ref, cre_ref, cim_ref,
                      e_re_ref, e_im_ref, ce_re_ref, ce_im_ref, eb_re_ref, eb_im_ref, t_ref):
    nt = (((1,), (1,)), ((), ()))
    for g in range(lr_ref.shape[0]):
        lr, li = lr_ref[g], li_ref[g]
        step = jnp.exp(ls_ref[g])
        kf = lax.broadcasted_iota(jnp.int32, (CHUNK + 1, SSM_N), 0).astype(F32)
        mag = jnp.exp(kf * (lr * step))
        ang = kf * (li * step)
        e_re, e_im = mag * jnp.cos(ang), mag * jnp.sin(ang)
        e_re_ref[g], e_im_ref[g] = e_re, e_im
        nr, ni = e_re[1:2] - 1.0, e_im[1:2]
        den = lr * lr + li * li
        coef_re = (nr * lr + ni * li) / den
        coef_im = (ni * lr - nr * li) / den
        bre, bim = bre_ref[g], bim_ref[g]
        bb_re = coef_re * bre - coef_im * bim
        bb_im = coef_re * bim + coef_im * bre
        cre, cim = cre_ref[g], cim_ref[g]
        ce_re_lags, ce_im_lags = [], []
        for k in range(CHUNK + 1):
            er, ei = e_re[k:k + 1], e_im[k:k + 1]
            ce_re = cre * er - cim * ei
            ce_im = cre * ei + cim * er
            ce_re_ref[g, k], ce_im_ref[g, k] = ce_re, ce_im
            if k < CHUNK:
                ce_re_lags.append(ce_re)
                ce_im_lags.append(ce_im)
                eb_re_ref[g, CHUNK - 1 - k] = er * bb_re - ei * bb_im
                eb_im_ref[g, CHUNK - 1 - k] = er * bb_im + ei * bb_re
        t_ref[g] = (
            lax.dot_general(jnp.concatenate(ce_re_lags, axis=0), bb_re, nt,
                            precision=lax.Precision.HIGHEST, preferred_element_type=F32)
            - lax.dot_general(jnp.concatenate(ce_im_lags, axis=0), bb_im, nt,
                              precision=lax.Precision.HIGHEST, preferred_element_type=F32))


def _ssm_params(lam_re, lam_im, log_step, b_re, b_im, c_re, c_im, gb):
    g = SSM_G
    kp1 = CHUNK + 1
    row = lambda shape: pl.BlockSpec((gb,) + shape, lambda i: (i,) + (0,) * len(shape))
    outs = [
        ((g, kp1, SSM_N), (kp1, SSM_N)), ((g, kp1, SSM_N), (kp1, SSM_N)),
        ((g, kp1, SSM_P, SSM_N), (kp1, SSM_P, SSM_N)), ((g, kp1, SSM_P, SSM_N), (kp1, SSM_P, SSM_N)),
        ((g, CHUNK, SSM_P, SSM_N), (CHUNK, SSM_P, SSM_N)), ((g, CHUNK, SSM_P, SSM_N), (CHUNK, SSM_P, SSM_N)),
        ((g, CP, SSM_P), (CP, SSM_P)),
    ]
    return pl.pallas_call(
        _ssm_param_kernel,
        grid=(g // gb,),
        in_specs=[row((1, SSM_N)), row((1, SSM_N)), row((1, 1)),
                  row((SSM_P, SSM_N)), row((SSM_P, SSM_N)), row((SSM_P, SSM_N)), row((SSM_P, SSM_N))],
        out_specs=[row(blk) for _, blk in outs],
        out_shape=[jax.ShapeDtypeStruct(full, F32) for full, _ in outs],
        compiler_params=_cparams("parallel"),
        name="ssm_params",
    )(lam_re.reshape(g, 1, SSM_N), lam_im.reshape(g, 1, SSM_N), log_step.reshape(g, 1, 1),
      jnp.swapaxes(b_re, 1, 2), jnp.swapaxes(b_im, 1, 2), c_re, c_im)


def _ssm_kernel(u_ref, um_ref, wc_ref, a1_ref, a2_ref, tc_ref, ce_ref, y_ref,
                s_ref, s0_ref, win_ref, wt_ref, ws_ref, *, bsz, n_chunks):
    i = pl.program_id(0)
    rows = bsz * n_chunks

    @pl.when(i < N_TILE)
    def _state_inputs():
        win_ref[...] = jnp.zeros(win_ref.shape, win_ref.dtype)
        for sig in range(CHUNK):
            for g in range(GPT):
                r0 = sig * LANES + g * SSM_P
                win_ref[r0:r0 + SSM_P, g * 2 * SSM_N:(g + 1) * 2 * SSM_N] = (
                    wc_ref[g, sig * SSM_P:(sig + 1) * SSM_P, :])
        v = jnp.dot(u_ref[0], win_ref[...], preferred_element_type=F32)
        v_meta = jnp.dot(um_ref[0], win_ref[...], preferred_element_type=F32)
        for g in range(GPT):
            s_ref[g, pl.ds(i, rows, stride=N_TILE), :] = v[:, g * 2 * SSM_N:(g + 1) * 2 * SSM_N]
            s0_ref[g, pl.ds(i, 1), :] = v_meta[0:1, g * 2 * SSM_N:(g + 1) * 2 * SSM_N]

    @pl.when(i == N_TILE - 1)
    def _recurrence():
        a1 = [a1_ref[:, g * 2 * SSM_N:(g + 1) * 2 * SSM_N] for g in range(GPT)]
        a2 = [a2_ref[:, g * 2 * SSM_N:(g + 1) * 2 * SSM_N] for g in range(GPT)]
        for b in range(bsz):
            def body(j, state):
                r = pl.multiple_of((b * n_chunks + j) * N_TILE, N_TILE)
                nxt = []
                for g in range(GPT):
                    v = s_ref[g, pl.ds(r, N_TILE), :]
                    s_ref[g, pl.ds(r, N_TILE), :] = state[g]
                    nxt.append(a1[g] * state[g] + a2[g] * pltpu.roll(state[g], SSM_N, 1) + v)
                return tuple(nxt)
            lax.fori_loop(0, n_chunks, body, tuple(s0_ref[g] for g in range(GPT)))

    @pl.when(i >= N_TILE)
    def _outputs():
        tile = i - N_TILE
        row_g = lax.shift_right_logical(lax.broadcasted_iota(jnp.int32, (LANES, LANES), 0), 4)
        col_g = lax.shift_right_logical(lax.broadcasted_iota(jnp.int32, (LANES, LANES), 1), 4)
        same_group = row_g == col_g
        zero_blk = jnp.zeros((LANES, LANES), BF16)
        lag_blocks = []
        for k in range(CHUNK):
            rep = jnp.concatenate([tc_ref[0, k]] * GPT, axis=0)
            lag_blocks.append(jnp.where(same_group, rep, 0.0).astype(BF16))
        for sig in range(CHUNK):
            for tau in range(CHUNK):
                blk = lag_blocks[tau - sig] if tau >= sig else zero_blk
                wt_ref[sig * LANES:(sig + 1) * LANES, tau * LANES:(tau + 1) * LANES] = blk
        for g in range(GPT):
            own = col_g == g
            for tau in range(CHUNK):
                ws_ref[g * LANES:(g + 1) * LANES, tau * LANES:(tau + 1) * LANES] = (
                    jnp.where(own, ce_ref[0, tau], 0.0).astype(BF16))
        states = jnp.concatenate(
            [s_ref[g, pl.ds(tile, rows, stride=N_TILE), :] for g in range(GPT)], axis=1).astype(BF16)
        y = _gelu_tanh(jnp.dot(u_ref[0], wt_ref[...], preferred_element_type=F32)
                       + jnp.dot(states, ws_ref[...], preferred_element_type=F32))
        for tau in range(CHUNK):
            y_ref[pl.ds(tau, rows, stride=CHUNK), :] = y[:, tau * LANES:(tau + 1) * LANES]


def _ssm(u, u_meta, w_compact, a1, a2, t_compact, ce_compact, bsz, n_chunks):
    rows = bsz * n_chunks
    in_tile = lambda i: (lax.rem(i, N_TILE), 0, 0)
    out_tile4 = lambda i: (jnp.maximum(i - N_TILE, 0), 0, 0, 0)
    whole = lambda i: (0, 0)
    return pl.pallas_call(
        functools.partial(_ssm_kernel, bsz=bsz, n_chunks=n_chunks),
        grid=(2 * N_TILE,),
        in_specs=[pl.BlockSpec((1, rows, CW), in_tile),
                  pl.BlockSpec((1,) + u_meta.shape[1:], in_tile),
                  pl.BlockSpec((GPT, CP, 2 * SSM_N), in_tile),
                  pl.BlockSpec((N_TILE, SW), whole),
                  pl.BlockSpec((N_TILE, SW), whole),
                  pl.BlockSpec((1, CHUNK, SSM_P, LANES), out_tile4),
                  pl.BlockSpec((1, CHUNK, 2 * SSM_N, LANES), out_tile4)],
        out_specs=pl.BlockSpec((rows * CHUNK, LANES), lambda i: (0, jnp.maximum(i - N_TILE, 0))),
        out_shape=jax.ShapeDtypeStruct((rows * CHUNK, SSM_WIDTH), F32),
        scratch_shapes=[pltpu.VMEM((GPT, rows * N_TILE, 2 * SSM_N), F32),
                        pltpu.VMEM((GPT, N_TILE, 2 * SSM_N), F32),
                        pltpu.VMEM((CW, SW), BF16),
                        pltpu.VMEM((CW, CW), BF16),
                        pltpu.VMEM((SW, CW), BF16)],
        compiler_params=_cparams("arbitrary"),
        name="ssm",
    )(u, u_meta, w_compact, a1, a2, t_compact, ce_compact)


def _merge_kernel(attn_ref, y_ref, wap_ref, wv_ref, wg_ref, g0_ref, g1_ref, o_ref):
    attn = attn_ref[...]
    y = y_ref[...].astype(BF16)
    for c in range(o_ref.shape[1] // MXU_N):
        cols = slice(c * MXU_N, (c + 1) * MXU_N)
        a = jnp.dot(attn, wap_ref[:, cols], preferred_element_type=F32)
        ssm = (jnp.dot(y, wv_ref[:, cols], preferred_element_type=F32)
               * _sigmoid(jnp.dot(y, wg_ref[:, cols], preferred_element_type=F32)))
        mixed = (_sigmoid(g0_ref[:, cols].astype(F32)) * a
                 + _sigmoid(g1_ref[:, cols].astype(F32)) * ssm)
        o_ref[:, cols] = mixed.astype(o_ref.dtype)


def _merge(attn, y, wap, wv, wg, gates, tm, tn):
    m = attn.shape[0]
    g1_off = D_MODEL // tn
    return pl.pallas_call(
        _merge_kernel,
        grid=(m // tm, D_MODEL // tn),
        in_specs=[
            pl.BlockSpec((tm, attn.shape[1]), lambda i, j: (i, 0)),
            pl.BlockSpec((tm, SSM_WIDTH), lambda i, j: (i, 0)),
            pl.BlockSpec((attn.shape[1], tn), lambda i, j: (0, j)),
            pl.BlockSpec((SSM_WIDTH, tn), lambda i, j: (0, j)),
            pl.BlockSpec((SSM_WIDTH, tn), lambda i, j: (0, j)),
            pl.BlockSpec((tm, tn), lambda i, j: (i, j)),
            pl.BlockSpec((tm, tn), lambda i, j: (i, g1_off + j)),
        ],
        out_specs=pl.BlockSpec((tm, tn), lambda i, j: (i, j)),
        out_shape=jax.ShapeDtypeStruct((m, D_MODEL), BF16),
        compiler_params=_cparams("parallel", "arbitrary"),
        name="gated_merge",
    )(attn, y, wap, wv, wg, gates, gates)


def _out_proj_kernel(x_ref, mx_ref, w_ref, g_ref, h_ref, n_ref):
    mx = mx_ref[...]
    d = h_ref.shape[1]
    sq = None
    for c in range(d // MXU_N):
        cols = slice(c * MXU_N, (c + 1) * MXU_N)
        h = x_ref[:, cols] + jnp.dot(mx, w_ref[:, cols], preferred_element_type=F32)
        h_ref[:, cols] = h
        part = jnp.sum(h * h, axis=-1, keepdims=True)
        sq = part if sq is None else sq + part
    inv = lax.rsqrt(sq * (1.0 / d) + EPS)
    n_ref[...] = (h_ref[...] * inv * g_ref[...]).astype(n_ref.dtype)


def _out_proj(x, mixed, w, g, tm):
    m, d = x.shape
    row = pl.BlockSpec((tm, d), lambda i: (i, 0))
    return pl.pallas_call(
        _out_proj_kernel,
        grid=(m // tm,),
        in_specs=[row, row, pl.BlockSpec((d, d), lambda i: (0, 0)), pl.BlockSpec((1, d), lambda i: (0, 0))],
        out_specs=[row, row],
        out_shape=[jax.ShapeDtypeStruct((m, d), F32), jax.ShapeDtypeStruct((m, d), BF16)],
        compiler_params=_cparams("parallel"),
        name="out_proj",
    )(x, mixed, w, g)


def _ffn_kernel(n_ref, h_ref, wg_ref, wu_ref, wd_ref, gf_ref, o_ref, acc_ref):
    f = pl.program_id(1)
    n = n_ref[...]
    tf = wg_ref.shape[1]
    d = wd_ref.shape[1]
    @pl.when(f == 0)
    def _():
        acc_ref[...] = jnp.zeros(acc_ref.shape, acc_ref.dtype)

    acts = []
    for c in range(tf // MXU_N):
        cols = slice(c * MXU_N, (c + 1) * MXU_N)
        gate = jnp.dot(n, wg_ref[:, cols], preferred_element_type=F32)
        up = jnp.dot(n, wu_ref[:, cols], preferred_element_type=F32)
        acts.append((gate * _sigmoid(gate) * up).astype(BF16))

    for c in range(d // FFN_ACC_COLS):
        cols = slice(c * FFN_ACC_COLS, (c + 1) * FFN_ACC_COLS)
        part = None
        for k, a in enumerate(acts):
            term = jnp.dot(a, wd_ref[k * MXU_N:(k + 1) * MXU_N, cols], preferred_element_type=F32)
            part = term if part is None else part + term
        acc_ref[:, cols] += part

    @pl.when(f == pl.num_programs(1) - 1)
    def _():
        o_ref[...] = _rms(h_ref[...] + acc_ref[...], gf_ref[...])


def _ffn(n, h, wg, wu, wd, gf, tm, tf):
    m, d = h.shape
    row = lambda i, f: (i, 0)
    return pl.pallas_call(
        _ffn_kernel,
        grid=(m // tm, D_FF // tf),
        in_specs=[
            pl.BlockSpec((tm, d), row),
            pl.BlockSpec((tm, d), row),
            pl.BlockSpec((d, tf), lambda i, f: (0, f)),
            pl.BlockSpec((d, tf), lambda i, f: (0, f)),
            pl.BlockSpec((tf, d), lambda i, f: (f, 0)),
            pl.BlockSpec((1, d), lambda i, f: (0, 0)),
        ],
        out_specs=pl.BlockSpec((tm, d), row),
        out_shape=jax.ShapeDtypeStruct((m, d), F32),
        scratch_shapes=[pltpu.VMEM((tm, d), F32)],
        compiler_params=_cparams("parallel", "arbitrary"),
        name="ffn",
    )(n, h, wg, wu, wd, gf)


def _rope_tables(length):
    half = QK_ROPE // 2
    freqs = np.float32(ROPE_THETA) ** (-np.arange(half, dtype=np.float32) / np.float32(half))
    ang = (np.arange(length, dtype=np.float32)[:, None] * freqs[None, :]).astype(np.float32)
    c, s = np.cos(ang).astype(np.float32), np.sin(ang).astype(np.float32)
    z = np.zeros_like(c)
    cosf = np.concatenate([c, c, z, z], axis=1)
    sin1 = np.concatenate([-s, z, z, z], axis=1)
    sin2 = np.concatenate([z, s, z, z], axis=1)
    return cosf, sin1, sin2


def _chunk_operators(e_re, e_im, ce_re, ce_im, eb_re, eb_im, t, d):
    g = SSM_G
    w_in = jnp.concatenate([eb_re, eb_im], axis=-1).reshape(g, CP, 2 * SSM_N)
    st = jnp.concatenate([ce_re[:, 1:], -ce_im[:, 1:]], axis=-1)
    ce_t = st.reshape(N_TILE, GPT, CHUNK, SSM_P, 2 * SSM_N).transpose(0, 2, 4, 1, 3)
    ce_t = ce_t.reshape(N_TILE, CHUNK, 2 * SSM_N, LANES)
    t = t.reshape(g, CHUNK, SSM_P, SSM_P).at[:, 0].add(jax.vmap(jnp.diag)(d))
    t_c = t.reshape(N_TILE, GPT, CHUNK, SSM_P, SSM_P).transpose(0, 2, 4, 1, 3)
    t_c = t_c.reshape(N_TILE, CHUNK, SSM_P, LANES)
    a1 = jnp.concatenate([e_re[:, CHUNK], e_re[:, CHUNK]], axis=-1)
    a2 = jnp.concatenate([-e_im[:, CHUNK], e_im[:, CHUNK]], axis=-1)
    return w_in.astype(BF16), ce_t, t_c, a1, a2


def kernel(x, meta_tokens, norm_mix, w_in, norm_q, w_q_up, norm_kv, w_kv_up, w_attn_proj, ssm_lambda_re, ssm_lambda_im, ssm_log_step, ssm_b_re, ssm_b_im, ssm_c_re, ssm_c_im, ssm_d, w_glu_val, w_glu_gate, w_out, norm_ffn, w_ffn_gate, w_ffn_up, w_ffn_down, norm_final):
    bsz, seq, d = x.shape
    assert d == D_MODEL and w_in.shape[0] == 1 and seq % 512 == 0 and N_META == CHUNK
    t_rows = bsz * seq
    n_chunks = seq // CHUNK
    bj = bsz * n_chunks
    x2 = x.reshape(t_rows, d)
    meta = jnp.pad(meta_tokens.astype(x.dtype), ((0, META_ROWS - N_META), (0, 0)))

    off_u = Q_LORA + KV_LORA + QK_ROPE
    w_all = jnp.concatenate([w_in[0][:, :off_u], jnp.zeros((d, UG_COL0 - off_u), w_in.dtype),
                             w_in[0][:, off_u:]], axis=1).astype(BF16)
    scale = (QK_NOPE + QK_ROPE) ** -0.5 * math.log2(math.e)
    wq = jnp.pad((w_q_up[0] * scale).reshape(Q_LORA, HEADS, QK_NOPE + QK_ROPE),
                 ((0, 0), (0, 0), (0, HEAD_PAD - QK_NOPE - QK_ROPE))).reshape(Q_LORA, HEADS * HEAD_PAD).astype(BF16)
    wkv = w_kv_up[0].astype(BF16)
    g_mix = norm_mix[0].reshape(1, d)
    g_q = norm_q[0].reshape(1, Q_LORA)
    g_kv = norm_kv[0].reshape(1, KV_LORA)
    cosf, sin1, sin2 = _rope_tables(N_META + seq)

    lat = _norm_matmul(x2, g_mix, w_all, F32, 512, LAT_W, LAT_W, 0)
    u_seq, gates = _ug_proj(x2, g_mix, w_all, 1024, SSM_WIDTH, UG_COL0 // SSM_WIDTH)
    lat_m = _norm_matmul(meta, g_mix, w_all, F32, META_ROWS, LAT_W, LAT_W, 0)
    u_m = _norm_matmul(meta, g_mix, w_all, BF16, META_ROWS, SSM_WIDTH, SSM_WIDTH, UG_COL0 // SSM_WIDTH)

    q, k, vt = _up_rope(lat, g_q, g_kv, wq, wkv, cosf[N_META:], sin1[N_META:], sin2[N_META:], 512)
    _, k_m, vt_m = _up_rope(lat_m, g_q, g_kv, wq, wkv, cosf[:META_ROWS], sin1[:META_ROWS],
                            sin2[:META_ROWS], META_ROWS)
    attn = _attention(q, k, vt, k_m, vt_m, bsz, seq, 512)

    e_re, e_im, ce_re, ce_im, eb_re, eb_im, t_blk = _ssm_params(
        ssm_lambda_re[0], ssm_lambda_im[0], ssm_log_step[0], ssm_b_re[0], ssm_b_im[0],
        ssm_c_re[0], ssm_c_im[0], 8)
    w_sin, ce_t, t_c, a1, a2 = _chunk_operators(e_re, e_im, ce_re, ce_im, eb_re, eb_im, t_blk, ssm_d[0])
    u_meta = jnp.pad(u_m[:N_META].reshape(1, CHUNK, N_TILE, LANES).transpose(2, 0, 1, 3).reshape(N_TILE, 1, CW),
                     ((0, 0), (0, 15), (0, 0)))
    y = _ssm(u_seq, u_meta, w_sin, a1.reshape(N_TILE, SW), a2.reshape(N_TILE, SW), t_c, ce_t, bsz, n_chunks)

    mixed = _merge(attn, y, w_attn_proj[0].astype(BF16), w_glu_val[0].astype(BF16),
                   w_glu_gate[0].astype(BF16), gates, 1024, 512)
    h1, n_ffn = _out_proj(x2, mixed, w_out[0].astype(BF16), norm_ffn[0].reshape(1, d), 512)
    out = _ffn(n_ffn, h1, w_ffn_gate[0].astype(BF16), w_ffn_up[0].astype(BF16),
               w_ffn_down[0].astype(BF16), norm_final.reshape(1, d), 512, 512)
    return out.reshape(bsz, seq, d)
```

```python
import functools
import math

import numpy as np
import jax
import jax.numpy as jnp
from jax import lax
from jax.experimental import pallas as pl
from jax.experimental.pallas import tpu as pltpu

F32 = jnp.float32
BF16 = jnp.bfloat16

D_MODEL = 2048
N_META = 16
EPS = 1e-6
HEADS = 16
Q_LORA = 512
KV_LORA = 512
QK_NOPE = 128
QK_ROPE = 64
V_HEAD = 128
ROPE_THETA = 10000.0
SSM_WIDTH = 1024
SSM_P = 16
SSM_G = 64
SSM_N = 64
D_FF = 5632
LANES = 128
MXU_N = 256
FFN_ACC_COLS = 512
HEAD_PAD = 2 * LANES
LAT_W = Q_LORA + KV_LORA + LANES
CHUNK = 16
CP = CHUNK * SSM_P
GPT = LANES // SSM_P
N_TILE = SSM_G // GPT
CW = CHUNK * LANES
SW = GPT * 2 * SSM_N
UG_COL0 = 2048
META_ROWS = 128
VT_ROWS = V_HEAD + 16
VMEM_LIMIT = 56 * 1024 * 1024


def _cparams(*sem):
    return pltpu.CompilerParams(dimension_semantics=sem, vmem_limit_bytes=VMEM_LIMIT)


def _sigmoid(x):
    return 1.0 / (1.0 + jnp.exp(-x))


def _gelu_tanh(x):
    return x * (0.5 * (1.0 + jnp.tanh(math.sqrt(2.0 / math.pi) * (x + 0.044715 * (x * x * x)))))


def _rms(x, g):
    ms = jnp.mean(x * x, axis=-1, keepdims=True)
    return x * lax.rsqrt(ms + EPS) * g


def _w_in_prep_kernel(w_ref, o_ref):
    rows = w_ref.shape[0]
    off_u = Q_LORA + KV_LORA + QK_ROPE
    n_ug = w_ref.shape[1] - off_u
    lane = lax.broadcasted_iota(jnp.int32, (rows, LANES), 1)
    low = lane < LANES // 2
    lat_full = off_u - LANES // 2
    o_ref[:, :lat_full] = w_ref[:, :lat_full].astype(BF16)
    o_ref[:, lat_full:LAT_W] = jnp.where(low, w_ref[:, lat_full:LAT_W], 0.0).astype(BF16)
    o_ref[:, LAT_W:UG_COL0] = jnp.zeros((rows, UG_COL0 - LAT_W), BF16)
    for c in range(n_ug // LANES):
        a = w_ref[:, lat_full + c * LANES:lat_full + (c + 1) * LANES]
        if (c + 2) * LANES + lat_full <= w_ref.shape[1]:
            b = w_ref[:, lat_full + (c + 1) * LANES:lat_full + (c + 2) * LANES]
        else:
            b = jnp.pad(w_ref[:, lat_full + (c + 1) * LANES:], ((0, 0), (0, LANES // 2)))
        stitched = pltpu.roll(jnp.where(low, b, a), LANES // 2, 1)
        o_ref[:, UG_COL0 + c * LANES:UG_COL0 + (c + 1) * LANES] = stitched.astype(BF16)


def _w_in_prep(w, rb):
    k, n = w.shape
    n_out = UG_COL0 + n - (Q_LORA + KV_LORA + QK_ROPE)
    return pl.pallas_call(
        _w_in_prep_kernel,
        grid=(k // rb,),
        in_specs=[pl.BlockSpec((rb, n), lambda i: (i, 0))],
        out_specs=pl.BlockSpec((rb, n_out), lambda i: (i, 0)),
        out_shape=jax.ShapeDtypeStruct((k, n_out), BF16),
        compiler_params=_cparams("parallel"),
        name="w_in_prep",
    )(w)


def _norm_matmul_kernel(x_ref, g_ref, w_ref, o_ref, n_ref):
    @pl.when(pl.program_id(1) == 0)
    def _():
        n_ref[...] = _rms(x_ref[...], g_ref[...]).astype(BF16)

    n = n_ref[...]
    tn = o_ref.shape[1]
    for c0 in range(0, tn, 2 * MXU_N):
        cols = slice(c0, min(c0 + 2 * MXU_N, tn))
        o_ref[:, cols] = jnp.dot(n, w_ref[:, cols], preferred_element_type=F32).astype(o_ref.dtype)


def _norm_matmul(x, g, w, out_dtype, tm, tn, n, col_block0):
    m, k = x.shape
    return pl.pallas_call(
        _norm_matmul_kernel,
        grid=(m // tm, n // tn),
        in_specs=[
            pl.BlockSpec((tm, k), lambda i, j: (i, 0)),
            pl.BlockSpec((1, k), lambda i, j: (0, 0)),
            pl.BlockSpec((k, tn), lambda i, j: (0, col_block0 + j)),
        ],
        out_specs=pl.BlockSpec((tm, tn), lambda i, j: (i, j)),
        out_shape=jax.ShapeDtypeStruct((m, n), out_dtype),
        scratch_shapes=[pltpu.VMEM((tm, k), BF16)],
        compiler_params=_cparams("parallel", "arbitrary"),
        name="norm_matmul",
    )(x, g, w)


def _ug_proj_kernel(x_ref, g_ref, w_ref, u_ref, gate_ref, n_ref, stage_ref):
    j = pl.program_id(1)
    tm = x_ref.shape[0]

    @pl.when(j == 0)
    def _():
        n_ref[...] = _rms(x_ref[...], g_ref[...]).astype(BF16)
        n = n_ref[...]
        for c in range(SSM_WIDTH // MXU_N):
            acc = jnp.dot(n, w_ref[:, c * MXU_N:(c + 1) * MXU_N], preferred_element_type=F32)
            for s in range(MXU_N // LANES):
                stage_ref[c * (MXU_N // LANES) + s] = acc[:, s * LANES:(s + 1) * LANES]
        for t in range(N_TILE):
            for sig in range(CHUNK):
                u_ref[t, :, sig * LANES:(sig + 1) * LANES] = (
                    stage_ref[t, pl.ds(sig, tm // CHUNK, stride=CHUNK), :].astype(u_ref.dtype))

    @pl.when(j > 0)
    def _():
        n = n_ref[...]
        tn = gate_ref.shape[1]
        for c0 in range(0, tn, 2 * MXU_N):
            cols = slice(c0, c0 + 2 * MXU_N)
            gate_ref[:, cols] = jnp.dot(n, w_ref[:, cols], preferred_element_type=F32).astype(gate_ref.dtype)


def _ug_proj(x, g, w, tm, tn, col_block0):
    m, k = x.shape
    n_gate = 2 * D_MODEL
    assert tn == SSM_WIDTH
    return pl.pallas_call(
        _ug_proj_kernel,
        grid=(m // tm, 1 + n_gate // tn),
        in_specs=[
            pl.BlockSpec((tm, k), lambda i, j: (i, 0)),
            pl.BlockSpec((1, k), lambda i, j: (0, 0)),
            pl.BlockSpec((k, tn), lambda i, j: (0, col_block0 + j)),
        ],
        out_specs=[
            pl.BlockSpec((N_TILE, tm // CHUNK, CW), lambda i, j: (0, i, 0)),
            pl.BlockSpec((tm, tn), lambda i, j: (i, jnp.maximum(j - 1, 0))),
        ],
        out_shape=[jax.ShapeDtypeStruct((N_TILE, m // CHUNK, CW), BF16),
                   jax.ShapeDtypeStruct((m, n_gate), BF16)],
        scratch_shapes=[pltpu.VMEM((tm, k), BF16), pltpu.VMEM((N_TILE, tm, LANES), F32)],
        compiler_params=_cparams("parallel", "arbitrary"),
        name="ug_proj",
    )(x, g, w)


def _rope_slab(x, cosf, sin1, sin2):
    return (x * cosf + pltpu.roll(x, LANES - QK_ROPE // 2, 1) * sin1
            + pltpu.roll(x, QK_ROPE // 2, 1) * sin2)


def _up_rope_kernel(ql_ref, kvl_ref, kr_ref, gq_ref, gkv_ref, wq_ref, wkv_ref,
                    cos_ref, sin1_ref, sin2_ref, q_ref, k_ref, vt_ref):
    cosf, sin1, sin2 = cos_ref[...], sin1_ref[...], sin2_ref[...]
    qn = _rms(ql_ref[...], gq_ref[...]).astype(BF16)
    kvn = _rms(kvl_ref[...], gkv_ref[...]).astype(BF16)
    k_rope = _rope_slab(kr_ref[...], cosf, sin1, sin2).astype(BF16)
    ones_rows = (lax.broadcasted_iota(jnp.int32, (VT_ROWS - V_HEAD, ql_ref.shape[0]), 0) == 0
                 ).astype(F32).astype(BF16)
    for h in range(HEADS):
        acc = jnp.dot(qn, wq_ref[:, h * HEAD_PAD:(h + 1) * HEAD_PAD], preferred_element_type=F32)
        q_ref[h, :, :LANES] = acc[:, :LANES].astype(BF16)
        q_ref[h, :, LANES:] = _rope_slab(acc[:, LANES:], cosf, sin1, sin2).astype(BF16)
    kv_w = QK_NOPE + V_HEAD
    for h in range(HEADS):
        acc = jnp.dot(kvn, wkv_ref[:, h * kv_w:(h + 1) * kv_w], preferred_element_type=F32)
        k_ref[h, :, :LANES] = acc[:, :QK_NOPE].astype(BF16)
        k_ref[h, :, LANES:] = k_rope
        vt_ref[h, :V_HEAD] = acc[:, QK_NOPE:].T.astype(BF16)
        vt_ref[h, V_HEAD:] = ones_rows


def _up_rope(lat, gq, gkv, wq, wkv, cosf, sin1, sin2, tm):
    m = lat.shape[0]
    n_pos_blocks = cosf.shape[0] // tm
    row = lambda i: (i, 0)
    const = lambda i: (0, 0)
    pos = lambda i: (i % n_pos_blocks, 0)
    head_major = lambda i: (0, i, 0)
    return pl.pallas_call(
        _up_rope_kernel,
        grid=(m // tm,),
        in_specs=[
            pl.BlockSpec((tm, Q_LORA), row),
            pl.BlockSpec((tm, KV_LORA), lambda i: (i, 1)),
            pl.BlockSpec((tm, LANES), lambda i: (i, (Q_LORA + KV_LORA) // LANES)),
            pl.BlockSpec((1, Q_LORA), const),
            pl.BlockSpec((1, KV_LORA), const),
            pl.BlockSpec(wq.shape, const),
            pl.BlockSpec(wkv.shape, const),
            pl.BlockSpec((tm, LANES), pos),
            pl.BlockSpec((tm, LANES), pos),
            pl.BlockSpec((tm, LANES), pos),
        ],
        out_specs=[
            pl.BlockSpec((HEADS, tm, HEAD_PAD), head_major),
            pl.BlockSpec((HEADS, tm, HEAD_PAD), head_major),
            pl.BlockSpec((HEADS, VT_ROWS, tm), lambda i: (0, 0, i)),
        ],
        out_shape=[
            jax.ShapeDtypeStruct((HEADS, m, HEAD_PAD), BF16),
            jax.ShapeDtypeStruct((HEADS, m, HEAD_PAD), BF16),
            jax.ShapeDtypeStruct((HEADS, VT_ROWS, m), BF16),
        ],
        compiler_params=_cparams("parallel"),
        name="up_rope",
    )(lat, lat, lat, gq, gkv, wq, wkv, cosf, sin1, sin2)


def _attn_kernel(qi_ref, ki_ref, q_ref, k_ref, vt_ref, km_ref, vmt_ref, o_ref, m_ref, acc_ref):
    t = pl.program_id(1)
    qi = qi_ref[t]
    ki = ki_ref[t]
    nt = (((1,), (1,)), ((), ()))

    @pl.when(ki == 0)
    def _init():
        def body(h, c):
            s = lax.dot_general(km_ref[h], q_ref[h], nt, preferred_element_type=F32)
            key = lax.broadcasted_iota(jnp.int32, s.shape, 0)
            s = jnp.where(key < N_META, s, -jnp.inf)
            m = jnp.max(s, axis=0, keepdims=True)
            p = jnp.exp2(s - m)
            m_ref[h] = m
            acc_ref[h] = jnp.dot(vmt_ref[h], p.astype(BF16), preferred_element_type=F32)
            return c
        lax.fori_loop(0, HEADS, body, 0, unroll=2)

    def tile(masked):
        def scores(h):
            return lax.dot_general(k_ref[h], q_ref[h], nt, preferred_element_type=F32)

        def body(h, s):
            if masked:
                key = lax.broadcasted_iota(jnp.int32, s.shape, 0)
                qry = lax.broadcasted_iota(jnp.int32, s.shape, 1)
                s = jnp.where(key <= qry, s, -jnp.inf)
            m_prev = m_ref[h]
            m_new = jnp.maximum(m_prev, jnp.max(s, axis=0, keepdims=True))
            alpha = jnp.exp2(m_prev - m_new)
            p = jnp.exp2(s - m_new)
            acc_ref[h] = alpha * acc_ref[h] + jnp.dot(vt_ref[h], p.astype(BF16),
                                                      preferred_element_type=F32)
            m_ref[h] = m_new

        s_next = scores(0)
        for h in range(HEADS):
            s_cur = s_next
            if h + 1 < HEADS:
                s_next = scores(h + 1)
            body(h, s_cur)

    @pl.when(ki < qi)
    def _full():
        tile(False)

    @pl.when(ki == qi)
    def _diag():
        tile(True)
        for h in range(HEADS):
            out = acc_ref[h, :V_HEAD] / acc_ref[h, V_HEAD:V_HEAD + 1]
            o_ref[:, h * V_HEAD:(h + 1) * V_HEAD] = out.T.astype(o_ref.dtype)


def _attention(q, k, vt, km, vmt, bsz, seq, tq):
    nq = seq // tq
    pairs = [(a, b) for a in range(nq) for b in range(a + 1)]
    qi_arr = jnp.asarray([p[0] for p in pairs], jnp.int32)
    ki_arr = jnp.asarray([p[1] for p in pairs], jnp.int32)
    grid_spec = pltpu.PrefetchScalarGridSpec(
        num_scalar_prefetch=2,
        grid=(bsz, len(pairs)),
        in_specs=[
            pl.BlockSpec((HEADS, tq, HEAD_PAD), lambda b, t, qi, ki: (0, b * nq + qi[t], 0)),
            pl.BlockSpec((HEADS, tq, HEAD_PAD), lambda b, t, qi, ki: (0, b * nq + ki[t], 0)),
            pl.BlockSpec((HEADS, VT_ROWS, tq), lambda b, t, qi, ki: (0, 0, b * nq + ki[t])),
            pl.BlockSpec(km.shape, lambda b, t, qi, ki: (0, 0, 0)),
            pl.BlockSpec(vmt.shape, lambda b, t, qi, ki: (0, 0, 0)),
        ],
        out_specs=pl.BlockSpec((tq, HEADS * V_HEAD), lambda b, t, qi, ki: (b * nq + qi[t], 0)),
        scratch_shapes=[
            pltpu.VMEM((HEADS, 1, tq), F32),
            pltpu.VMEM((HEADS, VT_ROWS, tq), F32),
        ],
    )
    return pl.pallas_call(
        _attn_kernel,
        grid_spec=grid_spec,
        out_shape=jax.ShapeDtypeStruct((bsz * seq, HEADS * V_HEAD), BF16),
        compiler_params=_cparams("parallel", "arbitrary"),
        name="flash_attn",
    )(qi_arr, ki_arr, q, k, vt, km, vmt)


def _ssm_param_kernel(lr_ref, li_ref, ls_ref, bre_ref, bim_ref, cre_ref, cim_ref,
                      e_re_ref, e_im_ref, ce_re_ref, ce_im_ref, eb_re_ref, eb_im_ref, t_ref):
    nt = (((1,), (1,)), ((), ()))
    for g in range(lr_ref.shape[0]):
        lr, li = lr_ref[g], li_ref[g]
        step = jnp.exp(ls_ref[g])
        kf = lax.broadcasted_iota(jnp.int32, (CHUNK + 1, SSM_N), 0).astype(F32)
        mag = jnp.exp(kf * (lr * step))
        ang = kf * (li * step)
        e_re, e_im = mag * jnp.cos(ang), mag * jnp.sin(ang)
        e_re_ref[g], e_im_ref[g] = e_re, e_im
        nr, ni = e_re[1:2] - 1.0, e_im[1:2]
        den = lr * lr + li * li
        coef_re = (nr * lr + ni * li) / den
        coef_im = (ni * lr - nr * li) / den
        bre, bim = bre_ref[g], bim_ref[g]
        bb_re = coef_re * bre - coef_im * bim
        bb_im = coef_re * bim + coef_im * bre
        cre, cim = cre_ref[g], cim_ref[g]
        ce_re_lags, ce_im_lags = [], []
        for k in range(CHUNK + 1):
            er, ei = e_re[k:k + 1], e_im[k:k + 1]
            ce_re = cre * er - cim * ei
            ce_im = cre * ei + cim * er
            ce_re_ref[g, k], ce_im_ref[g, k] = ce_re, ce_im
            if k < CHUNK:
                ce_re_lags.append(ce_re)
                ce_im_lags.append(ce_im)
                eb_re_ref[g, CHUNK - 1 - k] = er * bb_re - ei * bb_im
                eb_im_ref[g, CHUNK - 1 - k] = er * bb_im + ei * bb_re
        t_ref[g] = (
            lax.dot_general(jnp.concatenate(ce_re_lags, axis=0), bb_re, nt,
                            precision=lax.Precision.HIGHEST, preferred_element_type=F32)
            - lax.dot_general(jnp.concatenate(ce_im_lags, axis=0), bb_im, nt,
                              precision=lax.Precision.HIGHEST, preferred_element_type=F32))


def _ssm_params(lam_re, lam_im, log_step, b_re, b_im, c_re, c_im, gb):
    g = SSM_G
    kp1 = CHUNK + 1
    row = lambda shape: pl.BlockSpec((gb,) + shape, lambda i: (i,) + (0,) * len(shape))
    outs = [
        ((g, kp1, SSM_N), (kp1, SSM_N)), ((g, kp1, SSM_N), (kp1, SSM_N)),
        ((g, kp1, SSM_P, SSM_N), (kp1, SSM_P, SSM_N)), ((g, kp1, SSM_P, SSM_N), (kp1, SSM_P, SSM_N)),
        ((g, CHUNK, SSM_P, SSM_N), (CHUNK, SSM_P, SSM_N)), ((g, CHUNK, SSM_P, SSM_N), (CHUNK, SSM_P, SSM_N)),
        ((g, CP, SSM_P), (CP, SSM_P)),
    ]
    return pl.pallas_call(
        _ssm_param_kernel,
        grid=(g // gb,),
        in_specs=[row((1, SSM_N)), row((1, SSM_N)), row((1, 1)),
                  row((SSM_P, SSM_N)), row((SSM_P, SSM_N)), row((SSM_P, SSM_N)), row((SSM_P, SSM_N))],
        out_specs=[row(blk) for _, blk in outs],
        out_shape=[jax.ShapeDtypeStruct(full, F32) for full, _ in outs],
        compiler_params=_cparams("parallel"),
        name="ssm_params",
    )(lam_re.reshape(g, 1, SSM_N), lam_im.reshape(g, 1, SSM_N), log_step.reshape(g, 1, 1),
      jnp.swapaxes(b_re, 1, 2), jnp.swapaxes(b_im, 1, 2), c_re, c_im)


def _ssm_kernel(u_ref, um_ref, wc_ref, a1_ref, a2_ref, tc_ref, ce_ref, y_ref,
                s_ref, s0_ref, win_ref, wt_ref, ws_ref, *, bsz, n_chunks):
    i = pl.program_id(0)
    rows = bsz * n_chunks

    @pl.when(i < N_TILE)
    def _state_inputs():
        win_ref[...] = jnp.zeros(win_ref.shape, win_ref.dtype)
        for sig in range(CHUNK):
            for g in range(GPT):
                r0 = sig * LANES + g * SSM_P
                win_ref[r0:r0 + SSM_P, g * 2 * SSM_N:(g + 1) * 2 * SSM_N] = (
                    wc_ref[g, sig * SSM_P:(sig + 1) * SSM_P, :])
        v = jnp.dot(u_ref[0], win_ref[...], preferred_element_type=F32)
        v_meta = jnp.dot(um_ref[0], win_ref[...], preferred_element_type=F32)
        for g in range(GPT):
            s_ref[g, pl.ds(i, rows, stride=N_TILE), :] = v[:, g * 2 * SSM_N:(g + 1) * 2 * SSM_N]
            s0_ref[g, pl.ds(i, 1), :] = v_meta[0:1, g * 2 * SSM_N:(g + 1) * 2 * SSM_N]

    @pl.when(i == N_TILE - 1)
    def _recurrence():
        a1 = [a1_ref[:, g * 2 * SSM_N:(g + 1) * 2 * SSM_N] for g in range(GPT)]
        a2 = [a2_ref[:, g * 2 * SSM_N:(g + 1) * 2 * SSM_N] for g in range(GPT)]
        for b in range(bsz):
            def body(j, state):
                r = pl.multiple_of((b * n_chunks + j) * N_TILE, N_TILE)
                nxt = []
                for g in range(GPT):
                    v = s_ref[g, pl.ds(r, N_TILE), :]
                    s_ref[g, pl.ds(r, N_TILE), :] = state[g]
                    nxt.append(a1[g] * state[g] + a2[g] * pltpu.roll(state[g], SSM_N, 1) + v)
                return tuple(nxt)
            lax.fori_loop(0, n_chunks, body, tuple(s0_ref[g] for g in range(GPT)))

    @pl.when(i >= N_TILE)
    def _outputs():
        tile = i - N_TILE
        row_g = lax.shift_right_logical(lax.broadcasted_iota(jnp.int32, (LANES, LANES), 0), 4)
        col_g = lax.shift_right_logical(lax.broadcasted_iota(jnp.int32, (LANES, LANES), 1), 4)
        same_group = row_g == col_g
        zero_blk = jnp.zeros((LANES, LANES), BF16)
        lag_blocks = []
        for k in range(CHUNK):
            rep = jnp.concatenate([tc_ref[0, k]] * GPT, axis=0)
            lag_blocks.append(jnp.where(same_group, rep, 0.0).astype(BF16))
        for sig in range(CHUNK):
            for tau in range(CHUNK):
                blk = lag_blocks[tau - sig] if tau >= sig else zero_blk
                wt_ref[sig * LANES:(sig + 1) * LANES, tau * LANES:(tau + 1) * LANES] = blk
        for g in range(GPT):
            own = col_g == g
            for tau in range(CHUNK):
                ws_ref[g * LANES:(g + 1) * LANES, tau * LANES:(tau + 1) * LANES] = (
                    jnp.where(own, ce_ref[0, tau], 0.0).astype(BF16))
        states = jnp.concatenate(
            [s_ref[g, pl.ds(tile, rows, stride=N_TILE), :] for g in range(GPT)], axis=1).astype(BF16)
        y = _gelu_tanh(jnp.dot(u_ref[0], wt_ref[...], preferred_element_type=F32)
                       + jnp.dot(states, ws_ref[...], preferred_element_type=F32))
        for tau in range(CHUNK):
            y_ref[pl.ds(tau, rows, stride=CHUNK), :] = y[:, tau * LANES:(tau + 1) * LANES]


def _ssm(u, u_meta, w_compact, a1, a2, t_compact, ce_compact, bsz, n_chunks):
    rows = bsz * n_chunks
    in_tile = lambda i: (lax.rem(i, N_TILE), 0, 0)
    out_tile4 = lambda i: (jnp.maximum(i - N_TILE, 0), 0, 0, 0)
    whole = lambda i: (0, 0)
    return pl.pallas_call(
        functools.partial(_ssm_kernel, bsz=bsz, n_chunks=n_chunks),
        grid=(2 * N_TILE,),
        in_specs=[pl.BlockSpec((1, rows, CW), in_tile),
                  pl.BlockSpec((1,) + u_meta.shape[1:], in_tile),
                  pl.BlockSpec((GPT, CP, 2 * SSM_N), in_tile),
                  pl.BlockSpec((N_TILE, SW), whole),
                  pl.BlockSpec((N_TILE, SW), whole),
                  pl.BlockSpec((1, CHUNK, SSM_P, LANES), out_tile4),
                  pl.BlockSpec((1, CHUNK, 2 * SSM_N, LANES), out_tile4)],
        out_specs=pl.BlockSpec((rows * CHUNK, LANES), lambda i: (0, jnp.maximum(i - N_TILE, 0))),
        out_shape=jax.ShapeDtypeStruct((rows * CHUNK, SSM_WIDTH), F32),
        scratch_shapes=[pltpu.VMEM((GPT, rows * N_TILE, 2 * SSM_N), F32),
                        pltpu.VMEM((GPT, N_TILE, 2 * SSM_N), F32),
                        pltpu.VMEM((CW, SW), BF16),
                        pltpu.VMEM((CW, CW), BF16),
                        pltpu.VMEM((SW, CW), BF16)],
        compiler_params=_cparams("arbitrary"),
        name="ssm",
    )(u, u_meta, w_compact, a1, a2, t_compact, ce_compact)


def _merge_kernel(attn_ref, y_ref, wap_ref, wv_ref, wg_ref, g0_ref, g1_ref, o_ref):
    attn = attn_ref[...]
    y = y_ref[...].astype(BF16)
    for c in range(o_ref.shape[1] // MXU_N):
        cols = slice(c * MXU_N, (c + 1) * MXU_N)
        a = jnp.dot(attn, wap_ref[:, cols], preferred_element_type=F32)
        ssm = (jnp.dot(y, wv_ref[:, cols], preferred_element_type=F32)
               * _sigmoid(jnp.dot(y, wg_ref[:, cols], preferred_element_type=F32)))
        mixed = (_sigmoid(g0_ref[:, cols].astype(F32)) * a
                 + _sigmoid(g1_ref[:, cols].astype(F32)) * ssm)
        o_ref[:, cols] = mixed.astype(o_ref.dtype)


def _merge(attn, y, wap, wv, wg, gates, tm, tn):
    m = attn.shape[0]
    g1_off = D_MODEL // tn
    return pl.pallas_call(
        _merge_kernel,
        grid=(m // tm, D_MODEL // tn),
        in_specs=[
            pl.BlockSpec((tm, attn.shape[1]), lambda i, j: (i, 0)),
            pl.BlockSpec((tm, SSM_WIDTH), lambda i, j: (i, 0)),
            pl.BlockSpec((attn.shape[1], tn), lambda i, j: (0, j)),
            pl.BlockSpec((SSM_WIDTH, tn), lambda i, j: (0, j)),
            pl.BlockSpec((SSM_WIDTH, tn), lambda i, j: (0, j)),
            pl.BlockSpec((tm, tn), lambda i, j: (i, j)),
            pl.BlockSpec((tm, tn), lambda i, j: (i, g1_off + j)),
        ],
        out_specs=pl.BlockSpec((tm, tn), lambda i, j: (i, j)),
        out_shape=jax.ShapeDtypeStruct((m, D_MODEL), BF16),
        compiler_params=_cparams("parallel", "arbitrary"),
        name="gated_merge",
    )(attn, y, wap, wv, wg, gates, gates)


def _out_proj_kernel(x_ref, mx_ref, w_ref, g_ref, h_ref, n_ref):
    mx = mx_ref[...]
    d = h_ref.shape[1]
    sq = None
    for c in range(d // MXU_N):
        cols = slice(c * MXU_N, (c + 1) * MXU_N)
        h = x_ref[:, cols] + jnp.dot(mx, w_ref[:, cols], preferred_element_type=F32)
        h_ref[:, cols] = h
        part = jnp.sum(h * h, axis=-1, keepdims=True)
        sq = part if sq is None else sq + part
    inv = lax.rsqrt(sq * (1.0 / d) + EPS)
    n_ref[...] = (h_ref[...] * inv * g_ref[...]).astype(n_ref.dtype)


def _out_proj(x, mixed, w, g, tm):
    m, d = x.shape
    row = pl.BlockSpec((tm, d), lambda i: (i, 0))
    return pl.pallas_call(
        _out_proj_kernel,
        grid=(m // tm,),
        in_specs=[row, row, pl.BlockSpec((d, d), lambda i: (0, 0)), pl.BlockSpec((1, d), lambda i: (0, 0))],
        out_specs=[row, row],
        out_shape=[jax.ShapeDtypeStruct((m, d), F32), jax.ShapeDtypeStruct((m, d), BF16)],
        compiler_params=_cparams("parallel"),
        name="out_proj",
    )(x, mixed, w, g)


def _ffn_kernel(n_ref, h_ref, wg_ref, wu_ref, wd_ref, gf_ref, o_ref, acc_ref):
    f = pl.program_id(1)
    n = n_ref[...]
    tf = wg_ref.shape[1]
    d = wd_ref.shape[1]
    @pl.when(f == 0)
    def _():
        acc_ref[...] = jnp.zeros(acc_ref.shape, acc_ref.dtype)

    acts = []
    for c in range(tf // MXU_N):
        cols = slice(c * MXU_N, (c + 1) * MXU_N)
        gate = jnp.dot(n, wg_ref[:, cols], preferred_element_type=F32)
        up = jnp.dot(n, wu_ref[:, cols], preferred_element_type=F32)
        acts.append((gate * _sigmoid(gate) * up).astype(BF16))

    for c in range(d // FFN_ACC_COLS):
        cols = slice(c * FFN_ACC_COLS, (c + 1) * FFN_ACC_COLS)
        part = None
        for k, a in enumerate(acts):
            term = jnp.dot(a, wd_ref[k * MXU_N:(k + 1) * MXU_N, cols], preferred_element_type=F32)
            part = term if part is None else part + term
        acc_ref[:, cols] += part

    @pl.when(f == pl.num_programs(1) - 1)
    def _():
        o_ref[...] = _rms(h_ref[...] + acc_ref[...], gf_ref[...])


def _ffn(n, h, wg, wu, wd, gf, tm, tf):
    m, d = h.shape
    row = lambda i, f: (i, 0)
    return pl.pallas_call(
        _ffn_kernel,
        grid=(m // tm, D_FF // tf),
        in_specs=[
            pl.BlockSpec((tm, d), row),
            pl.BlockSpec((tm, d), row),
            pl.BlockSpec((d, tf), lambda i, f: (0, f)),
            pl.BlockSpec((d, tf), lambda i, f: (0, f)),
            pl.BlockSpec((tf, d), lambda i, f: (f, 0)),
            pl.BlockSpec((1, d), lambda i, f: (0, 0)),
        ],
        out_specs=pl.BlockSpec((tm, d), row),
        out_shape=jax.ShapeDtypeStruct((m, d), F32),
        scratch_shapes=[pltpu.VMEM((tm, d), F32)],
        compiler_params=_cparams("parallel", "arbitrary"),
        name="ffn",
    )(n, h, wg, wu, wd, gf)


def _rope_tables(length):
    half = QK_ROPE // 2
    freqs = np.float32(ROPE_THETA) ** (-np.arange(half, dtype=np.float32) / np.float32(half))
    ang = (np.arange(length, dtype=np.float32)[:, None] * freqs[None, :]).astype(np.float32)
    c, s = np.cos(ang).astype(np.float32), np.sin(ang).astype(np.float32)
    z = np.zeros_like(c)
    cosf = np.concatenate([c, c, z, z], axis=1)
    sin1 = np.concatenate([-s, z, z, z], axis=1)
    sin2 = np.concatenate([z, s, z, z], axis=1)
    return cosf, sin1, sin2


def _chunk_operators(e_re, e_im, ce_re, ce_im, eb_re, eb_im, t, d):
    g = SSM_G
    w_in = jnp.concatenate([eb_re, eb_im], axis=-1).reshape(g, CP, 2 * SSM_N)
    st = jnp.concatenate([ce_re[:, 1:], -ce_im[:, 1:]], axis=-1)
    ce_t = st.reshape(N_TILE, GPT, CHUNK, SSM_P, 2 * SSM_N).transpose(0, 2, 4, 1, 3)
    ce_t = ce_t.reshape(N_TILE, CHUNK, 2 * SSM_N, LANES)
    t = t.reshape(g, CHUNK, SSM_P, SSM_P).at[:, 0].add(jax.vmap(jnp.diag)(d))
    t_c = t.reshape(N_TILE, GPT, CHUNK, SSM_P, SSM_P).transpose(0, 2, 4, 1, 3)
    t_c = t_c.reshape(N_TILE, CHUNK, SSM_P, LANES)
    a1 = jnp.concatenate([e_re[:, CHUNK], e_re[:, CHUNK]], axis=-1)
    a2 = jnp.concatenate([-e_im[:, CHUNK], e_im[:, CHUNK]], axis=-1)
    return w_in.astype(BF16), ce_t, t_c, a1, a2


def kernel(x, meta_tokens, norm_mix, w_in, norm_q, w_q_up, norm_kv, w_kv_up, w_attn_proj, ssm_lambda_re, ssm_lambda_im, ssm_log_step, ssm_b_re, ssm_b_im, ssm_c_re, ssm_c_im, ssm_d, w_glu_val, w_glu_gate, w_out, norm_ffn, w_ffn_gate, w_ffn_up, w_ffn_down, norm_final):
    bsz, seq, d = x.shape
    assert d == D_MODEL and w_in.shape[0] == 1 and seq % 512 == 0 and N_META == CHUNK
    t_rows = bsz * seq
    n_chunks = seq // CHUNK
    bj = bsz * n_chunks
    x2 = x.reshape(t_rows, d)
    meta = jnp.pad(meta_tokens.astype(x.dtype), ((0, META_ROWS - N_META), (0, 0)))

    w_all = _w_in_prep(w_in[0], 256)
    scale = (QK_NOPE + QK_ROPE) ** -0.5 * math.log2(math.e)
    wq = jnp.pad((w_q_up[0] * scale).reshape(Q_LORA, HEADS, QK_NOPE + QK_ROPE),
                 ((0, 0), (0, 0), (0, HEAD_PAD - QK_NOPE - QK_ROPE))).reshape(Q_LORA, HEADS * HEAD_PAD).astype(BF16)
    wkv = w_kv_up[0].astype(BF16)
    g_mix = norm_mix[0].reshape(1, d)
    g_q = norm_q[0].reshape(1, Q_LORA)
    g_kv = norm_kv[0].reshape(1, KV_LORA)
    cosf, sin1, sin2 = _rope_tables(N_META + seq)

    lat = _norm_matmul(x2, g_mix, w_all, F32, 512, LAT_W, LAT_W, 0)
    u_seq, gates = _ug_proj(x2, g_mix, w_all, 1024, SSM_WIDTH, UG_COL0 // SSM_WIDTH)
    lat_m = _norm_matmul(meta, g_mix, w_all, F32, META_ROWS, LAT_W, LAT_W, 0)
    u_m = _norm_matmul(meta, g_mix, w_all, BF16, META_ROWS, SSM_WIDTH, SSM_WIDTH, UG_COL0 // SSM_WIDTH)

    q, k, vt = _up_rope(lat, g_q, g_kv, wq, wkv, cosf[N_META:], sin1[N_META:], sin2[N_META:], 512)
    _, k_m, vt_m = _up_rope(lat_m, g_q, g_kv, wq, wkv, cosf[:META_ROWS], sin1[:META_ROWS],
                            sin2[:META_ROWS], META_ROWS)
    attn = _attention(q, k, vt, k_m, vt_m, bsz, seq, 512)

    e_re, e_im, ce_re, ce_im, eb_re, eb_im, t_blk = _ssm_params(
        ssm_lambda_re[0], ssm_lambda_im[0], ssm_log_step[0], ssm_b_re[0], ssm_b_im[0],
        ssm_c_re[0], ssm_c_im[0], 8)
    w_sin, ce_t, t_c, a1, a2 = _chunk_operators(e_re, e_im, ce_re, ce_im, eb_re, eb_im, t_blk, ssm_d[0])
    u_meta = jnp.pad(u_m[:N_META].reshape(1, CHUNK, N_TILE, LANES).transpose(2, 0, 1, 3).reshape(N_TILE, 1, CW),
                     ((0, 0), (0, 15), (0, 0)))
    y = _ssm(u_seq, u_meta, w_sin, a1.reshape(N_TILE, SW), a2.reshape(N_TILE, SW), t_c, ce_t, bsz, n_chunks)

    mixed = _merge(attn, y, w_attn_proj[0].astype(BF16), w_glu_val[0].astype(BF16),
                   w_glu_gate[0].astype(BF16), gates, 1024, 512)
    h1, n_ffn = _out_proj(x2, mixed, w_out[0].astype(BF16), norm_ffn[0].reshape(1, d), 512)
    out = _ffn(n_ffn, h1, w_ffn_gate[0].astype(BF16), w_ffn_up[0].astype(BF16),
               w_ffn_down[0].astype(BF16), norm_final.reshape(1, d), 512, 512)
    return out.reshape(bsz, seq, d)
```

```python
import functools
import math

import numpy as np
import jax
import jax.numpy as jnp
from jax import lax
from jax.experimental import pallas as pl
from jax.experimental.pallas import tpu as pltpu

F32 = jnp.float32
BF16 = jnp.bfloat16

D_MODEL = 2048
N_META = 16
EPS = 1e-6
HEADS = 16
Q_LORA = 512
KV_LORA = 512
QK_NOPE = 128
QK_ROPE = 64
V_HEAD = 128
ROPE_THETA = 10000.0
SSM_WIDTH = 1024
SSM_P = 16
SSM_G = 64
SSM_N = 64
D_FF = 5632
LANES = 128
MXU_N = 256
FFN_ACC_COLS = 512
HEAD_PAD = 2 * LANES
LAT_W = Q_LORA + KV_LORA + LANES
CHUNK = 16
CP = CHUNK * SSM_P
GPT = LANES // SSM_P
N_TILE = SSM_G // GPT
CW = CHUNK * LANES
SW = GPT * 2 * SSM_N
UG_ROW0 = 2048
NT_DIMS = (((1,), (1,)), ((), ()))
META_ROWS = 128
V_PAD = 2 * LANES
VMEM_LIMIT = 56 * 1024 * 1024


def _cparams(*sem):
    return pltpu.CompilerParams(dimension_semantics=sem, vmem_limit_bytes=VMEM_LIMIT)


def _sigmoid(x):
    return 1.0 / (1.0 + jnp.exp(-x))


def _gelu_tanh(x):
    return x * (0.5 * (1.0 + jnp.tanh(math.sqrt(2.0 / math.pi) * (x + 0.044715 * (x * x * x)))))


def _rms(x, g):
    ms = jnp.mean(x * x, axis=-1, keepdims=True)
    return x * lax.rsqrt(ms + EPS) * g


def _w_in_prep_kernel(w_ref, o_ref):
    off_u = Q_LORA + KV_LORA + QK_ROPE
    o_ref[:off_u] = w_ref[:off_u].astype(BF16)
    o_ref[off_u:UG_ROW0] = jnp.zeros((UG_ROW0 - off_u, o_ref.shape[1]), BF16)
    o_ref[UG_ROW0:] = w_ref[off_u:].astype(BF16)


def _w_in_prep(wt, cb):
    n, k = wt.shape
    n_out = UG_ROW0 + n - (Q_LORA + KV_LORA + QK_ROPE)
    return pl.pallas_call(
        _w_in_prep_kernel,
        grid=(k // cb,),
        in_specs=[pl.BlockSpec((n, cb), lambda i: (0, i))],
        out_specs=pl.BlockSpec((n_out, cb), lambda i: (0, i)),
        out_shape=jax.ShapeDtypeStruct((n_out, k), BF16),
        compiler_params=_cparams("parallel"),
        name="w_in_prep",
    )(wt)


def _norm_matmul_kernel(x_ref, g_ref, w_ref, o_ref, n_ref):
    @pl.when(pl.program_id(1) == 0)
    def _():
        n_ref[...] = _rms(x_ref[...], g_ref[...]).astype(BF16)

    n = n_ref[...]
    tn = o_ref.shape[1]
    for c0 in range(0, tn, 2 * MXU_N):
        cols = slice(c0, min(c0 + 2 * MXU_N, tn))
        o_ref[:, cols] = lax.dot_general(n, w_ref[cols, :], NT_DIMS,
                                         preferred_element_type=F32).astype(o_ref.dtype)


def _norm_matmul(x, g, wt, out_dtype, tm, tn, n, row_block0):
    m, k = x.shape
    return pl.pallas_call(
        _norm_matmul_kernel,
        grid=(m // tm, n // tn),
        in_specs=[
            pl.BlockSpec((tm, k), lambda i, j: (i, 0)),
            pl.BlockSpec((1, k), lambda i, j: (0, 0)),
            pl.BlockSpec((tn, k), lambda i, j: (row_block0 + j, 0)),
        ],
        out_specs=pl.BlockSpec((tm, tn), lambda i, j: (i, j)),
        out_shape=jax.ShapeDtypeStruct((m, n), out_dtype),
        scratch_shapes=[pltpu.VMEM((tm, k), BF16)],
        compiler_params=_cparams("parallel", "arbitrary"),
        name="norm_matmul",
    )(x, g, wt)


def _ug_proj_kernel(x_ref, g_ref, w_ref, u_ref, gate_ref, n_ref, stage_ref):
    j = pl.program_id(1)
    tm = x_ref.shape[0]

    @pl.when(j == 0)
    def _():
        n_ref[...] = _rms(x_ref[...], g_ref[...]).astype(BF16)
        n = n_ref[...]
        for c in range(SSM_WIDTH // MXU_N):
            acc = lax.dot_general(n, w_ref[c * MXU_N:(c + 1) * MXU_N, :], NT_DIMS,
                                  preferred_element_type=F32)
            for s in range(MXU_N // LANES):
                stage_ref[c * (MXU_N // LANES) + s] = acc[:, s * LANES:(s + 1) * LANES]
        for t in range(N_TILE):
            for sig in range(CHUNK):
                u_ref[t, :, sig * LANES:(sig + 1) * LANES] = (
                    stage_ref[t, pl.ds(sig, tm // CHUNK, stride=CHUNK), :].astype(u_ref.dtype))

    @pl.when(j > 0)
    def _():
        n = n_ref[...]
        tn = gate_ref.shape[1]
        for c0 in range(0, tn, 2 * MXU_N):
            cols = slice(c0, c0 + 2 * MXU_N)
            gate_ref[:, cols] = lax.dot_general(n, w_ref[cols, :], NT_DIMS,
                                                preferred_element_type=F32).astype(gate_ref.dtype)


def _ug_proj(x, g, wt, tm, tn, row_block0):
    m, k = x.shape
    n_gate = 2 * D_MODEL
    assert tn == SSM_WIDTH
    return pl.pallas_call(
        _ug_proj_kernel,
        grid=(m // tm, 1 + n_gate // tn),
        in_specs=[
            pl.BlockSpec((tm, k), lambda i, j: (i, 0)),
            pl.BlockSpec((1, k), lambda i, j: (0, 0)),
            pl.BlockSpec((tn, k), lambda i, j: (row_block0 + j, 0)),
        ],
        out_specs=[
            pl.BlockSpec((N_TILE, tm // CHUNK, CW), lambda i, j: (0, i, 0)),
            pl.BlockSpec((tm, tn), lambda i, j: (i, jnp.maximum(j - 1, 0))),
        ],
        out_shape=[jax.ShapeDtypeStruct((N_TILE, m // CHUNK, CW), BF16),
                   jax.ShapeDtypeStruct((m, n_gate), BF16)],
        scratch_shapes=[pltpu.VMEM((tm, k), BF16), pltpu.VMEM((N_TILE, tm, LANES), F32)],
        compiler_params=_cparams("parallel", "arbitrary"),
        name="ug_proj",
    )(x, g, wt)


def _rope_slab(x, cosf, sin1, sin2):
    return (x * cosf + pltpu.roll(x, LANES - QK_ROPE // 2, 1) * sin1
            + pltpu.roll(x, QK_ROPE // 2, 1) * sin2)


def _up_rope_kernel(ql_ref, kvl_ref, kr_ref, gq_ref, gkv_ref, wq_ref, wkv_ref,
                    cos_ref, sin1_ref, sin2_ref, q_ref, k_ref, v_ref):
    cosf, sin1, sin2 = cos_ref[...], sin1_ref[...], sin2_ref[...]
    qn = _rms(ql_ref[...], gq_ref[...]).astype(BF16)
    kvn = _rms(kvl_ref[...], gkv_ref[...]).astype(BF16)
    k_rope = _rope_slab(kr_ref[...], cosf, sin1, sin2).astype(BF16)
    ones_cols = (lax.broadcasted_iota(jnp.int32, (ql_ref.shape[0], V_PAD - V_HEAD), 1) == 0
                 ).astype(F32).astype(BF16)
    for h in range(HEADS):
        acc = jnp.dot(qn, wq_ref[:, h * HEAD_PAD:(h + 1) * HEAD_PAD], preferred_element_type=F32)
        q_ref[h, :, :LANES] = acc[:, :LANES].astype(BF16)
        q_ref[h, :, LANES:] = _rope_slab(acc[:, LANES:], cosf, sin1, sin2).astype(BF16)
    kv_w = QK_NOPE + V_HEAD
    for h in range(HEADS):
        acc = jnp.dot(kvn, wkv_ref[:, h * kv_w:(h + 1) * kv_w], preferred_element_type=F32)
        k_ref[h, :, :LANES] = acc[:, :QK_NOPE].astype(BF16)
        k_ref[h, :, LANES:] = k_rope
        v_ref[h, :, :V_HEAD] = acc[:, QK_NOPE:].astype(BF16)
        v_ref[h, :, V_HEAD:] = ones_cols


def _up_rope(lat, gq, gkv, wq, wkv, cosf, sin1, sin2, tm):
    m = lat.shape[0]
    n_pos_blocks = cosf.shape[0] // tm
    row = lambda i: (i, 0)
    const = lambda i: (0, 0)
    pos = lambda i: (i % n_pos_blocks, 0)
    head_major = lambda i: (0, i, 0)
    return pl.pallas_call(
        _up_rope_kernel,
        grid=(m // tm,),
        in_specs=[
            pl.BlockSpec((tm, Q_LORA), row),
            pl.BlockSpec((tm, KV_LORA), lambda i: (i, 1)),
            pl.BlockSpec((tm, LANES), lambda i: (i, (Q_LORA + KV_LORA) // LANES)),
            pl.BlockSpec((1, Q_LORA), const),
            pl.BlockSpec((1, KV_LORA), const),
            pl.BlockSpec(wq.shape, const),
            pl.BlockSpec(wkv.shape, const),
            pl.BlockSpec((tm, LANES), pos),
            pl.BlockSpec((tm, LANES), pos),
            pl.BlockSpec((tm, LANES), pos),
        ],
        out_specs=[
            pl.BlockSpec((HEADS, tm, HEAD_PAD), head_major),
            pl.BlockSpec((HEADS, tm, HEAD_PAD), head_major),
            pl.BlockSpec((HEADS, tm, V_PAD), head_major),
        ],
        out_shape=[
            jax.ShapeDtypeStruct((HEADS, m, HEAD_PAD), BF16),
            jax.ShapeDtypeStruct((HEADS, m, HEAD_PAD), BF16),
            jax.ShapeDtypeStruct((HEADS, m, V_PAD), BF16),
        ],
        compiler_params=_cparams("parallel"),
        name="up_rope",
    )(lat, lat, lat, gq, gkv, wq, wkv, cosf, sin1, sin2)


def _attn_kernel(qi_ref, ki_ref, q_ref, k_ref, v_ref, km_ref, vm_ref, o_ref, m_ref, acc_ref):
    t = pl.program_id(1)
    qi = qi_ref[t]
    ki = ki_ref[t]

    def lane_tiled(x, width):
        return jnp.concatenate([x] * (width // LANES), axis=1)

    @pl.when(ki == 0)
    def _init():
        def body(h, c):
            s = lax.dot_general(q_ref[h], km_ref[h], NT_DIMS, preferred_element_type=F32)
            key = lax.broadcasted_iota(jnp.int32, s.shape, 1)
            s = jnp.where(key < N_META, s, -jnp.inf)
            m = jnp.broadcast_to(jnp.max(s, axis=-1, keepdims=True), s.shape)
            p = jnp.exp2(s - m)
            m_ref[h] = m
            acc_ref[h] = jnp.dot(p.astype(BF16), vm_ref[h], preferred_element_type=F32)
            return c
        lax.fori_loop(0, HEADS, body, 0, unroll=2)

    def tile(masked):
        def scores(h):
            return lax.dot_general(q_ref[h], k_ref[h], NT_DIMS, preferred_element_type=F32)

        def body(h, s):
            if masked:
                qry = lax.broadcasted_iota(jnp.int32, s.shape, 0)
                key = lax.broadcasted_iota(jnp.int32, s.shape, 1)
                s = jnp.where(key <= qry, s, -jnp.inf)
            m_prev = m_ref[h]
            m_new = jnp.maximum(m_prev, jnp.max(s, axis=-1, keepdims=True))
            alpha = jnp.exp2(m_prev - m_new)
            p = jnp.exp2(s - lane_tiled(m_new, s.shape[1]))
            acc_ref[h] = lane_tiled(alpha, V_PAD) * acc_ref[h] + jnp.dot(
                p.astype(BF16), v_ref[h], preferred_element_type=F32)
            m_ref[h] = m_new

        s_next = scores(0)
        for h in range(HEADS):
            s_cur = s_next
            if h + 1 < HEADS:
                s_next = scores(h + 1)
            body(h, s_cur)

    @pl.when(ki < qi)
    def _full():
        tile(False)

    @pl.when(ki == qi)
    def _diag():
        tile(True)
        for h in range(HEADS):
            out = acc_ref[h, :, :V_HEAD] / acc_ref[h, :, V_HEAD:V_HEAD + 1]
            o_ref[:, h * V_HEAD:(h + 1) * V_HEAD] = out.astype(o_ref.dtype)


def _attention(q, k, v, km, vm, bsz, seq, tq):
    nq = seq // tq
    pairs = [(a, b) for a in range(nq) for b in range(a + 1)]
    qi_arr = jnp.asarray([p[0] for p in pairs], jnp.int32)
    ki_arr = jnp.asarray([p[1] for p in pairs], jnp.int32)
    grid_spec = pltpu.PrefetchScalarGridSpec(
        num_scalar_prefetch=2,
        grid=(bsz, len(pairs)),
        in_specs=[
            pl.BlockSpec((HEADS, tq, HEAD_PAD), lambda b, t, qi, ki: (0, b * nq + qi[t], 0)),
            pl.BlockSpec((HEADS, tq, HEAD_PAD), lambda b, t, qi, ki: (0, b * nq + ki[t], 0)),
            pl.BlockSpec((HEADS, tq, V_PAD), lambda b, t, qi, ki: (0, b * nq + ki[t], 0)),
            pl.BlockSpec(km.shape, lambda b, t, qi, ki: (0, 0, 0)),
            pl.BlockSpec(vm.shape, lambda b, t, qi, ki: (0, 0, 0)),
        ],
        out_specs=pl.BlockSpec((tq, HEADS * V_HEAD), lambda b, t, qi, ki: (b * nq + qi[t], 0)),
        scratch_shapes=[
            pltpu.VMEM((HEADS, tq, LANES), F32),
            pltpu.VMEM((HEADS, tq, V_PAD), F32),
        ],
    )
    return pl.pallas_call(
        _attn_kernel,
        grid_spec=grid_spec,
        out_shape=jax.ShapeDtypeStruct((bsz * seq, HEADS * V_HEAD), BF16),
        compiler_params=_cparams("parallel", "arbitrary"),
        name="flash_attn",
    )(qi_arr, ki_arr, q, k, v, km, vm)


def _ssm_param_kernel(lr_ref, li_ref, ls_ref, bre_ref, bim_ref, cre_ref, cim_ref,
                      e_re_ref, e_im_ref, ce_re_ref, ce_im_ref, eb_re_ref, eb_im_ref, t_ref):
    nt = (((1,), (1,)), ((), ()))
    for g in range(lr_ref.shape[0]):
        lr, li = lr_ref[g], li_ref[g]
        step = jnp.exp(ls_ref[g])
        kf = lax.broadcasted_iota(jnp.int32, (CHUNK + 1, SSM_N), 0).astype(F32)
        mag = jnp.exp(kf * (lr * step))
        ang = kf * (li * step)
        e_re, e_im = mag * jnp.cos(ang), mag * jnp.sin(ang)
        e_re_ref[g], e_im_ref[g] = e_re, e_im
        nr, ni = e_re[1:2] - 1.0, e_im[1:2]
        den = lr * lr + li * li
        coef_re = (nr * lr + ni * li) / den
        coef_im = (ni * lr - nr * li) / den
        bre, bim = bre_ref[g], bim_ref[g]
        bb_re = coef_re * bre - coef_im * bim
        bb_im = coef_re * bim + coef_im * bre
        cre, cim = cre_ref[g], cim_ref[g]
        ce_re_lags, ce_im_lags = [], []
        for k in range(CHUNK + 1):
            er, ei = e_re[k:k + 1], e_im[k:k + 1]
            ce_re = cre * er - cim * ei
            ce_im = cre * ei + cim * er
            ce_re_ref[g, k], ce_im_ref[g, k] = ce_re, ce_im
            if k < CHUNK:
                ce_re_lags.append(ce_re)
                ce_im_lags.append(ce_im)
                eb_re_ref[g, CHUNK - 1 - k] = er * bb_re - ei * bb_im
                eb_im_ref[g, CHUNK - 1 - k] = er * bb_im + ei * bb_re
        t_ref[g] = (
            lax.dot_general(jnp.concatenate(ce_re_lags, axis=0), bb_re, nt,
                            precision=lax.Precision.HIGHEST, preferred_element_type=F32)
            - lax.dot_general(jnp.concatenate(ce_im_lags, axis=0), bb_im, nt,
                              precision=lax.Precision.HIGHEST, preferred_element_type=F32))


def _ssm_params(lam_re, lam_im, log_step, b_re, b_im, c_re, c_im, gb):
    g = SSM_G
    kp1 = CHUNK + 1
    row = lambda shape: pl.BlockSpec((gb,) + shape, lambda i: (i,) + (0,) * len(shape))
    outs = [
        ((g, kp1, SSM_N), (kp1, SSM_N)), ((g, kp1, SSM_N), (kp1, SSM_N)),
        ((g, kp1, SSM_P, SSM_N), (kp1, SSM_P, SSM_N)), ((g, kp1, SSM_P, SSM_N), (kp1, SSM_P, SSM_N)),
        ((g, CHUNK, SSM_P, SSM_N), (CHUNK, SSM_P, SSM_N)), ((g, CHUNK, SSM_P, SSM_N), (CHUNK, SSM_P, SSM_N)),
        ((g, CP, SSM_P), (CP, SSM_P)),
    ]
    return pl.pallas_call(
        _ssm_param_kernel,
        grid=(g // gb,),
        in_specs=[row((1, SSM_N)), row((1, SSM_N)), row((1, 1)),
                  row((SSM_P, SSM_N)), row((SSM_P, SSM_N)), row((SSM_P, SSM_N)), row((SSM_P, SSM_N))],
        out_specs=[row(blk) for _, blk in outs],
        out_shape=[jax.ShapeDtypeStruct(full, F32) for full, _ in outs],
        compiler_params=_cparams("parallel"),
        name="ssm_params",
    )(lam_re.reshape(g, 1, SSM_N), lam_im.reshape(g, 1, SSM_N), log_step.reshape(g, 1, 1),
      jnp.swapaxes(b_re, 1, 2), jnp.swapaxes(b_im, 1, 2), c_re, c_im)


def _ssm_kernel(u_ref, um_ref, wc_ref, a1_ref, a2_ref, tc_ref, ce_ref, y_ref,
                s_ref, s0_ref, win_ref, wt_ref, ws_ref, *, bsz, n_chunks):
    i = pl.program_id(0)
    rows = bsz * n_chunks

    @pl.when(i < N_TILE)
    def _state_inputs():
        win_ref[...] = jnp.zeros(win_ref.shape, win_ref.dtype)
        for sig in range(CHUNK):
            for g in range(GPT):
                r0 = sig * LANES + g * SSM_P
                win_ref[r0:r0 + SSM_P, g * 2 * SSM_N:(g + 1) * 2 * SSM_N] = (
                    wc_ref[g, sig * SSM_P:(sig + 1) * SSM_P, :])
        v = jnp.dot(u_ref[0], win_ref[...], preferred_element_type=F32)
        v_meta = jnp.dot(um_ref[0], win_ref[...], preferred_element_type=F32)
        for g in range(GPT):
            s_ref[g, pl.ds(i, rows, stride=N_TILE), :] = v[:, g * 2 * SSM_N:(g + 1) * 2 * SSM_N]
            s0_ref[g, pl.ds(i, 1), :] = v_meta[0:1, g * 2 * SSM_N:(g + 1) * 2 * SSM_N]

    @pl.when(i == N_TILE - 1)
    def _recurrence():
        a1 = [a1_ref[:, g * 2 * SSM_N:(g + 1) * 2 * SSM_N] for g in range(GPT)]
        a2 = [a2_ref[:, g * 2 * SSM_N:(g + 1) * 2 * SSM_N] for g in range(GPT)]
        for b in range(bsz):
            def body(j, state):
                r = pl.multiple_of((b * n_chunks + j) * N_TILE, N_TILE)
                nxt = []
                for g in range(GPT):
                    v = s_ref[g, pl.ds(r, N_TILE), :]
                    s_ref[g, pl.ds(r, N_TILE), :] = state[g]
                    nxt.append(a1[g] * state[g] + a2[g] * pltpu.roll(state[g], SSM_N, 1) + v)
                return tuple(nxt)
            lax.fori_loop(0, n_chunks, body, tuple(s0_ref[g] for g in range(GPT)))

    @pl.when(i >= N_TILE)
    def _outputs():
        tile = i - N_TILE
        row_g = lax.shift_right_logical(lax.broadcasted_iota(jnp.int32, (LANES, LANES), 0), 4)
        col_g = lax.shift_right_logical(lax.broadcasted_iota(jnp.int32, (LANES, LANES), 1), 4)
        same_group = row_g == col_g
        zero_blk = jnp.zeros((LANES, LANES), BF16)
        lag_blocks = []
        for k in range(CHUNK):
            rep = jnp.concatenate([tc_ref[0, k]] * GPT, axis=0)
            lag_blocks.append(jnp.where(same_group, rep, 0.0).astype(BF16))
        for sig in range(CHUNK):
            for tau in range(CHUNK):
                blk = lag_blocks[tau - sig] if tau >= sig else zero_blk
                wt_ref[sig * LANES:(sig + 1) * LANES, tau * LANES:(tau + 1) * LANES] = blk
        for g in range(GPT):
            own = col_g == g
            for tau in range(CHUNK):
                ws_ref[g * LANES:(g + 1) * LANES, tau * LANES:(tau + 1) * LANES] = (
                    jnp.where(own, ce_ref[0, tau], 0.0).astype(BF16))
        states = jnp.concatenate(
            [s_ref[g, pl.ds(tile, rows, stride=N_TILE), :] for g in range(GPT)], axis=1).astype(BF16)
        y = _gelu_tanh(jnp.dot(u_ref[0], wt_ref[...], preferred_element_type=F32)
                       + jnp.dot(states, ws_ref[...], preferred_element_type=F32))
        for tau in range(CHUNK):
            y_ref[pl.ds(tau, rows, stride=CHUNK), :] = y[:, tau * LANES:(tau + 1) * LANES]


def _ssm(u, u_meta, w_compact, a1, a2, t_compact, ce_compact, bsz, n_chunks):
    rows = bsz * n_chunks
    in_tile = lambda i: (lax.rem(i, N_TILE), 0, 0)
    out_tile4 = lambda i: (jnp.maximum(i - N_TILE, 0), 0, 0, 0)
    whole = lambda i: (0, 0)
    return pl.pallas_call(
        functools.partial(_ssm_kernel, bsz=bsz, n_chunks=n_chunks),
        grid=(2 * N_TILE,),
        in_specs=[pl.BlockSpec((1, rows, CW), in_tile),
                  pl.BlockSpec((1,) + u_meta.shape[1:], in_tile),
                  pl.BlockSpec((GPT, CP, 2 * SSM_N), in_tile),
                  pl.BlockSpec((N_TILE, SW), whole),
                  pl.BlockSpec((N_TILE, SW), whole),
                  pl.BlockSpec((1, CHUNK, SSM_P, LANES), out_tile4),
                  pl.BlockSpec((1, CHUNK, 2 * SSM_N, LANES), out_tile4)],
        out_specs=pl.BlockSpec((rows * CHUNK, LANES), lambda i: (0, jnp.maximum(i - N_TILE, 0))),
        out_shape=jax.ShapeDtypeStruct((rows * CHUNK, SSM_WIDTH), F32),
        scratch_shapes=[pltpu.VMEM((GPT, rows * N_TILE, 2 * SSM_N), F32),
                        pltpu.VMEM((GPT, N_TILE, 2 * SSM_N), F32),
                        pltpu.VMEM((CW, SW), BF16),
                        pltpu.VMEM((CW, CW), BF16),
                        pltpu.VMEM((SW, CW), BF16)],
        compiler_params=_cparams("arbitrary"),
        name="ssm",
    )(u, u_meta, w_compact, a1, a2, t_compact, ce_compact)


def _merge_kernel(attn_ref, y_ref, wap_ref, wv_ref, wg_ref, g0_ref, g1_ref, o_ref):
    attn = attn_ref[...]
    y = y_ref[...].astype(BF16)
    for c in range(o_ref.shape[1] // MXU_N):
        cols = slice(c * MXU_N, (c + 1) * MXU_N)
        a = jnp.dot(attn, wap_ref[:, cols], preferred_element_type=F32)
        ssm = (jnp.dot(y, wv_ref[:, cols], preferred_element_type=F32)
               * _sigmoid(jnp.dot(y, wg_ref[:, cols], preferred_element_type=F32)))
        mixed = (_sigmoid(g0_ref[:, cols].astype(F32)) * a
                 + _sigmoid(g1_ref[:, cols].astype(F32)) * ssm)
        o_ref[:, cols] = mixed.astype(o_ref.dtype)


def _merge(attn, y, wap, wv, wg, gates, tm, tn):
    m = attn.shape[0]
    g1_off = D_MODEL // tn
    return pl.pallas_call(
        _merge_kernel,
        grid=(m // tm, D_MODEL // tn),
        in_specs=[
            pl.BlockSpec((tm, attn.shape[1]), lambda i, j: (i, 0)),
            pl.BlockSpec((tm, SSM_WIDTH), lambda i, j: (i, 0)),
            pl.BlockSpec((attn.shape[1], tn), lambda i, j: (0, j)),
            pl.BlockSpec((SSM_WIDTH, tn), lambda i, j: (0, j)),
            pl.BlockSpec((SSM_WIDTH, tn), lambda i, j: (0, j)),
            pl.BlockSpec((tm, tn), lambda i, j: (i, j)),
            pl.BlockSpec((tm, tn), lambda i, j: (i, g1_off + j)),
        ],
        out_specs=pl.BlockSpec((tm, tn), lambda i, j: (i, j)),
        out_shape=jax.ShapeDtypeStruct((m, D_MODEL), BF16),
        compiler_params=_cparams("parallel", "arbitrary"),
        name="gated_merge",
    )(attn, y, wap, wv, wg, gates, gates)


def _out_proj_kernel(x_ref, mx_ref, w_ref, g_ref, h_ref, n_ref):
    mx = mx_ref[...]
    d = h_ref.shape[1]
    sq = None
    for c in range(d // MXU_N):
        cols = slice(c * MXU_N, (c + 1) * MXU_N)
        h = x_ref[:, cols] + jnp.dot(mx, w_ref[:, cols], preferred_element_type=F32)
        h_ref[:, cols] = h
        part = jnp.sum(h * h, axis=-1, keepdims=True)
        sq = part if sq is None else sq + part
    inv = lax.rsqrt(sq * (1.0 / d) + EPS)
    n_ref[...] = (h_ref[...] * inv * g_ref[...]).astype(n_ref.dtype)


def _out_proj(x, mixed, w, g, tm):
    m, d = x.shape
    row = pl.BlockSpec((tm, d), lambda i: (i, 0))
    return pl.pallas_call(
        _out_proj_kernel,
        grid=(m // tm,),
        in_specs=[row, row, pl.BlockSpec((d, d), lambda i: (0, 0)), pl.BlockSpec((1, d), lambda i: (0, 0))],
        out_specs=[row, row],
        out_shape=[jax.ShapeDtypeStruct((m, d), F32), jax.ShapeDtypeStruct((m, d), BF16)],
        compiler_params=_cparams("parallel"),
        name="out_proj",
    )(x, mixed, w, g)


def _ffn_kernel(n_ref, h_ref, wg_ref, wu_ref, wd_ref, gf_ref, o_ref, acc_ref):
    f = pl.program_id(1)
    n = n_ref[...]
    tf = wg_ref.shape[1]
    d = wd_ref.shape[1]
    @pl.when(f == 0)
    def _():
        acc_ref[...] = jnp.zeros(acc_ref.shape, acc_ref.dtype)

    acts = []
    for c in range(tf // MXU_N):
        cols = slice(c * MXU_N, (c + 1) * MXU_N)
        gate = jnp.dot(n, wg_ref[:, cols], preferred_element_type=F32)
        up = jnp.dot(n, wu_ref[:, cols], preferred_element_type=F32)
        acts.append((gate * _sigmoid(gate) * up).astype(BF16))

    for c in range(d // FFN_ACC_COLS):
        cols = slice(c * FFN_ACC_COLS, (c + 1) * FFN_ACC_COLS)
        part = None
        for k, a in enumerate(acts):
            term = jnp.dot(a, wd_ref[k * MXU_N:(k + 1) * MXU_N, cols], preferred_element_type=F32)
            part = term if part is None else part + term
        acc_ref[:, cols] += part

    @pl.when(f == pl.num_programs(1) - 1)
    def _():
        o_ref[...] = _rms(h_ref[...] + acc_ref[...], gf_ref[...])


def _ffn(n, h, wg, wu, wd, gf, tm, tf):
    m, d = h.shape
    row = lambda i, f: (i, 0)
    return pl.pallas_call(
        _ffn_kernel,
        grid=(m // tm, D_FF // tf),
        in_specs=[
            pl.BlockSpec((tm, d), row),
            pl.BlockSpec((tm, d), row),
            pl.BlockSpec((d, tf), lambda i, f: (0, f)),
            pl.BlockSpec((d, tf), lambda i, f: (0, f)),
            pl.BlockSpec((tf, d), lambda i, f: (f, 0)),
            pl.BlockSpec((1, d), lambda i, f: (0, 0)),
        ],
        out_specs=pl.BlockSpec((tm, d), row),
        out_shape=jax.ShapeDtypeStruct((m, d), F32),
        scratch_shapes=[pltpu.VMEM((tm, d), F32)],
        compiler_params=_cparams("parallel", "arbitrary"),
        name="ffn",
    )(n, h, wg, wu, wd, gf)


def _rope_tables(length):
    half = QK_ROPE // 2
    freqs = np.float32(ROPE_THETA) ** (-np.arange(half, dtype=np.float32) / np.float32(half))
    ang = (np.arange(length, dtype=np.float32)[:, None] * freqs[None, :]).astype(np.float32)
    c, s = np.cos(ang).astype(np.float32), np.sin(ang).astype(np.float32)
    z = np.zeros_like(c)
    cosf = np.concatenate([c, c, z, z], axis=1)
    sin1 = np.concatenate([-s, z, z, z], axis=1)
    sin2 = np.concatenate([z, s, z, z], axis=1)
    return cosf, sin1, sin2


def _chunk_operators(e_re, e_im, ce_re, ce_im, eb_re, eb_im, t, d):
    g = SSM_G
    w_in = jnp.concatenate([eb_re, eb_im], axis=-1).reshape(g, CP, 2 * SSM_N)
    st = jnp.concatenate([ce_re[:, 1:], -ce_im[:, 1:]], axis=-1)
    ce_t = st.reshape(N_TILE, GPT, CHUNK, SSM_P, 2 * SSM_N).transpose(0, 2, 4, 1, 3)
    ce_t = ce_t.reshape(N_TILE, CHUNK, 2 * SSM_N, LANES)
    t = t.reshape(g, CHUNK, SSM_P, SSM_P).at[:, 0].add(jax.vmap(jnp.diag)(d))
    t_c = t.reshape(N_TILE, GPT, CHUNK, SSM_P, SSM_P).transpose(0, 2, 4, 1, 3)
    t_c = t_c.reshape(N_TILE, CHUNK, SSM_P, LANES)
    a1 = jnp.concatenate([e_re[:, CHUNK], e_re[:, CHUNK]], axis=-1)
    a2 = jnp.concatenate([-e_im[:, CHUNK], e_im[:, CHUNK]], axis=-1)
    return w_in.astype(BF16), ce_t, t_c, a1, a2


def kernel(x, meta_tokens, norm_mix, w_in, norm_q, w_q_up, norm_kv, w_kv_up, w_attn_proj, ssm_lambda_re, ssm_lambda_im, ssm_log_step, ssm_b_re, ssm_b_im, ssm_c_re, ssm_c_im, ssm_d, w_glu_val, w_glu_gate, w_out, norm_ffn, w_ffn_gate, w_ffn_up, w_ffn_down, norm_final):
    bsz, seq, d = x.shape
    assert d == D_MODEL and w_in.shape[0] == 1 and seq % 512 == 0 and N_META == CHUNK
    t_rows = bsz * seq
    n_chunks = seq // CHUNK
    bj = bsz * n_chunks
    x2 = x.reshape(t_rows, d)
    meta = jnp.pad(meta_tokens.astype(x.dtype), ((0, META_ROWS - N_META), (0, 0)))

    w_all = _w_in_prep(jnp.swapaxes(w_in[0], 0, 1), 256)
    scale = (QK_NOPE + QK_ROPE) ** -0.5 * math.log2(math.e)
    wq = jnp.pad((w_q_up[0] * scale).reshape(Q_LORA, HEADS, QK_NOPE + QK_ROPE),
                 ((0, 0), (0, 0), (0, HEAD_PAD - QK_NOPE - QK_ROPE))).reshape(Q_LORA, HEADS * HEAD_PAD).astype(BF16)
    wkv = w_kv_up[0].astype(BF16)
    g_mix = norm_mix[0].reshape(1, d)
    g_q = norm_q[0].reshape(1, Q_LORA)
    g_kv = norm_kv[0].reshape(1, KV_LORA)
    cosf, sin1, sin2 = _rope_tables(N_META + seq)

    lat = _norm_matmul(x2, g_mix, w_all, F32, 512, LAT_W, LAT_W, 0)
    u_seq, gates = _ug_proj(x2, g_mix, w_all, 1024, SSM_WIDTH, UG_ROW0 // SSM_WIDTH)
    lat_m = _norm_matmul(meta, g_mix, w_all, F32, META_ROWS, LAT_W, LAT_W, 0)
    u_m = _norm_matmul(meta, g_mix, w_all, BF16, META_ROWS, SSM_WIDTH, SSM_WIDTH, UG_ROW0 // SSM_WIDTH)

    q, k, v = _up_rope(lat, g_q, g_kv, wq, wkv, cosf[N_META:], sin1[N_META:], sin2[N_META:], 512)
    _, k_m, v_m = _up_rope(lat_m, g_q, g_kv, wq, wkv, cosf[:META_ROWS], sin1[:META_ROWS],
                            sin2[:META_ROWS], META_ROWS)
    attn = _attention(q, k, v, k_m, v_m, bsz, seq, 512)

    e_re, e_im, ce_re, ce_im, eb_re, eb_im, t_blk = _ssm_params(
        ssm_lambda_re[0], ssm_lambda_im[0], ssm_log_step[0], ssm_b_re[0], ssm_b_im[0],
        ssm_c_re[0], ssm_c_im[0], 8)
    w_sin, ce_t, t_c, a1, a2 = _chunk_operators(e_re, e_im, ce_re, ce_im, eb_re, eb_im, t_blk, ssm_d[0])
    u_meta = jnp.pad(u_m[:N_META].reshape(1, CHUNK, N_TILE, LANES).transpose(2, 0, 1, 3).reshape(N_TILE, 1, CW),
                     ((0, 0), (0, 15), (0, 0)))
    y = _ssm(u_seq, u_meta, w_sin, a1.reshape(N_TILE, SW), a2.reshape(N_TILE, SW), t_c, ce_t, bsz, n_chunks)

    mixed = _merge(attn, y, w_attn_proj[0].astype(BF16), w_glu_val[0].astype(BF16),
                   w_glu_gate[0].astype(BF16), gates, 1024, 512)
    h1, n_ffn = _out_proj(x2, mixed, w_out[0].astype(BF16), norm_ffn[0].reshape(1, d), 512)
    out = _ffn(n_ffn, h1, w_ffn_gate[0].astype(BF16), w_ffn_up[0].astype(BF16),
               w_ffn_down[0].astype(BF16), norm_final.reshape(1, d), 512, 512)
    return out.reshape(bsz, seq, d)
```

```python
import functools
import math

import numpy as np
import jax
import jax.numpy as jnp
from jax import lax
from jax.experimental import pallas as pl
from jax.experimental.pallas import tpu as pltpu

F32 = jnp.float32
BF16 = jnp.bfloat16

D_MODEL = 2048
N_META = 16
EPS = 1e-6
HEADS = 16
Q_LORA = 512
KV_LORA = 512
QK_NOPE = 128
QK_ROPE = 64
V_HEAD = 128
ROPE_THETA = 10000.0
SSM_WIDTH = 1024
SSM_P = 16
SSM_G = 64
SSM_N = 64
D_FF = 5632
LANES = 128
MXU_N = 256
FFN_ACC_COLS = 512
SCAN_UNROLL = 8
HEAD_PAD = 2 * LANES
LAT_W = Q_LORA + KV_LORA + LANES
CHUNK = 16
CP = CHUNK * SSM_P
GPT = LANES // SSM_P
N_TILE = SSM_G // GPT
CW = CHUNK * LANES
SW = GPT * 2 * SSM_N
UG_ROW0 = 2048
NT_DIMS = (((1,), (1,)), ((), ()))
META_ROWS = 128
V_PAD = 2 * LANES
VMEM_LIMIT = 56 * 1024 * 1024


def _cparams(*sem):
    return pltpu.CompilerParams(dimension_semantics=sem, vmem_limit_bytes=VMEM_LIMIT)


def _sigmoid(x):
    return 1.0 / (1.0 + jnp.exp(-x))


def _gelu_tanh(x):
    return x * (0.5 * (1.0 + jnp.tanh(math.sqrt(2.0 / math.pi) * (x + 0.044715 * (x * x * x)))))


def _rms(x, g):
    ms = jnp.mean(x * x, axis=-1, keepdims=True)
    return x * lax.rsqrt(ms + EPS) * g


def _w_in_prep_kernel(w_ref, o_ref):
    off_u = Q_LORA + KV_LORA + QK_ROPE
    o_ref[:off_u] = w_ref[:off_u].astype(BF16)
    o_ref[off_u:UG_ROW0] = jnp.zeros((UG_ROW0 - off_u, o_ref.shape[1]), BF16)
    o_ref[UG_ROW0:] = w_ref[off_u:].astype(BF16)


def _w_in_prep(wt, cb):
    n, k = wt.shape
    n_out = UG_ROW0 + n - (Q_LORA + KV_LORA + QK_ROPE)
    return pl.pallas_call(
        _w_in_prep_kernel,
        grid=(k // cb,),
        in_specs=[pl.BlockSpec((n, cb), lambda i: (0, i))],
        out_specs=pl.BlockSpec((n_out, cb), lambda i: (0, i)),
        out_shape=jax.ShapeDtypeStruct((n_out, k), BF16),
        compiler_params=_cparams("parallel"),
        name="w_in_prep",
    )(wt)


def _norm_matmul_kernel(x_ref, g_ref, w_ref, o_ref, n_ref):
    @pl.when(pl.program_id(1) == 0)
    def _():
        n_ref[...] = _rms(x_ref[...], g_ref[...]).astype(BF16)

    n = n_ref[...]
    tn = o_ref.shape[1]
    for c0 in range(0, tn, 2 * MXU_N):
        cols = slice(c0, min(c0 + 2 * MXU_N, tn))
        o_ref[:, cols] = lax.dot_general(n, w_ref[cols, :], NT_DIMS,
                                         preferred_element_type=F32).astype(o_ref.dtype)


def _norm_matmul(x, g, wt, out_dtype, tm, tn, n, row_block0):
    m, k = x.shape
    return pl.pallas_call(
        _norm_matmul_kernel,
        grid=(m // tm, n // tn),
        in_specs=[
            pl.BlockSpec((tm, k), lambda i, j: (i, 0)),
            pl.BlockSpec((1, k), lambda i, j: (0, 0)),
            pl.BlockSpec((tn, k), lambda i, j: (row_block0 + j, 0)),
        ],
        out_specs=pl.BlockSpec((tm, tn), lambda i, j: (i, j)),
        out_shape=jax.ShapeDtypeStruct((m, n), out_dtype),
        scratch_shapes=[pltpu.VMEM((tm, k), BF16)],
        compiler_params=_cparams("parallel", "arbitrary"),
        name="norm_matmul",
    )(x, g, wt)


def _ug_proj_kernel(x_ref, g_ref, w_ref, u_ref, gate_ref, n_ref, stage_ref):
    j = pl.program_id(1)
    tm = x_ref.shape[0]

    @pl.when(j == 0)
    def _():
        n_ref[...] = _rms(x_ref[...], g_ref[...]).astype(BF16)
        n = n_ref[...]
        for c in range(SSM_WIDTH // MXU_N):
            acc = lax.dot_general(n, w_ref[c * MXU_N:(c + 1) * MXU_N, :], NT_DIMS,
                                  preferred_element_type=F32)
            for s in range(MXU_N // LANES):
                stage_ref[c * (MXU_N // LANES) + s] = acc[:, s * LANES:(s + 1) * LANES]
        for t in range(N_TILE):
            for sig in range(CHUNK):
                u_ref[t, :, sig * LANES:(sig + 1) * LANES] = (
                    stage_ref[t, pl.ds(sig, tm // CHUNK, stride=CHUNK), :].astype(u_ref.dtype))

    @pl.when(j > 0)
    def _():
        n = n_ref[...]
        tn = gate_ref.shape[1]
        for c0 in range(0, tn, 2 * MXU_N):
            cols = slice(c0, c0 + 2 * MXU_N)
            gate_ref[:, cols] = lax.dot_general(n, w_ref[cols, :], NT_DIMS,
                                                preferred_element_type=F32).astype(gate_ref.dtype)


def _ug_proj(x, g, wt, tm, tn, row_block0):
    m, k = x.shape
    n_gate = 2 * D_MODEL
    assert tn == SSM_WIDTH
    return pl.pallas_call(
        _ug_proj_kernel,
        grid=(m // tm, 1 + n_gate // tn),
        in_specs=[
            pl.BlockSpec((tm, k), lambda i, j: (i, 0)),
            pl.BlockSpec((1, k), lambda i, j: (0, 0)),
            pl.BlockSpec((tn, k), lambda i, j: (row_block0 + j, 0)),
        ],
        out_specs=[
            pl.BlockSpec((N_TILE, tm // CHUNK, CW), lambda i, j: (0, i, 0)),
            pl.BlockSpec((tm, tn), lambda i, j: (i, jnp.maximum(j - 1, 0))),
        ],
        out_shape=[jax.ShapeDtypeStruct((N_TILE, m // CHUNK, CW), BF16),
                   jax.ShapeDtypeStruct((m, n_gate), BF16)],
        scratch_shapes=[pltpu.VMEM((tm, k), BF16), pltpu.VMEM((N_TILE, tm, LANES), F32)],
        compiler_params=_cparams("parallel", "arbitrary"),
        name="ug_proj",
    )(x, g, wt)


def _rope_slab(x, cosf, sin1, sin2):
    return (x * cosf + pltpu.roll(x, LANES - QK_ROPE // 2, 1) * sin1
            + pltpu.roll(x, QK_ROPE // 2, 1) * sin2)


def _up_rope_kernel(ql_ref, kvl_ref, kr_ref, gq_ref, gkv_ref, wq_ref, wkv_ref,
                    cos_ref, sin1_ref, sin2_ref, q_ref, k_ref, v_ref):
    cosf, sin1, sin2 = cos_ref[...], sin1_ref[...], sin2_ref[...]
    qn = _rms(ql_ref[...], gq_ref[...]).astype(BF16)
    kvn = _rms(kvl_ref[...], gkv_ref[...]).astype(BF16)
    k_rope = _rope_slab(kr_ref[...], cosf, sin1, sin2).astype(BF16)
    ones_cols = (lax.broadcasted_iota(jnp.int32, (ql_ref.shape[0], V_PAD - V_HEAD), 1) == 0
                 ).astype(F32).astype(BF16)
    for h in range(HEADS):
        acc = jnp.dot(qn, wq_ref[:, h * HEAD_PAD:(h + 1) * HEAD_PAD], preferred_element_type=F32)
        q_ref[h, :, :LANES] = acc[:, :LANES].astype(BF16)
        q_ref[h, :, LANES:] = _rope_slab(acc[:, LANES:], cosf, sin1, sin2).astype(BF16)
    kv_w = QK_NOPE + V_HEAD
    for h in range(HEADS):
        acc = jnp.dot(kvn, wkv_ref[:, h * kv_w:(h + 1) * kv_w], preferred_element_type=F32)
        k_ref[h, :, :LANES] = acc[:, :QK_NOPE].astype(BF16)
        k_ref[h, :, LANES:] = k_rope
        v_ref[h, :, :V_HEAD] = acc[:, QK_NOPE:].astype(BF16)
        v_ref[h, :, V_HEAD:] = ones_cols


def _up_rope(lat, gq, gkv, wq, wkv, cosf, sin1, sin2, tm):
    m = lat.shape[0]
    n_pos_blocks = cosf.shape[0] // tm
    row = lambda i: (i, 0)
    const = lambda i: (0, 0)
    pos = lambda i: (i % n_pos_blocks, 0)
    head_major = lambda i: (0, i, 0)
    return pl.pallas_call(
        _up_rope_kernel,
        grid=(m // tm,),
        in_specs=[
            pl.BlockSpec((tm, Q_LORA), row),
            pl.BlockSpec((tm, KV_LORA), lambda i: (i, 1)),
            pl.BlockSpec((tm, LANES), lambda i: (i, (Q_LORA + KV_LORA) // LANES)),
            pl.BlockSpec((1, Q_LORA), const),
            pl.BlockSpec((1, KV_LORA), const),
            pl.BlockSpec(wq.shape, const),
            pl.BlockSpec(wkv.shape, const),
            pl.BlockSpec((tm, LANES), pos),
            pl.BlockSpec((tm, LANES), pos),
            pl.BlockSpec((tm, LANES), pos),
        ],
        out_specs=[
            pl.BlockSpec((HEADS, tm, HEAD_PAD), head_major),
            pl.BlockSpec((HEADS, tm, HEAD_PAD), head_major),
            pl.BlockSpec((HEADS, tm, V_PAD), head_major),
        ],
        out_shape=[
            jax.ShapeDtypeStruct((HEADS, m, HEAD_PAD), BF16),
            jax.ShapeDtypeStruct((HEADS, m, HEAD_PAD), BF16),
            jax.ShapeDtypeStruct((HEADS, m, V_PAD), BF16),
        ],
        compiler_params=_cparams("parallel"),
        name="up_rope",
    )(lat, lat, lat, gq, gkv, wq, wkv, cosf, sin1, sin2)


def _attn_kernel(qi_ref, ki_ref, q_ref, k_ref, v_ref, km_ref, vm_ref, o_ref, m_ref, acc_ref):
    t = pl.program_id(1)
    qi = qi_ref[t]
    ki = ki_ref[t]

    def lane_tiled(x, width):
        return jnp.concatenate([x] * (width // LANES), axis=1)

    @pl.when(ki == 0)
    def _init():
        def body(h, c):
            s = lax.dot_general(q_ref[h], km_ref[h], NT_DIMS, preferred_element_type=F32)
            key = lax.broadcasted_iota(jnp.int32, s.shape, 1)
            s = jnp.where(key < N_META, s, -jnp.inf)
            m = jnp.broadcast_to(jnp.max(s, axis=-1, keepdims=True), s.shape)
            p = jnp.exp2(s - m)
            m_ref[h] = m
            acc_ref[h] = jnp.dot(p.astype(BF16), vm_ref[h], preferred_element_type=F32)
            return c
        lax.fori_loop(0, HEADS, body, 0, unroll=2)

    def tile(masked):
        def scores(h):
            return lax.dot_general(q_ref[h], k_ref[h], NT_DIMS, preferred_element_type=F32)

        def body(h, s):
            if masked:
                qry = lax.broadcasted_iota(jnp.int32, s.shape, 0)
                key = lax.broadcasted_iota(jnp.int32, s.shape, 1)
                s = jnp.where(key <= qry, s, -jnp.inf)
            m_prev = m_ref[h]
            m_new = jnp.maximum(m_prev, jnp.max(s, axis=-1, keepdims=True))
            alpha = jnp.exp2(m_prev - m_new)
            p = jnp.exp2(s - lane_tiled(m_new, s.shape[1]))
            acc_ref[h] = lane_tiled(alpha, V_PAD) * acc_ref[h] + jnp.dot(
                p.astype(BF16), v_ref[h], preferred_element_type=F32)
            m_ref[h] = m_new

        s_next = scores(0)
        for h in range(HEADS):
            s_cur = s_next
            if h + 1 < HEADS:
                s_next = scores(h + 1)
            body(h, s_cur)

    @pl.when(ki < qi)
    def _full():
        tile(False)

    @pl.when(ki == qi)
    def _diag():
        tile(True)
        for h in range(HEADS):
            out = acc_ref[h, :, :V_HEAD] / acc_ref[h, :, V_HEAD:V_HEAD + 1]
            o_ref[:, h * V_HEAD:(h + 1) * V_HEAD] = out.astype(o_ref.dtype)


def _attention(q, k, v, km, vm, bsz, seq, tq):
    nq = seq // tq
    pairs = [(a, b) for a in range(nq) for b in range(a + 1)]
    qi_arr = jnp.asarray([p[0] for p in pairs], jnp.int32)
    ki_arr = jnp.asarray([p[1] for p in pairs], jnp.int32)
    grid_spec = pltpu.PrefetchScalarGridSpec(
        num_scalar_prefetch=2,
        grid=(bsz, len(pairs)),
        in_specs=[
            pl.BlockSpec((HEADS, tq, HEAD_PAD), lambda b, t, qi, ki: (0, b * nq + qi[t], 0)),
            pl.BlockSpec((HEADS, tq, HEAD_PAD), lambda b, t, qi, ki: (0, b * nq + ki[t], 0)),
            pl.BlockSpec((HEADS, tq, V_PAD), lambda b, t, qi, ki: (0, b * nq + ki[t], 0)),
            pl.BlockSpec(km.shape, lambda b, t, qi, ki: (0, 0, 0)),
            pl.BlockSpec(vm.shape, lambda b, t, qi, ki: (0, 0, 0)),
        ],
        out_specs=pl.BlockSpec((tq, HEADS * V_HEAD), lambda b, t, qi, ki: (b * nq + qi[t], 0)),
        scratch_shapes=[
            pltpu.VMEM((HEADS, tq, LANES), F32),
            pltpu.VMEM((HEADS, tq, V_PAD), F32),
        ],
    )
    return pl.pallas_call(
        _attn_kernel,
        grid_spec=grid_spec,
        out_shape=jax.ShapeDtypeStruct((bsz * seq, HEADS * V_HEAD), BF16),
        compiler_params=_cparams("parallel", "arbitrary"),
        name="flash_attn",
    )(qi_arr, ki_arr, q, k, v, km, vm)


def _ssm_param_kernel(lr_ref, li_ref, ls_ref, bre_ref, bim_ref, cre_ref, cim_ref, d_ref,
                      win_ref, cen_ref, tl_ref, a1_ref, a2_ref):
    lane = lax.broadcasted_iota(jnp.int32, (SSM_P, LANES), 1)
    row = lax.broadcasted_iota(jnp.int32, (SSM_P, LANES), 0)
    on_diag = jnp.bitwise_and(lane, SSM_P - 1) == row
    cat = lambda x, y: jnp.concatenate([x, y], axis=1)
    for g in range(lr_ref.shape[0]):
        lr, li = lr_ref[g], li_ref[g]
        step = jnp.exp(ls_ref[g])
        kf = lax.broadcasted_iota(jnp.int32, (CHUNK + 1, SSM_N), 0).astype(F32)
        mag = jnp.exp(kf * (lr * step))
        ang = kf * (li * step)
        e_re, e_im = mag * jnp.cos(ang), mag * jnp.sin(ang)
        a1_ref[g] = cat(e_re[CHUNK:], e_re[CHUNK:])
        a2_ref[g] = cat(-e_im[CHUNK:], e_im[CHUNK:])
        nr, ni = e_re[1:2] - 1.0, e_im[1:2]
        den = lr * lr + li * li
        coef_re = (nr * lr + ni * li) / den
        coef_im = (ni * lr - nr * li) / den
        bre, bim = bre_ref[g], bim_ref[g]
        bb_re = coef_re * bre - coef_im * bim
        bb_im = coef_re * bim + coef_im * bre
        cre, cim = cre_ref[g], cim_ref[g]
        ce_re_lags, ce_im_lags = [], []
        for k in range(CHUNK + 1):
            er, ei = e_re[k:k + 1], e_im[k:k + 1]
            ce_re = cre * er - cim * ei
            ce_im = cre * ei + cim * er
            if k >= 1:
                cen_ref[g, k - 1] = cat(ce_re, -ce_im).astype(cen_ref.dtype)
            if k < CHUNK:
                ce_re_lags.append(ce_re)
                ce_im_lags.append(ce_im)
                r0 = (CHUNK - 1 - k) * SSM_P
                win_ref[g, r0:r0 + SSM_P, :] = cat(er * bb_re - ei * bb_im,
                                                   er * bb_im + ei * bb_re).astype(win_ref.dtype)
        tl = (lax.dot_general(jnp.concatenate(ce_re_lags, axis=0), jnp.concatenate([bb_re] * GPT, axis=0),
                              NT_DIMS, precision=lax.Precision.HIGHEST, preferred_element_type=F32)
              - lax.dot_general(jnp.concatenate(ce_im_lags, axis=0), jnp.concatenate([bb_im] * GPT, axis=0),
                                NT_DIMS, precision=lax.Precision.HIGHEST, preferred_element_type=F32))
        tl_ref[g] = tl
        tl_ref[g, :SSM_P, :] = tl[:SSM_P] + jnp.where(on_diag, d_ref[g], 0.0)


def _ssm_params(lam_re, lam_im, log_step, b_re, b_im, c_re, c_im, d, gb):
    g = SSM_G
    row = lambda shape: pl.BlockSpec((gb,) + shape, lambda i: (i,) + (0,) * len(shape))
    outs = [
        ((g, CP, 2 * SSM_N), (CP, 2 * SSM_N), BF16),
        ((g, CHUNK, SSM_P, 2 * SSM_N), (CHUNK, SSM_P, 2 * SSM_N), BF16),
        ((g, CP, LANES), (CP, LANES), F32),
        ((g, 1, 2 * SSM_N), (1, 2 * SSM_N), F32),
        ((g, 1, 2 * SSM_N), (1, 2 * SSM_N), F32),
    ]
    return pl.pallas_call(
        _ssm_param_kernel,
        grid=(g // gb,),
        in_specs=[row((1, SSM_N)), row((1, SSM_N)), row((1, 1)),
                  row((SSM_P, SSM_N)), row((SSM_P, SSM_N)), row((SSM_P, SSM_N)), row((SSM_P, SSM_N)),
                  row((SSM_P, 1))],
        out_specs=[row(blk) for _, blk, _ in outs],
        out_shape=[jax.ShapeDtypeStruct(full, dt) for full, _, dt in outs],
        compiler_params=_cparams("parallel"),
        name="ssm_params",
    )(lam_re.reshape(g, 1, SSM_N), lam_im.reshape(g, 1, SSM_N), log_step.reshape(g, 1, 1),
      jnp.swapaxes(b_re, 1, 2), jnp.swapaxes(b_im, 1, 2), c_re, c_im, d.reshape(g, SSM_P, 1))


def _ssm_kernel(u_ref, um_ref, wc_ref, a1_ref, a2_ref, tl_ref, cen_ref, y_ref,
                s_ref, s0_ref, win_ref, wt_ref, ws_ref, *, bsz, n_chunks):
    i = pl.program_id(0)
    rows = bsz * n_chunks

    @pl.when(i < N_TILE)
    def _state_inputs():
        @pl.when(i == 0)
        def _():
            win_ref[...] = jnp.zeros(win_ref.shape, win_ref.dtype)

        for sig in range(CHUNK):
            for g in range(GPT):
                r0 = sig * LANES + g * SSM_P
                win_ref[r0:r0 + SSM_P, g * 2 * SSM_N:(g + 1) * 2 * SSM_N] = (
                    wc_ref[g, sig * SSM_P:(sig + 1) * SSM_P, :])
        v = jnp.dot(u_ref[0], win_ref[...], preferred_element_type=F32)
        v_meta = jnp.dot(um_ref[0], win_ref[...], preferred_element_type=F32)
        for g in range(GPT):
            s_ref[g, pl.ds(i, rows, stride=N_TILE), :] = v[:, g * 2 * SSM_N:(g + 1) * 2 * SSM_N]
            s0_ref[g, pl.ds(i, 1), :] = v_meta[0:1, g * 2 * SSM_N:(g + 1) * 2 * SSM_N]

    @pl.when(i == N_TILE - 1)
    def _recurrence():
        a1 = [a1_ref[:, g * 2 * SSM_N:(g + 1) * 2 * SSM_N] for g in range(GPT)]
        a2 = [a2_ref[:, g * 2 * SSM_N:(g + 1) * 2 * SSM_N] for g in range(GPT)]
        for b in range(bsz):
            def body(j, carry):
                r = pl.multiple_of((b * n_chunks + j) * N_TILE, N_TILE)
                nxt = []
                for g in range(GPT):
                    s, sw = carry[2 * g], carry[2 * g + 1]
                    v = s_ref[g, pl.ds(r, N_TILE), :]
                    vw = pltpu.roll(v, SSM_N, 1)
                    s_ref[g, pl.ds(r, N_TILE), :] = s
                    nxt.append(a1[g] * s + a2[g] * sw + v)
                    nxt.append(a1[g] * sw - a2[g] * s + vw)
                return tuple(nxt)
            init = []
            for g in range(GPT):
                init += [s0_ref[g], pltpu.roll(s0_ref[g], SSM_N, 1)]
            lax.fori_loop(0, n_chunks, body, tuple(init), unroll=SCAN_UNROLL)

    @pl.when(i >= N_TILE)
    def _outputs():
        tile = i - N_TILE
        row_g = lax.shift_right_logical(lax.broadcasted_iota(jnp.int32, (LANES, LANES), 0), 4)
        col_g = lax.shift_right_logical(lax.broadcasted_iota(jnp.int32, (LANES, LANES), 1), 4)
        same_group = row_g == col_g
        @pl.when(i == N_TILE)
        def _():
            wt_ref[...] = jnp.zeros(wt_ref.shape, wt_ref.dtype)
            ws_ref[...] = jnp.zeros(ws_ref.shape, ws_ref.dtype)

        for k in range(CHUNK):
            rep = jnp.concatenate([tl_ref[g, k * SSM_P:(k + 1) * SSM_P, :] for g in range(GPT)], axis=0)
            blk = jnp.where(same_group, rep, 0.0).astype(BF16)
            for sig in range(CHUNK - k):
                tau = sig + k
                wt_ref[tau * LANES:(tau + 1) * LANES, sig * LANES:(sig + 1) * LANES] = blk
        for g in range(GPT):
            for tau in range(CHUNK):
                r0 = tau * LANES + g * SSM_P
                ws_ref[r0:r0 + SSM_P, g * 2 * SSM_N:(g + 1) * 2 * SSM_N] = cen_ref[g, tau]
        states = jnp.concatenate(
            [s_ref[g, pl.ds(tile, rows, stride=N_TILE), :] for g in range(GPT)], axis=1).astype(BF16)
        u = u_ref[0]
        steps_per_tile = MXU_N // LANES
        for c in range(CW // MXU_N):
            k_hi = (c + 1) * MXU_N
            out_rows = slice(c * MXU_N, k_hi)
            y = _gelu_tanh(
                lax.dot_general(u[:, :k_hi], wt_ref[out_rows, :k_hi], NT_DIMS, preferred_element_type=F32)
                + lax.dot_general(states, ws_ref[out_rows, :], NT_DIMS, preferred_element_type=F32))
            for s in range(steps_per_tile):
                tau = c * steps_per_tile + s
                y_ref[pl.ds(tau, rows, stride=CHUNK), :] = y[:, s * LANES:(s + 1) * LANES]


def _ssm(u, u_meta, w_compact, a1, a2, t_lags, ce_lags, bsz, n_chunks):
    rows = bsz * n_chunks
    in_tile = lambda i: (lax.rem(i, N_TILE), 0, 0)
    out_tile = lambda i: (jnp.maximum(i - N_TILE, 0), 0, 0)
    out_tile4 = lambda i: (jnp.maximum(i - N_TILE, 0), 0, 0, 0)
    whole = lambda i: (0, 0)
    return pl.pallas_call(
        functools.partial(_ssm_kernel, bsz=bsz, n_chunks=n_chunks),
        grid=(2 * N_TILE,),
        in_specs=[pl.BlockSpec((1, rows, CW), in_tile),
                  pl.BlockSpec((1,) + u_meta.shape[1:], in_tile),
                  pl.BlockSpec((GPT, CP, 2 * SSM_N), in_tile),
                  pl.BlockSpec((N_TILE, SW), whole),
                  pl.BlockSpec((N_TILE, SW), whole),
                  pl.BlockSpec((GPT, CP, LANES), out_tile),
                  pl.BlockSpec((GPT, CHUNK, SSM_P, 2 * SSM_N), out_tile4)],
        out_specs=pl.BlockSpec((rows * CHUNK, LANES), lambda i: (0, jnp.maximum(i - N_TILE, 0))),
        out_shape=jax.ShapeDtypeStruct((rows * CHUNK, SSM_WIDTH), F32),
        scratch_shapes=[pltpu.VMEM((GPT, rows * N_TILE, 2 * SSM_N), F32),
                        pltpu.VMEM((GPT, N_TILE, 2 * SSM_N), F32),
                        pltpu.VMEM((CW, SW), BF16),
                        pltpu.VMEM((CW, CW), BF16),
                        pltpu.VMEM((CW, SW), BF16)],
        compiler_params=_cparams("arbitrary"),
        name="ssm",
    )(u, u_meta, w_compact, a1, a2, t_lags, ce_lags)


def _merge_kernel(attn_ref, y_ref, wap_ref, wv_ref, wg_ref, g0_ref, g1_ref, o_ref):
    attn = attn_ref[...]
    y = y_ref[...].astype(BF16)
    for c in range(o_ref.shape[1] // MXU_N):
        cols = slice(c * MXU_N, (c + 1) * MXU_N)
        a = jnp.dot(attn, wap_ref[:, cols], preferred_element_type=F32)
        ssm = (jnp.dot(y, wv_ref[:, cols], preferred_element_type=F32)
               * _sigmoid(jnp.dot(y, wg_ref[:, cols], preferred_element_type=F32)))
        mixed = (_sigmoid(g0_ref[:, cols].astype(F32)) * a
                 + _sigmoid(g1_ref[:, cols].astype(F32)) * ssm)
        o_ref[:, cols] = mixed.astype(o_ref.dtype)


def _merge(attn, y, wap, wv, wg, gates, tm, tn):
    m = attn.shape[0]
    g1_off = D_MODEL // tn
    return pl.pallas_call(
        _merge_kernel,
        grid=(m // tm, D_MODEL // tn),
        in_specs=[
            pl.BlockSpec((tm, attn.shape[1]), lambda i, j: (i, 0)),
            pl.BlockSpec((tm, SSM_WIDTH), lambda i, j: (i, 0)),
            pl.BlockSpec((attn.shape[1], tn), lambda i, j: (0, j)),
            pl.BlockSpec((SSM_WIDTH, tn), lambda i, j: (0, j)),
            pl.BlockSpec((SSM_WIDTH, tn), lambda i, j: (0, j)),
            pl.BlockSpec((tm, tn), lambda i, j: (i, j)),
            pl.BlockSpec((tm, tn), lambda i, j: (i, g1_off + j)),
        ],
        out_specs=pl.BlockSpec((tm, tn), lambda i, j: (i, j)),
        out_shape=jax.ShapeDtypeStruct((m, D_MODEL), BF16),
        compiler_params=_cparams("parallel", "arbitrary"),
        name="gated_merge",
    )(attn, y, wap, wv, wg, gates, gates)


def _out_proj_kernel(x_ref, mx_ref, w_ref, g_ref, h_ref, n_ref):
    mx = mx_ref[...]
    d = h_ref.shape[1]
    sq = None
    for c in range(d // MXU_N):
        cols = slice(c * MXU_N, (c + 1) * MXU_N)
        h = x_ref[:, cols] + jnp.dot(mx, w_ref[:, cols], preferred_element_type=F32)
        h_ref[:, cols] = h
        part = jnp.sum(h * h, axis=-1, keepdims=True)
        sq = part if sq is None else sq + part
    inv = lax.rsqrt(sq * (1.0 / d) + EPS)
    n_ref[...] = (h_ref[...] * inv * g_ref[...]).astype(n_ref.dtype)


def _out_proj(x, mixed, w, g, tm):
    m, d = x.shape
    row = pl.BlockSpec((tm, d), lambda i: (i, 0))
    return pl.pallas_call(
        _out_proj_kernel,
        grid=(m // tm,),
        in_specs=[row, row, pl.BlockSpec((d, d), lambda i: (0, 0)), pl.BlockSpec((1, d), lambda i: (0, 0))],
        out_specs=[row, row],
        out_shape=[jax.ShapeDtypeStruct((m, d), F32), jax.ShapeDtypeStruct((m, d), BF16)],
        compiler_params=_cparams("parallel"),
        name="out_proj",
    )(x, mixed, w, g)


def _ffn_kernel(n_ref, h_ref, wg_ref, wu_ref, wd_ref, gf_ref, o_ref, acc_ref):
    f = pl.program_id(1)
    n = n_ref[...]
    tf = wg_ref.shape[1]
    d = wd_ref.shape[1]
    @pl.when(f == 0)
    def _():
        acc_ref[...] = jnp.zeros(acc_ref.shape, acc_ref.dtype)

    acts = []
    for c in range(tf // MXU_N):
        cols = slice(c * MXU_N, (c + 1) * MXU_N)
        gate = jnp.dot(n, wg_ref[:, cols], preferred_element_type=F32)
        up = jnp.dot(n, wu_ref[:, cols], preferred_element_type=F32)
        acts.append((gate * _sigmoid(gate) * up).astype(BF16))

    for c in range(d // FFN_ACC_COLS):
        cols = slice(c * FFN_ACC_COLS, (c + 1) * FFN_ACC_COLS)
        part = None
        for k, a in enumerate(acts):
            term = jnp.dot(a, wd_ref[k * MXU_N:(k + 1) * MXU_N, cols], preferred_element_type=F32)
            part = term if part is None else part + term
        acc_ref[:, cols] += part

    @pl.when(f == pl.num_programs(1) - 1)
    def _():
        o_ref[...] = _rms(h_ref[...] + acc_ref[...], gf_ref[...])


def _ffn(n, h, wg, wu, wd, gf, tm, tf):
    m, d = h.shape
    row = lambda i, f: (i, 0)
    return pl.pallas_call(
        _ffn_kernel,
        grid=(m // tm, D_FF // tf),
        in_specs=[
            pl.BlockSpec((tm, d), row),
            pl.BlockSpec((tm, d), row),
            pl.BlockSpec((d, tf), lambda i, f: (0, f)),
            pl.BlockSpec((d, tf), lambda i, f: (0, f)),
            pl.BlockSpec((tf, d), lambda i, f: (f, 0)),
            pl.BlockSpec((1, d), lambda i, f: (0, 0)),
        ],
        out_specs=pl.BlockSpec((tm, d), row),
        out_shape=jax.ShapeDtypeStruct((m, d), F32),
        scratch_shapes=[pltpu.VMEM((tm, d), F32)],
        compiler_params=_cparams("parallel", "arbitrary"),
        name="ffn",
    )(n, h, wg, wu, wd, gf)


def _rope_tables(length):
    half = QK_ROPE // 2
    freqs = np.float32(ROPE_THETA) ** (-np.arange(half, dtype=np.float32) / np.float32(half))
    ang = (np.arange(length, dtype=np.float32)[:, None] * freqs[None, :]).astype(np.float32)
    c, s = np.cos(ang).astype(np.float32), np.sin(ang).astype(np.float32)
    z = np.zeros_like(c)
    cosf = np.concatenate([c, c, z, z], axis=1)
    sin1 = np.concatenate([-s, z, z, z], axis=1)
    sin2 = np.concatenate([z, s, z, z], axis=1)
    return cosf, sin1, sin2


def kernel(x, meta_tokens, norm_mix, w_in, norm_q, w_q_up, norm_kv, w_kv_up, w_attn_proj, ssm_lambda_re, ssm_lambda_im, ssm_log_step, ssm_b_re, ssm_b_im, ssm_c_re, ssm_c_im, ssm_d, w_glu_val, w_glu_gate, w_out, norm_ffn, w_ffn_gate, w_ffn_up, w_ffn_down, norm_final):
    bsz, seq, d = x.shape
    assert d == D_MODEL and w_in.shape[0] == 1 and seq % 512 == 0 and N_META == CHUNK
    t_rows = bsz * seq
    n_chunks = seq // CHUNK
    bj = bsz * n_chunks
    x2 = x.reshape(t_rows, d)
    meta = jnp.pad(meta_tokens.astype(x.dtype), ((0, META_ROWS - N_META), (0, 0)))

    w_all = _w_in_prep(jnp.swapaxes(w_in[0], 0, 1), 256)
    scale = (QK_NOPE + QK_ROPE) ** -0.5 * math.log2(math.e)
    wq = jnp.pad((w_q_up[0] * scale).reshape(Q_LORA, HEADS, QK_NOPE + QK_ROPE),
                 ((0, 0), (0, 0), (0, HEAD_PAD - QK_NOPE - QK_ROPE))).reshape(Q_LORA, HEADS * HEAD_PAD).astype(BF16)
    wkv = w_kv_up[0].astype(BF16)
    g_mix = norm_mix[0].reshape(1, d)
    g_q = norm_q[0].reshape(1, Q_LORA)
    g_kv = norm_kv[0].reshape(1, KV_LORA)
    cosf, sin1, sin2 = _rope_tables(N_META + seq)

    lat = _norm_matmul(x2, g_mix, w_all, F32, 512, LAT_W, LAT_W, 0)
    u_seq, gates = _ug_proj(x2, g_mix, w_all, 1024, SSM_WIDTH, UG_ROW0 // SSM_WIDTH)
    lat_m = _norm_matmul(meta, g_mix, w_all, F32, META_ROWS, LAT_W, LAT_W, 0)
    u_m = _norm_matmul(meta, g_mix, w_all, BF16, META_ROWS, SSM_WIDTH, SSM_WIDTH, UG_ROW0 // SSM_WIDTH)

    q, k, v = _up_rope(lat, g_q, g_kv, wq, wkv, cosf[N_META:], sin1[N_META:], sin2[N_META:], 512)
    _, k_m, v_m = _up_rope(lat_m, g_q, g_kv, wq, wkv, cosf[:META_ROWS], sin1[:META_ROWS],
                            sin2[:META_ROWS], META_ROWS)
    attn = _attention(q, k, v, k_m, v_m, bsz, seq, 512)

    w_sin, ce_lags, t_lags, a1, a2 = _ssm_params(
        ssm_lambda_re[0], ssm_lambda_im[0], ssm_log_step[0], ssm_b_re[0], ssm_b_im[0],
        ssm_c_re[0], ssm_c_im[0], ssm_d[0], 8)
    u_meta = jnp.pad(u_m[:N_META].reshape(1, CHUNK, N_TILE, LANES).transpose(2, 0, 1, 3).reshape(N_TILE, 1, CW),
                     ((0, 0), (0, 15), (0, 0)))
    y = _ssm(u_seq, u_meta, w_sin, a1.reshape(N_TILE, SW), a2.reshape(N_TILE, SW), t_lags, ce_lags,
             bsz, n_chunks)

    mixed = _merge(attn, y, w_attn_proj[0].astype(BF16), w_glu_val[0].astype(BF16),
                   w_glu_gate[0].astype(BF16), gates, 1024, 512)
    h1, n_ffn = _out_proj(x2, mixed, w_out[0].astype(BF16), norm_ffn[0].reshape(1, d), 512)
    out = _ffn(n_ffn, h1, w_ffn_gate[0].astype(BF16), w_ffn_up[0].astype(BF16),
               w_ffn_down[0].astype(BF16), norm_final.reshape(1, d), 512, 512)
    return out.reshape(bsz, seq, d)
```

```python
import functools
import math

import numpy as np
import jax
import jax.numpy as jnp
from jax import lax
from jax.experimental import pallas as pl
from jax.experimental.pallas import tpu as pltpu

F32 = jnp.float32
BF16 = jnp.bfloat16

D_MODEL = 2048
N_META = 16
EPS = 1e-6
HEADS = 16
Q_LORA = 512
KV_LORA = 512
QK_NOPE = 128
QK_ROPE = 64
V_HEAD = 128
ROPE_THETA = 10000.0
SSM_WIDTH = 1024
SSM_P = 16
SSM_G = 64
SSM_N = 64
D_FF = 5632
LANES = 128
MXU_N = 256
FFN_ACC_COLS = 512
SCAN_UNROLL = 8
HEAD_PAD = 2 * LANES
LAT_W = Q_LORA + KV_LORA + LANES
CHUNK = 16
CP = CHUNK * SSM_P
GPT = LANES // SSM_P
N_TILE = SSM_G // GPT
CW = CHUNK * LANES
SW = GPT * 2 * SSM_N
UG_ROW0 = 2048
NT_DIMS = (((1,), (1,)), ((), ()))
META_ROWS = 128
V_PAD = 2 * LANES
VMEM_LIMIT = 56 * 1024 * 1024


def _cparams(*sem):
    return pltpu.CompilerParams(dimension_semantics=sem, vmem_limit_bytes=VMEM_LIMIT)


def _sigmoid(x):
    return 0.5 + 0.5 * jnp.tanh(0.5 * x)


def _gelu_tanh(x):
    return x * (0.5 * (1.0 + jnp.tanh(math.sqrt(2.0 / math.pi) * (x + 0.044715 * (x * x * x)))))


def _rms(x, g):
    ms = jnp.mean(x * x, axis=-1, keepdims=True)
    return x * lax.rsqrt(ms + EPS) * g


def _w_in_prep_kernel(w_ref, o_ref):
    off_u = Q_LORA + KV_LORA + QK_ROPE
    o_ref[:off_u] = w_ref[:off_u].astype(BF16)
    o_ref[off_u:UG_ROW0] = jnp.zeros((UG_ROW0 - off_u, o_ref.shape[1]), BF16)
    o_ref[UG_ROW0:] = w_ref[off_u:].astype(BF16)


def _w_in_prep(wt, cb):
    n, k = wt.shape
    n_out = UG_ROW0 + n - (Q_LORA + KV_LORA + QK_ROPE)
    return pl.pallas_call(
        _w_in_prep_kernel,
        grid=(k // cb,),
        in_specs=[pl.BlockSpec((n, cb), lambda i: (0, i))],
        out_specs=pl.BlockSpec((n_out, cb), lambda i: (0, i)),
        out_shape=jax.ShapeDtypeStruct((n_out, k), BF16),
        compiler_params=_cparams("parallel"),
        name="w_in_prep",
    )(wt)


def _norm_matmul_kernel(x_ref, g_ref, w_ref, o_ref, *rest):
    n_ref = rest[-1]
    @pl.when(pl.program_id(1) == 0)
    def _():
        n_ref[...] = _rms(x_ref[...], g_ref[...]).astype(BF16)
        if len(rest) == 2:
            rest[0][...] = n_ref[...]

    n = n_ref[...]
    tn = o_ref.shape[1]
    for c0 in range(0, tn, 2 * MXU_N):
        cols = slice(c0, min(c0 + 2 * MXU_N, tn))
        o_ref[:, cols] = lax.dot_general(n, w_ref[cols, :], NT_DIMS,
                                         preferred_element_type=F32).astype(o_ref.dtype)


def _norm_matmul(x, g, wt, out_dtype, tm, tn, n, row_block0, emit_norm=False):
    m, k = x.shape
    out_specs = [pl.BlockSpec((tm, tn), lambda i, j: (i, j))]
    out_shape = [jax.ShapeDtypeStruct((m, n), out_dtype)]
    if emit_norm:
        out_specs.append(pl.BlockSpec((tm, k), lambda i, j: (i, 0)))
        out_shape.append(jax.ShapeDtypeStruct((m, k), BF16))
    return pl.pallas_call(
        _norm_matmul_kernel,
        grid=(m // tm, n // tn),
        in_specs=[
            pl.BlockSpec((tm, k), lambda i, j: (i, 0)),
            pl.BlockSpec((1, k), lambda i, j: (0, 0)),
            pl.BlockSpec((tn, k), lambda i, j: (row_block0 + j, 0)),
        ],
        out_specs=out_specs if emit_norm else out_specs[0],
        out_shape=out_shape if emit_norm else out_shape[0],
        scratch_shapes=[pltpu.VMEM((tm, k), BF16)],
        compiler_params=_cparams("parallel", "arbitrary"),
        name="norm_matmul",
    )(x, g, wt)


def _ug_proj_kernel(n_ref, w_ref, u_ref, gate_ref, stage_ref):
    j = pl.program_id(1)
    tm = n_ref.shape[0]

    @pl.when(j == 0)
    def _():
        n = n_ref[...]
        for c in range(SSM_WIDTH // MXU_N):
            acc = lax.dot_general(n, w_ref[c * MXU_N:(c + 1) * MXU_N, :], NT_DIMS,
                                  preferred_element_type=F32)
            for s in range(MXU_N // LANES):
                stage_ref[c * (MXU_N // LANES) + s] = acc[:, s * LANES:(s + 1) * LANES]
        for t in range(N_TILE):
            for sig in range(CHUNK):
                u_ref[t, :, sig * LANES:(sig + 1) * LANES] = (
                    stage_ref[t, pl.ds(sig, tm // CHUNK, stride=CHUNK), :].astype(u_ref.dtype))

    @pl.when(j > 0)
    def _():
        n = n_ref[...]
        tn = gate_ref.shape[1]
        for c0 in range(0, tn, 2 * MXU_N):
            cols = slice(c0, c0 + 2 * MXU_N)
            gate_ref[:, cols] = lax.dot_general(n, w_ref[cols, :], NT_DIMS,
                                                preferred_element_type=F32).astype(gate_ref.dtype)


def _ug_proj(n, wt, tm, tn, row_block0):
    m, k = n.shape
    n_gate = 2 * D_MODEL
    assert tn == SSM_WIDTH
    return pl.pallas_call(
        _ug_proj_kernel,
        grid=(m // tm, 1 + n_gate // tn),
        in_specs=[
            pl.BlockSpec((tm, k), lambda i, j: (i, 0)),
            pl.BlockSpec((tn, k), lambda i, j: (row_block0 + j, 0)),
        ],
        out_specs=[
            pl.BlockSpec((N_TILE, tm // CHUNK, CW), lambda i, j: (0, i, 0)),
            pl.BlockSpec((tm, tn), lambda i, j: (i, jnp.maximum(j - 1, 0))),
        ],
        out_shape=[jax.ShapeDtypeStruct((N_TILE, m // CHUNK, CW), BF16),
                   jax.ShapeDtypeStruct((m, n_gate), BF16)],
        scratch_shapes=[pltpu.VMEM((N_TILE, tm, LANES), F32)],
        compiler_params=_cparams("parallel", "arbitrary"),
        name="ug_proj",
    )(n, wt)


def _rope_slab(x, cosf, sin1, sin2):
    return (x * cosf + pltpu.roll(x, LANES - QK_ROPE // 2, 1) * sin1
            + pltpu.roll(x, QK_ROPE // 2, 1) * sin2)


def _up_rope_kernel(ql_ref, kvl_ref, kr_ref, gq_ref, gkv_ref, wq_ref, wkv_ref,
                    cos_ref, sin1_ref, sin2_ref, q_ref, k_ref, v_ref):
    cosf, sin1, sin2 = cos_ref[...], sin1_ref[...], sin2_ref[...]
    qn = _rms(ql_ref[...], gq_ref[...]).astype(BF16)
    kvn = _rms(kvl_ref[...], gkv_ref[...]).astype(BF16)
    k_rope = _rope_slab(kr_ref[...], cosf, sin1, sin2).astype(BF16)
    ones_cols = (lax.broadcasted_iota(jnp.int32, (ql_ref.shape[0], V_PAD - V_HEAD), 1) == 0
                 ).astype(F32).astype(BF16)
    for h in range(HEADS):
        acc = jnp.dot(qn, wq_ref[:, h * HEAD_PAD:(h + 1) * HEAD_PAD], preferred_element_type=F32)
        q_ref[h, :, :LANES] = acc[:, :LANES].astype(BF16)
        q_ref[h, :, LANES:] = _rope_slab(acc[:, LANES:], cosf, sin1, sin2).astype(BF16)
    kv_w = QK_NOPE + V_HEAD
    for h in range(HEADS):
        acc = jnp.dot(kvn, wkv_ref[:, h * kv_w:(h + 1) * kv_w], preferred_element_type=F32)
        k_ref[h, :, :LANES] = acc[:, :QK_NOPE].astype(BF16)
        k_ref[h, :, LANES:] = k_rope
        v_ref[h, :, :V_HEAD] = acc[:, QK_NOPE:].astype(BF16)
        v_ref[h, :, V_HEAD:] = ones_cols


def _up_rope(lat, gq, gkv, wq, wkv, cosf, sin1, sin2, tm):
    m = lat.shape[0]
    n_pos_blocks = cosf.shape[0] // tm
    row = lambda i: (i, 0)
    const = lambda i: (0, 0)
    pos = lambda i: (i % n_pos_blocks, 0)
    head_major = lambda i: (0, i, 0)
    return pl.pallas_call(
        _up_rope_kernel,
        grid=(m // tm,),
        in_specs=[
            pl.BlockSpec((tm, Q_LORA), row),
            pl.BlockSpec((tm, KV_LORA), lambda i: (i, 1)),
            pl.BlockSpec((tm, LANES), lambda i: (i, (Q_LORA + KV_LORA) // LANES)),
            pl.BlockSpec((1, Q_LORA), const),
            pl.BlockSpec((1, KV_LORA), const),
            pl.BlockSpec(wq.shape, const),
            pl.BlockSpec(wkv.shape, const),
            pl.BlockSpec((tm, LANES), pos),
            pl.BlockSpec((tm, LANES), pos),
            pl.BlockSpec((tm, LANES), pos),
        ],
        out_specs=[
            pl.BlockSpec((HEADS, tm, HEAD_PAD), head_major),
            pl.BlockSpec((HEADS, tm, HEAD_PAD), head_major),
            pl.BlockSpec((HEADS, tm, V_PAD), head_major),
        ],
        out_shape=[
            jax.ShapeDtypeStruct((HEADS, m, HEAD_PAD), BF16),
            jax.ShapeDtypeStruct((HEADS, m, HEAD_PAD), BF16),
            jax.ShapeDtypeStruct((HEADS, m, V_PAD), BF16),
        ],
        compiler_params=_cparams("parallel"),
        name="up_rope",
    )(lat, lat, lat, gq, gkv, wq, wkv, cosf, sin1, sin2)


def _attn_kernel(qi_ref, ki_ref, q_ref, k_ref, v_ref, km_ref, vm_ref, o_ref, m_ref, acc_ref):
    t = pl.program_id(1)
    qi = qi_ref[t]
    ki = ki_ref[t]

    def lane_tiled(x, width):
        return jnp.concatenate([x] * (width // LANES), axis=1)

    @pl.when(ki == 0)
    def _init():
        def body(h, c):
            s = lax.dot_general(q_ref[h], km_ref[h], NT_DIMS, preferred_element_type=F32)
            key = lax.broadcasted_iota(jnp.int32, s.shape, 1)
            s = jnp.where(key < N_META, s, -jnp.inf)
            m = jnp.broadcast_to(jnp.max(s, axis=-1, keepdims=True), s.shape)
            p = jnp.exp2(s - m)
            m_ref[h] = m
            acc_ref[h] = jnp.dot(p.astype(BF16), vm_ref[h], preferred_element_type=F32)
            return c
        lax.fori_loop(0, HEADS, body, 0, unroll=2)

    def tile(masked):
        def scores(h):
            return lax.dot_general(q_ref[h], k_ref[h], NT_DIMS, preferred_element_type=F32)

        def body(h, s):
            if masked:
                qry = lax.broadcasted_iota(jnp.int32, s.shape, 0)
                key = lax.broadcasted_iota(jnp.int32, s.shape, 1)
                s = jnp.where(key <= qry, s, -jnp.inf)
            m_prev = m_ref[h]
            m_new = jnp.maximum(m_prev, jnp.max(s, axis=-1, keepdims=True))
            alpha = jnp.exp2(m_prev - m_new)
            p = jnp.exp2(s - lane_tiled(m_new, s.shape[1]))
            acc_ref[h] = lane_tiled(alpha, V_PAD) * acc_ref[h] + jnp.dot(
                p.astype(BF16), v_ref[h], preferred_element_type=F32)
            m_ref[h] = m_new

        s_next = scores(0)
        for h in range(HEADS):
            s_cur = s_next
            if h + 1 < HEADS:
                s_next = scores(h + 1)
            body(h, s_cur)

    @pl.when(ki < qi)
    def _full():
        tile(False)

    @pl.when(ki == qi)
    def _diag():
        tile(True)
        for h in range(HEADS):
            out = acc_ref[h, :, :V_HEAD] / acc_ref[h, :, V_HEAD:V_HEAD + 1]
            o_ref[:, h * V_HEAD:(h + 1) * V_HEAD] = out.astype(o_ref.dtype)


def _attention(q, k, v, km, vm, bsz, seq, tq):
    nq = seq // tq
    pairs = [(a, b) for a in range(nq) for b in range(a + 1)]
    qi_arr = jnp.asarray([p[0] for p in pairs], jnp.int32)
    ki_arr = jnp.asarray([p[1] for p in pairs], jnp.int32)
    grid_spec = pltpu.PrefetchScalarGridSpec(
        num_scalar_prefetch=2,
        grid=(bsz, len(pairs)),
        in_specs=[
            pl.BlockSpec((HEADS, tq, HEAD_PAD), lambda b, t, qi, ki: (0, b * nq + qi[t], 0)),
            pl.BlockSpec((HEADS, tq, HEAD_PAD), lambda b, t, qi, ki: (0, b * nq + ki[t], 0)),
            pl.BlockSpec((HEADS, tq, V_PAD), lambda b, t, qi, ki: (0, b * nq + ki[t], 0)),
            pl.BlockSpec(km.shape, lambda b, t, qi, ki: (0, 0, 0)),
            pl.BlockSpec(vm.shape, lambda b, t, qi, ki: (0, 0, 0)),
        ],
        out_specs=pl.BlockSpec((tq, HEADS * V_HEAD), lambda b, t, qi, ki: (b * nq + qi[t], 0)),
        scratch_shapes=[
            pltpu.VMEM((HEADS, tq, LANES), F32),
            pltpu.VMEM((HEADS, tq, V_PAD), F32),
        ],
    )
    return pl.pallas_call(
        _attn_kernel,
        grid_spec=grid_spec,
        out_shape=jax.ShapeDtypeStruct((bsz * seq, HEADS * V_HEAD), BF16),
        compiler_params=_cparams("parallel", "arbitrary"),
        name="flash_attn",
    )(qi_arr, ki_arr, q, k, v, km, vm)


def _ssm_param_kernel(lr_ref, li_ref, ls_ref, bre_ref, bim_ref, cre_ref, cim_ref, d_ref,
                      win_ref, cen_ref, tl_ref, a1_ref, a2_ref):
    lane = lax.broadcasted_iota(jnp.int32, (SSM_P, LANES), 1)
    row = lax.broadcasted_iota(jnp.int32, (SSM_P, LANES), 0)
    on_diag = jnp.bitwise_and(lane, SSM_P - 1) == row
    cat = lambda x, y: jnp.concatenate([x, y], axis=1)
    for g in range(lr_ref.shape[0]):
        lr, li = lr_ref[g], li_ref[g]
        step = jnp.exp(ls_ref[g])
        kf = lax.broadcasted_iota(jnp.int32, (CHUNK + 1, SSM_N), 0).astype(F32)
        mag = jnp.exp(kf * (lr * step))
        ang = kf * (li * step)
        e_re, e_im = mag * jnp.cos(ang), mag * jnp.sin(ang)
        a1_ref[g] = cat(e_re[CHUNK:], e_re[CHUNK:])
        a2_ref[g] = cat(-e_im[CHUNK:], e_im[CHUNK:])
        nr, ni = e_re[1:2] - 1.0, e_im[1:2]
        den = lr * lr + li * li
        coef_re = (nr * lr + ni * li) / den
        coef_im = (ni * lr - nr * li) / den
        bre, bim = bre_ref[g], bim_ref[g]
        bb_re = coef_re * bre - coef_im * bim
        bb_im = coef_re * bim + coef_im * bre
        cre, cim = cre_ref[g], cim_ref[g]
        ce_re_lags, ce_im_lags = [], []
        for k in range(CHUNK + 1):
            er, ei = e_re[k:k + 1], e_im[k:k + 1]
            ce_re = cre * er - cim * ei
            ce_im = cre * ei + cim * er
            if k >= 1:
                cen_ref[g, k - 1] = cat(ce_re, -ce_im).astype(cen_ref.dtype)
            if k < CHUNK:
                ce_re_lags.append(ce_re)
                ce_im_lags.append(ce_im)
                r0 = (CHUNK - 1 - k) * SSM_P
                win_ref[g, r0:r0 + SSM_P, :] = cat(er * bb_re - ei * bb_im,
                                                   er * bb_im + ei * bb_re).astype(win_ref.dtype)
        tl = (lax.dot_general(jnp.concatenate(ce_re_lags, axis=0), jnp.concatenate([bb_re] * GPT, axis=0),
                              NT_DIMS, precision=lax.Precision.HIGHEST, preferred_element_type=F32)
              - lax.dot_general(jnp.concatenate(ce_im_lags, axis=0), jnp.concatenate([bb_im] * GPT, axis=0),
                                NT_DIMS, precision=lax.Precision.HIGHEST, preferred_element_type=F32))
        tl_ref[g] = tl
        tl_ref[g, :SSM_P, :] = tl[:SSM_P] + jnp.where(on_diag, d_ref[g], 0.0)


def _ssm_params(lam_re, lam_im, log_step, b_re, b_im, c_re, c_im, d, gb):
    g = SSM_G
    row = lambda shape: pl.BlockSpec((gb,) + shape, lambda i: (i,) + (0,) * len(shape))
    outs = [
        ((g, CP, 2 * SSM_N), (CP, 2 * SSM_N), BF16),
        ((g, CHUNK, SSM_P, 2 * SSM_N), (CHUNK, SSM_P, 2 * SSM_N), BF16),
        ((g, CP, LANES), (CP, LANES), F32),
        ((g, 1, 2 * SSM_N), (1, 2 * SSM_N), F32),
        ((g, 1, 2 * SSM_N), (1, 2 * SSM_N), F32),
    ]
    return pl.pallas_call(
        _ssm_param_kernel,
        grid=(g // gb,),
        in_specs=[row((1, SSM_N)), row((1, SSM_N)), row((1, 1)),
                  row((SSM_P, SSM_N)), row((SSM_P, SSM_N)), row((SSM_P, SSM_N)), row((SSM_P, SSM_N)),
                  row((SSM_P, 1))],
        out_specs=[row(blk) for _, blk, _ in outs],
        out_shape=[jax.ShapeDtypeStruct(full, dt) for full, _, dt in outs],
        compiler_params=_cparams("parallel"),
        name="ssm_params",
    )(lam_re.reshape(g, 1, SSM_N), lam_im.reshape(g, 1, SSM_N), log_step.reshape(g, 1, 1),
      jnp.swapaxes(b_re, 1, 2), jnp.swapaxes(b_im, 1, 2), c_re, c_im, d.reshape(g, SSM_P, 1))


def _ssm_kernel(u_ref, um_ref, wc_ref, a1_ref, a2_ref, tl_ref, cen_ref, y_ref,
                s_ref, s0_ref, win_ref, wt_ref, ws_ref, *, bsz, n_chunks):
    i = pl.program_id(0)
    rows = bsz * n_chunks

    @pl.when(i < N_TILE)
    def _state_inputs():
        @pl.when(i == 0)
        def _():
            win_ref[...] = jnp.zeros(win_ref.shape, win_ref.dtype)

        for sig in range(CHUNK):
            for g in range(GPT):
                r0 = sig * LANES + g * SSM_P
                win_ref[r0:r0 + SSM_P, g * 2 * SSM_N:(g + 1) * 2 * SSM_N] = (
                    wc_ref[g, sig * SSM_P:(sig + 1) * SSM_P, :])
        v = jnp.dot(u_ref[0], win_ref[...], preferred_element_type=F32)
        v_meta = jnp.dot(um_ref[0], win_ref[...], preferred_element_type=F32)
        for g in range(GPT):
            s_ref[g, pl.ds(i, rows, stride=N_TILE), :] = v[:, g * 2 * SSM_N:(g + 1) * 2 * SSM_N]
            s0_ref[g, pl.ds(i, 1), :] = v_meta[0:1, g * 2 * SSM_N:(g + 1) * 2 * SSM_N]

    @pl.when(i == N_TILE - 1)
    def _recurrence():
        a1 = [a1_ref[:, g * 2 * SSM_N:(g + 1) * 2 * SSM_N] for g in range(GPT)]
        a2 = [a2_ref[:, g * 2 * SSM_N:(g + 1) * 2 * SSM_N] for g in range(GPT)]
        for b in range(bsz):
            def body(j, carry):
                r = pl.multiple_of((b * n_chunks + j) * N_TILE, N_TILE)
                nxt = []
                for g in range(GPT):
                    s, sw = carry[2 * g], carry[2 * g + 1]
                    v = s_ref[g, pl.ds(r, N_TILE), :]
                    vw = pltpu.roll(v, SSM_N, 1)
                    s_ref[g, pl.ds(r, N_TILE), :] = s
                    nxt.append(a1[g] * s + a2[g] * sw + v)
                    nxt.append(a1[g] * sw - a2[g] * s + vw)
                return tuple(nxt)
            init = []
            for g in range(GPT):
                init += [s0_ref[g], pltpu.roll(s0_ref[g], SSM_N, 1)]
            lax.fori_loop(0, n_chunks, body, tuple(init), unroll=SCAN_UNROLL)

    @pl.when(i >= N_TILE)
    def _outputs():
        tile = i - N_TILE
        row_g = lax.shift_right_logical(lax.broadcasted_iota(jnp.int32, (LANES, LANES), 0), 4)
        col_g = lax.shift_right_logical(lax.broadcasted_iota(jnp.int32, (LANES, LANES), 1), 4)
        same_group = row_g == col_g
        @pl.when(i == N_TILE)
        def _():
            wt_ref[...] = jnp.zeros(wt_ref.shape, wt_ref.dtype)
            ws_ref[...] = jnp.zeros(ws_ref.shape, ws_ref.dtype)

        for k in range(CHUNK):
            rep = jnp.concatenate([tl_ref[g, k * SSM_P:(k + 1) * SSM_P, :] for g in range(GPT)], axis=0)
            blk = jnp.where(same_group, rep, 0.0).astype(BF16)
            for sig in range(CHUNK - k):
                tau = sig + k
                wt_ref[tau * LANES:(tau + 1) * LANES, sig * LANES:(sig + 1) * LANES] = blk
        for g in range(GPT):
            for tau in range(CHUNK):
                r0 = tau * LANES + g * SSM_P
                ws_ref[r0:r0 + SSM_P, g * 2 * SSM_N:(g + 1) * 2 * SSM_N] = cen_ref[g, tau]
        states = jnp.concatenate(
            [s_ref[g, pl.ds(tile, rows, stride=N_TILE), :] for g in range(GPT)], axis=1).astype(BF16)
        u = u_ref[0]
        steps_per_tile = MXU_N // LANES
        for c in range(CW // MXU_N):
            k_hi = (c + 1) * MXU_N
            out_rows = slice(c * MXU_N, k_hi)
            y = _gelu_tanh(
                lax.dot_general(u[:, :k_hi], wt_ref[out_rows, :k_hi], NT_DIMS, preferred_element_type=F32)
                + lax.dot_general(states, ws_ref[out_rows, :], NT_DIMS, preferred_element_type=F32))
            for s in range(steps_per_tile):
                tau = c * steps_per_tile + s
                y_ref[pl.ds(tau, rows, stride=CHUNK), :] = y[:, s * LANES:(s + 1) * LANES]


def _ssm(u, u_meta, w_compact, a1, a2, t_lags, ce_lags, bsz, n_chunks):
    rows = bsz * n_chunks
    in_tile = lambda i: (lax.rem(i, N_TILE), 0, 0)
    out_tile = lambda i: (jnp.maximum(i - N_TILE, 0), 0, 0)
    out_tile4 = lambda i: (jnp.maximum(i - N_TILE, 0), 0, 0, 0)
    whole = lambda i: (0, 0)
    return pl.pallas_call(
        functools.partial(_ssm_kernel, bsz=bsz, n_chunks=n_chunks),
        grid=(2 * N_TILE,),
        in_specs=[pl.BlockSpec((1, rows, CW), in_tile),
                  pl.BlockSpec((1,) + u_meta.shape[1:], in_tile),
                  pl.BlockSpec((GPT, CP, 2 * SSM_N), in_tile),
                  pl.BlockSpec((N_TILE, SW), whole),
                  pl.BlockSpec((N_TILE, SW), whole),
                  pl.BlockSpec((GPT, CP, LANES), out_tile),
                  pl.BlockSpec((GPT, CHUNK, SSM_P, 2 * SSM_N), out_tile4)],
        out_specs=pl.BlockSpec((rows * CHUNK, LANES), lambda i: (0, jnp.maximum(i - N_TILE, 0))),
        out_shape=jax.ShapeDtypeStruct((rows * CHUNK, SSM_WIDTH), F32),
        scratch_shapes=[pltpu.VMEM((GPT, rows * N_TILE, 2 * SSM_N), F32),
                        pltpu.VMEM((GPT, N_TILE, 2 * SSM_N), F32),
                        pltpu.VMEM((CW, SW), BF16),
                        pltpu.VMEM((CW, CW), BF16),
                        pltpu.VMEM((CW, SW), BF16)],
        compiler_params=_cparams("arbitrary"),
        name="ssm",
    )(u, u_meta, w_compact, a1, a2, t_lags, ce_lags)


def _merge_kernel(attn_ref, y_ref, wap_ref, wv_ref, wg_ref, g0_ref, g1_ref, o_ref):
    attn = attn_ref[...]
    y = y_ref[...].astype(BF16)
    for c in range(o_ref.shape[1] // MXU_N):
        cols = slice(c * MXU_N, (c + 1) * MXU_N)
        a = jnp.dot(attn, wap_ref[:, cols], preferred_element_type=F32)
        ssm = (jnp.dot(y, wv_ref[:, cols], preferred_element_type=F32)
               * _sigmoid(jnp.dot(y, wg_ref[:, cols], preferred_element_type=F32)))
        mixed = (_sigmoid(g0_ref[:, cols].astype(F32)) * a
                 + _sigmoid(g1_ref[:, cols].astype(F32)) * ssm)
        o_ref[:, cols] = mixed.astype(o_ref.dtype)


def _merge(attn, y, wap, wv, wg, gates, tm, tn):
    m = attn.shape[0]
    g1_off = D_MODEL // tn
    return pl.pallas_call(
        _merge_kernel,
        grid=(m // tm, D_MODEL // tn),
        in_specs=[
            pl.BlockSpec((tm, attn.shape[1]), lambda i, j: (i, 0)),
            pl.BlockSpec((tm, SSM_WIDTH), lambda i, j: (i, 0)),
            pl.BlockSpec((attn.shape[1], tn), lambda i, j: (0, j)),
            pl.BlockSpec((SSM_WIDTH, tn), lambda i, j: (0, j)),
            pl.BlockSpec((SSM_WIDTH, tn), lambda i, j: (0, j)),
            pl.BlockSpec((tm, tn), lambda i, j: (i, j)),
            pl.BlockSpec((tm, tn), lambda i, j: (i, g1_off + j)),
        ],
        out_specs=pl.BlockSpec((tm, tn), lambda i, j: (i, j)),
        out_shape=jax.ShapeDtypeStruct((m, D_MODEL), BF16),
        compiler_params=_cparams("parallel", "arbitrary"),
        name="gated_merge",
    )(attn, y, wap, wv, wg, gates, gates)


def _out_proj_kernel(x_ref, mx_ref, w_ref, g_ref, h_ref, n_ref):
    mx = mx_ref[...]
    d = h_ref.shape[1]
    sq = None
    for c in range(d // MXU_N):
        cols = slice(c * MXU_N, (c + 1) * MXU_N)
        h = x_ref[:, cols] + jnp.dot(mx, w_ref[:, cols], preferred_element_type=F32)
        h_ref[:, cols] = h
        part = jnp.sum(h * h, axis=-1, keepdims=True)
        sq = part if sq is None else sq + part
    inv = lax.rsqrt(sq * (1.0 / d) + EPS)
    n_ref[...] = (h_ref[...] * inv * g_ref[...]).astype(n_ref.dtype)


def _out_proj(x, mixed, w, g, tm):
    m, d = x.shape
    row = pl.BlockSpec((tm, d), lambda i: (i, 0))
    return pl.pallas_call(
        _out_proj_kernel,
        grid=(m // tm,),
        in_specs=[row, row, pl.BlockSpec((d, d), lambda i: (0, 0)), pl.BlockSpec((1, d), lambda i: (0, 0))],
        out_specs=[row, row],
        out_shape=[jax.ShapeDtypeStruct((m, d), F32), jax.ShapeDtypeStruct((m, d), BF16)],
        compiler_params=_cparams("parallel"),
        name="out_proj",
    )(x, mixed, w, g)


def _ffn_kernel(n_ref, h_ref, wg_ref, wu_ref, wd_ref, gf_ref, o_ref, acc_ref):
    f = pl.program_id(1)
    n = n_ref[...]
    tf = wg_ref.shape[1]
    d = wd_ref.shape[1]
    @pl.when(f == 0)
    def _():
        acc_ref[...] = jnp.zeros(acc_ref.shape, acc_ref.dtype)

    acts = []
    for c in range(tf // MXU_N):
        cols = slice(c * MXU_N, (c + 1) * MXU_N)
        gate = jnp.dot(n, wg_ref[:, cols], preferred_element_type=F32)
        up = jnp.dot(n, wu_ref[:, cols], preferred_element_type=F32)
        acts.append((gate * _sigmoid(gate) * up).astype(BF16))

    for c in range(d // FFN_ACC_COLS):
        cols = slice(c * FFN_ACC_COLS, (c + 1) * FFN_ACC_COLS)
        part = None
        for k, a in enumerate(acts):
            term = jnp.dot(a, wd_ref[k * MXU_N:(k + 1) * MXU_N, cols], preferred_element_type=F32)
            part = term if part is None else part + term
        acc_ref[:, cols] += part

    @pl.when(f == pl.num_programs(1) - 1)
    def _():
        o_ref[...] = _rms(h_ref[...] + acc_ref[...], gf_ref[...])


def _ffn(n, h, wg, wu, wd, gf, tm, tf):
    m, d = h.shape
    row = lambda i, f: (i, 0)
    return pl.pallas_call(
        _ffn_kernel,
        grid=(m // tm, D_FF // tf),
        in_specs=[
            pl.BlockSpec((tm, d), row),
            pl.BlockSpec((tm, d), row),
            pl.BlockSpec((d, tf), lambda i, f: (0, f)),
            pl.BlockSpec((d, tf), lambda i, f: (0, f)),
            pl.BlockSpec((tf, d), lambda i, f: (f, 0)),
            pl.BlockSpec((1, d), lambda i, f: (0, 0)),
        ],
        out_specs=pl.BlockSpec((tm, d), row),
        out_shape=jax.ShapeDtypeStruct((m, d), F32),
        scratch_shapes=[pltpu.VMEM((tm, d), F32)],
        compiler_params=_cparams("parallel", "arbitrary"),
        name="ffn",
    )(n, h, wg, wu, wd, gf)


def _rope_tables(length):
    half = QK_ROPE // 2
    freqs = np.float32(ROPE_THETA) ** (-np.arange(half, dtype=np.float32) / np.float32(half))
    ang = (np.arange(length, dtype=np.float32)[:, None] * freqs[None, :]).astype(np.float32)
    c, s = np.cos(ang).astype(np.float32), np.sin(ang).astype(np.float32)
    z = np.zeros_like(c)
    cosf = np.concatenate([c, c, z, z], axis=1)
    sin1 = np.concatenate([-s, z, z, z], axis=1)
    sin2 = np.concatenate([z, s, z, z], axis=1)
    return cosf, sin1, sin2


def kernel(x, meta_tokens, norm_mix, w_in, norm_q, w_q_up, norm_kv, w_kv_up, w_attn_proj, ssm_lambda_re, ssm_lambda_im, ssm_log_step, ssm_b_re, ssm_b_im, ssm_c_re, ssm_c_im, ssm_d, w_glu_val, w_glu_gate, w_out, norm_ffn, w_ffn_gate, w_ffn_up, w_ffn_down, norm_final):
    bsz, seq, d = x.shape
    assert d == D_MODEL and w_in.shape[0] == 1 and seq % 512 == 0 and N_META == CHUNK
    t_rows = bsz * seq
    n_chunks = seq // CHUNK
    bj = bsz * n_chunks
    x2 = x.reshape(t_rows, d)
    meta = jnp.pad(meta_tokens.astype(x.dtype), ((0, META_ROWS - N_META), (0, 0)))

    w_all = _w_in_prep(jnp.swapaxes(w_in[0], 0, 1), 256)
    scale = (QK_NOPE + QK_ROPE) ** -0.5 * math.log2(math.e)
    wq = jnp.pad((w_q_up[0] * scale).reshape(Q_LORA, HEADS, QK_NOPE + QK_ROPE),
                 ((0, 0), (0, 0), (0, HEAD_PAD - QK_NOPE - QK_ROPE))).reshape(Q_LORA, HEADS * HEAD_PAD).astype(BF16)
    wkv = w_kv_up[0].astype(BF16)
    g_mix = norm_mix[0].reshape(1, d)
    g_q = norm_q[0].reshape(1, Q_LORA)
    g_kv = norm_kv[0].reshape(1, KV_LORA)
    cosf, sin1, sin2 = _rope_tables(N_META + seq)

    lat, n_mix = _norm_matmul(x2, g_mix, w_all, F32, 512, LAT_W, LAT_W, 0, emit_norm=True)
    u_seq, gates = _ug_proj(n_mix, w_all, 1024, SSM_WIDTH, UG_ROW0 // SSM_WIDTH)
    lat_m = _norm_matmul(meta, g_mix, w_all, F32, META_ROWS, LAT_W, LAT_W, 0)
    u_m = _norm_matmul(meta, g_mix, w_all, BF16, META_ROWS, SSM_WIDTH, SSM_WIDTH, UG_ROW0 // SSM_WIDTH)

    q, k, v = _up_rope(lat, g_q, g_kv, wq, wkv, cosf[N_META:], sin1[N_META:], sin2[N_META:], 512)
    _, k_m, v_m = _up_rope(lat_m, g_q, g_kv, wq, wkv, cosf[:META_ROWS], sin1[:META_ROWS],
                            sin2[:META_ROWS], META_ROWS)
    attn = _attention(q, k, v, k_m, v_m, bsz, seq, 512)

    w_sin, ce_lags, t_lags, a1, a2 = _ssm_params(
        ssm_lambda_re[0], ssm_lambda_im[0], ssm_log_step[0], ssm_b_re[0], ssm_b_im[0],
        ssm_c_re[0], ssm_c_im[0], ssm_d[0], 8)
    u_meta = jnp.pad(u_m[:N_META].reshape(1, CHUNK, N_TILE, LANES).transpose(2, 0, 1, 3).reshape(N_TILE, 1, CW),
                     ((0, 0), (0, 15), (0, 0)))
    y = _ssm(u_seq, u_meta, w_sin, a1.reshape(N_TILE, SW), a2.reshape(N_TILE, SW), t_lags, ce_lags,
             bsz, n_chunks)

    mixed = _merge(attn, y, w_attn_proj[0].astype(BF16), w_glu_val[0].astype(BF16),
                   w_glu_gate[0].astype(BF16), gates, 1024, 1024)
    h1, n_ffn = _out_proj(x2, mixed, w_out[0].astype(BF16), norm_ffn[0].reshape(1, d), 512)
    out = _ffn(n_ffn, h1, w_ffn_gate[0].astype(BF16), w_ffn_up[0].astype(BF16),
               w_ffn_down[0].astype(BF16), norm_final.reshape(1, d), 512, 512)
    return out.reshape(bsz, seq, d)
```

```python
import functools
import math

import numpy as np
import jax
import jax.numpy as jnp
from jax import lax
from jax.experimental import pallas as pl
from jax.experimental.pallas import tpu as pltpu

F32 = jnp.float32
BF16 = jnp.bfloat16

D_MODEL = 2048
N_META = 16
EPS = 1e-6
HEADS = 16
Q_LORA = 512
KV_LORA = 512
QK_NOPE = 128
QK_ROPE = 64
V_HEAD = 128
ROPE_THETA = 10000.0
SSM_WIDTH = 1024
SSM_P = 16
SSM_G = 64
SSM_N = 64
D_FF = 5632
LANES = 128
MXU_N = 256
FFN_ACC_COLS = 512
SCAN_UNROLL = 8
HEAD_PAD = 2 * LANES
LAT_W = Q_LORA + KV_LORA + LANES
CHUNK = 16
CP = CHUNK * SSM_P
GPT = LANES // SSM_P
N_TILE = SSM_G // GPT
CW = CHUNK * LANES
SW = GPT * 2 * SSM_N
UG_ROW0 = 2048
NT_DIMS = (((1,), (1,)), ((), ()))
META_ROWS = 128
V_PAD = 2 * LANES
VMEM_LIMIT = 56 * 1024 * 1024


def _cparams(*sem):
    return pltpu.CompilerParams(dimension_semantics=sem, vmem_limit_bytes=VMEM_LIMIT)


def _sigmoid(x):
    return 0.5 + 0.5 * jnp.tanh(0.5 * x)


def _gelu_tanh(x):
    return x * (0.5 * (1.0 + jnp.tanh(math.sqrt(2.0 / math.pi) * (x + 0.044715 * (x * x * x)))))


def _rms(x, g):
    ms = jnp.mean(x * x, axis=-1, keepdims=True)
    return x * lax.rsqrt(ms + EPS) * g


def _w_in_prep_kernel(w_ref, o_ref):
    off_u = Q_LORA + KV_LORA + QK_ROPE
    o_ref[:off_u] = w_ref[:off_u].astype(BF16)
    o_ref[off_u:UG_ROW0] = jnp.zeros((UG_ROW0 - off_u, o_ref.shape[1]), BF16)
    o_ref[UG_ROW0:] = w_ref[off_u:].astype(BF16)


def _w_in_prep(wt, cb):
    n, k = wt.shape
    n_out = UG_ROW0 + n - (Q_LORA + KV_LORA + QK_ROPE)
    return pl.pallas_call(
        _w_in_prep_kernel,
        grid=(k // cb,),
        in_specs=[pl.BlockSpec((n, cb), lambda i: (0, i))],
        out_specs=pl.BlockSpec((n_out, cb), lambda i: (0, i)),
        out_shape=jax.ShapeDtypeStruct((n_out, k), BF16),
        compiler_params=_cparams("parallel"),
        name="w_in_prep",
    )(wt)


def _norm_matmul_kernel(x_ref, g_ref, w_ref, o_ref, *rest):
    n_ref = rest[-1]
    @pl.when(pl.program_id(1) == 0)
    def _():
        n_ref[...] = _rms(x_ref[...], g_ref[...]).astype(BF16)
        if len(rest) == 2:
            rest[0][...] = n_ref[...]

    n = n_ref[...]
    tn = o_ref.shape[1]
    for c0 in range(0, tn, 2 * MXU_N):
        cols = slice(c0, min(c0 + 2 * MXU_N, tn))
        o_ref[:, cols] = lax.dot_general(n, w_ref[cols, :], NT_DIMS,
                                         preferred_element_type=F32).astype(o_ref.dtype)


def _norm_matmul(x, g, wt, out_dtype, tm, tn, n, row_block0, emit_norm=False):
    m, k = x.shape
    out_specs = [pl.BlockSpec((tm, tn), lambda i, j: (i, j))]
    out_shape = [jax.ShapeDtypeStruct((m, n), out_dtype)]
    if emit_norm:
        out_specs.append(pl.BlockSpec((tm, k), lambda i, j: (i, 0)))
        out_shape.append(jax.ShapeDtypeStruct((m, k), BF16))
    return pl.pallas_call(
        _norm_matmul_kernel,
        grid=(m // tm, n // tn),
        in_specs=[
            pl.BlockSpec((tm, k), lambda i, j: (i, 0)),
            pl.BlockSpec((1, k), lambda i, j: (0, 0)),
            pl.BlockSpec((tn, k), lambda i, j: (row_block0 + j, 0)),
        ],
        out_specs=out_specs if emit_norm else out_specs[0],
        out_shape=out_shape if emit_norm else out_shape[0],
        scratch_shapes=[pltpu.VMEM((tm, k), BF16)],
        compiler_params=_cparams("parallel", "arbitrary"),
        name="norm_matmul",
    )(x, g, wt)


def _ug_proj_kernel(n_ref, w_ref, u_ref, gate_ref, stage_ref):
    j = pl.program_id(1)
    tm = n_ref.shape[0]

    @pl.when(j == 0)
    def _():
        n = n_ref[...]
        for c in range(SSM_WIDTH // MXU_N):
            acc = lax.dot_general(n, w_ref[c * MXU_N:(c + 1) * MXU_N, :], NT_DIMS,
                                  preferred_element_type=F32)
            for s in range(MXU_N // LANES):
                stage_ref[c * (MXU_N // LANES) + s] = acc[:, s * LANES:(s + 1) * LANES]
        for t in range(N_TILE):
            for sig in range(CHUNK):
                u_ref[t, :, sig * LANES:(sig + 1) * LANES] = (
                    stage_ref[t, pl.ds(sig, tm // CHUNK, stride=CHUNK), :].astype(u_ref.dtype))

    @pl.when(j > 0)
    def _():
        n = n_ref[...]
        tn = gate_ref.shape[1]
        for c0 in range(0, tn, 2 * MXU_N):
            cols = slice(c0, c0 + 2 * MXU_N)
            gate_ref[:, cols] = lax.dot_general(n, w_ref[cols, :], NT_DIMS,
                                                preferred_element_type=F32).astype(gate_ref.dtype)


def _ug_proj(n, wt, tm, tn, row_block0):
    m, k = n.shape
    n_gate = 2 * D_MODEL
    assert tn == SSM_WIDTH
    return pl.pallas_call(
        _ug_proj_kernel,
        grid=(m // tm, 1 + n_gate // tn),
        in_specs=[
            pl.BlockSpec((tm, k), lambda i, j: (i, 0)),
            pl.BlockSpec((tn, k), lambda i, j: (row_block0 + j, 0)),
        ],
        out_specs=[
            pl.BlockSpec((N_TILE, tm // CHUNK, CW), lambda i, j: (0, i, 0)),
            pl.BlockSpec((tm, tn), lambda i, j: (i, jnp.maximum(j - 1, 0))),
        ],
        out_shape=[jax.ShapeDtypeStruct((N_TILE, m // CHUNK, CW), BF16),
                   jax.ShapeDtypeStruct((m, n_gate), BF16)],
        scratch_shapes=[pltpu.VMEM((N_TILE, tm, LANES), F32)],
        compiler_params=_cparams("parallel", "arbitrary"),
        name="ug_proj",
    )(n, wt)


def _rope_slab(x, cosf, sin1, sin2):
    return (x * cosf + pltpu.roll(x, LANES - QK_ROPE // 2, 1) * sin1
            + pltpu.roll(x, QK_ROPE // 2, 1) * sin2)


def _up_rope_kernel(ql_ref, kvl_ref, kr_ref, gq_ref, gkv_ref, wq_ref, wkv_ref,
                    cos_ref, sin1_ref, sin2_ref, q_ref, k_ref, v_ref):
    cosf, sin1, sin2 = cos_ref[...], sin1_ref[...], sin2_ref[...]
    qn = _rms(ql_ref[...], gq_ref[...]).astype(BF16)
    kvn = _rms(kvl_ref[...], gkv_ref[...]).astype(BF16)
    k_rope = _rope_slab(kr_ref[...], cosf, sin1, sin2).astype(BF16)
    ones_cols = (lax.broadcasted_iota(jnp.int32, (ql_ref.shape[0], V_PAD - V_HEAD), 1) == 0
                 ).astype(F32).astype(BF16)
    for h in range(HEADS):
        acc = jnp.dot(qn, wq_ref[:, h * HEAD_PAD:(h + 1) * HEAD_PAD], preferred_element_type=F32)
        q_ref[h, :, :LANES] = acc[:, :LANES].astype(BF16)
        q_ref[h, :, LANES:] = _rope_slab(acc[:, LANES:], cosf, sin1, sin2).astype(BF16)
    kv_w = QK_NOPE + V_HEAD
    for h in range(HEADS):
        acc = jnp.dot(kvn, wkv_ref[:, h * kv_w:(h + 1) * kv_w], preferred_element_type=F32)
        k_ref[h, :, :LANES] = acc[:, :QK_NOPE].astype(BF16)
        k_ref[h, :, LANES:] = k_rope
        v_ref[h, :, :V_HEAD] = acc[:, QK_NOPE:].astype(BF16)
        v_ref[h, :, V_HEAD:] = ones_cols


def _up_rope(lat, gq, gkv, wq, wkv, cosf, sin1, sin2, tm):
    m = lat.shape[0]
    n_pos_blocks = cosf.shape[0] // tm
    row = lambda i: (i, 0)
    const = lambda i: (0, 0)
    pos = lambda i: (i % n_pos_blocks, 0)
    head_major = lambda i: (0, i, 0)
    return pl.pallas_call(
        _up_rope_kernel,
        grid=(m // tm,),
        in_specs=[
            pl.BlockSpec((tm, Q_LORA), row),
            pl.BlockSpec((tm, KV_LORA), lambda i: (i, 1)),
            pl.BlockSpec((tm, LANES), lambda i: (i, (Q_LORA + KV_LORA) // LANES)),
            pl.BlockSpec((1, Q_LORA), const),
            pl.BlockSpec((1, KV_LORA), const),
            pl.BlockSpec(wq.shape, const),
            pl.BlockSpec(wkv.shape, const),
            pl.BlockSpec((tm, LANES), pos),
            pl.BlockSpec((tm, LANES), pos),
            pl.BlockSpec((tm, LANES), pos),
        ],
        out_specs=[
            pl.BlockSpec((HEADS, tm, HEAD_PAD), head_major),
            pl.BlockSpec((HEADS, tm, HEAD_PAD), head_major),
            pl.BlockSpec((HEADS, tm, V_PAD), head_major),
        ],
        out_shape=[
            jax.ShapeDtypeStruct((HEADS, m, HEAD_PAD), BF16),
            jax.ShapeDtypeStruct((HEADS, m, HEAD_PAD), BF16),
            jax.ShapeDtypeStruct((HEADS, m, V_PAD), BF16),
        ],
        compiler_params=_cparams("parallel"),
        name="up_rope",
    )(lat, lat, lat, gq, gkv, wq, wkv, cosf, sin1, sin2)


def _attn_kernel(qi_ref, ki_ref, q_ref, k_ref, v_ref, km_ref, vm_ref, *refs):
    n_side = (len(refs) - 3) // 2
    side_in, o_ref, side_out = refs[:n_side], refs[n_side], refs[n_side + 1:2 * n_side + 1]
    m_ref, acc_ref = refs[2 * n_side + 1:]
    for w_ref, wb_ref in zip(side_in, side_out):
        wb_ref[...] = w_ref[...].astype(wb_ref.dtype)
    t = pl.program_id(1)
    qi = qi_ref[t]
    ki = ki_ref[t]

    def lane_tiled(x, width):
        return jnp.concatenate([x] * (width // LANES), axis=1)

    @pl.when(ki == 0)
    def _init():
        def body(h, c):
            s = lax.dot_general(q_ref[h], km_ref[h], NT_DIMS, preferred_element_type=F32)
            key = lax.broadcasted_iota(jnp.int32, s.shape, 1)
            s = jnp.where(key < N_META, s, -jnp.inf)
            m = jnp.broadcast_to(jnp.max(s, axis=-1, keepdims=True), s.shape)
            p = jnp.exp2(s - m)
            m_ref[h] = m
            acc_ref[h] = jnp.dot(p.astype(BF16), vm_ref[h], preferred_element_type=F32)
            return c
        lax.fori_loop(0, HEADS, body, 0, unroll=2)

    def tile(masked):
        def scores(h):
            return lax.dot_general(q_ref[h], k_ref[h], NT_DIMS, preferred_element_type=F32)

        def body(h, s):
            if masked:
                qry = lax.broadcasted_iota(jnp.int32, s.shape, 0)
                key = lax.broadcasted_iota(jnp.int32, s.shape, 1)
                s = jnp.where(key <= qry, s, -jnp.inf)
            m_prev = m_ref[h]
            m_new = jnp.maximum(m_prev, jnp.max(s, axis=-1, keepdims=True))
            alpha = jnp.exp2(m_prev - m_new)
            p = jnp.exp2(s - lane_tiled(m_new, s.shape[1]))
            acc_ref[h] = lane_tiled(alpha, V_PAD) * acc_ref[h] + jnp.dot(
                p.astype(BF16), v_ref[h], preferred_element_type=F32)
            m_ref[h] = m_new

        s_next = scores(0)
        for h in range(HEADS):
            s_cur = s_next
            if h + 1 < HEADS:
                s_next = scores(h + 1)
            body(h, s_cur)

    @pl.when(ki < qi)
    def _full():
        tile(False)

    @pl.when(ki == qi)
    def _diag():
        tile(True)
        for h in range(HEADS):
            out = acc_ref[h, :, :V_HEAD] / acc_ref[h, :, V_HEAD:V_HEAD + 1]
            o_ref[:, h * V_HEAD:(h + 1) * V_HEAD] = out.astype(o_ref.dtype)


def _attention(q, k, v, km, vm, side_weights, bsz, seq, tq):
    nq = seq // tq
    pairs = [(a, b) for a in range(nq) for b in range(a + 1)]
    qi_arr = jnp.asarray([p[0] for p in pairs], jnp.int32)
    ki_arr = jnp.asarray([p[1] for p in pairs], jnp.int32)
    n_steps = bsz * len(pairs)
    side_specs = []
    for w in side_weights:
        tiles = w.shape[0] // 16
        n_blk = max(nb for nb in range(1, n_steps + 1) if tiles % nb == 0)
        spec = pl.BlockSpec((w.shape[0] // n_blk, w.shape[1]),
                            lambda b, t, qi, ki, n_blk=n_blk: (jnp.minimum(b * len(pairs) + t, n_blk - 1), 0))
        side_specs.append(spec)
    grid_spec = pltpu.PrefetchScalarGridSpec(
        num_scalar_prefetch=2,
        grid=(bsz, len(pairs)),
        in_specs=[
            pl.BlockSpec((HEADS, tq, HEAD_PAD), lambda b, t, qi, ki: (0, b * nq + qi[t], 0)),
            pl.BlockSpec((HEADS, tq, HEAD_PAD), lambda b, t, qi, ki: (0, b * nq + ki[t], 0)),
            pl.BlockSpec((HEADS, tq, V_PAD), lambda b, t, qi, ki: (0, b * nq + ki[t], 0)),
            pl.BlockSpec(km.shape, lambda b, t, qi, ki: (0, 0, 0)),
            pl.BlockSpec(vm.shape, lambda b, t, qi, ki: (0, 0, 0)),
        ] + side_specs,
        out_specs=[pl.BlockSpec((tq, HEADS * V_HEAD), lambda b, t, qi, ki: (b * nq + qi[t], 0))] + side_specs,
        scratch_shapes=[
            pltpu.VMEM((HEADS, tq, LANES), F32),
            pltpu.VMEM((HEADS, tq, V_PAD), F32),
        ],
    )
    outs = pl.pallas_call(
        _attn_kernel,
        grid_spec=grid_spec,
        out_shape=[jax.ShapeDtypeStruct((bsz * seq, HEADS * V_HEAD), BF16)]
        + [jax.ShapeDtypeStruct(w.shape, BF16) for w in side_weights],
        compiler_params=_cparams("arbitrary", "arbitrary"),
        name="flash_attn",
    )(qi_arr, ki_arr, q, k, v, km, vm, *side_weights)
    return outs[0], outs[1:]


def _ssm_param_kernel(lr_ref, li_ref, ls_ref, bre_ref, bim_ref, cre_ref, cim_ref, d_ref,
                      win_ref, cen_ref, tl_ref, a1_ref, a2_ref):
    lane = lax.broadcasted_iota(jnp.int32, (SSM_P, LANES), 1)
    row = lax.broadcasted_iota(jnp.int32, (SSM_P, LANES), 0)
    on_diag = jnp.bitwise_and(lane, SSM_P - 1) == row
    cat = lambda x, y: jnp.concatenate([x, y], axis=1)
    for g in range(lr_ref.shape[0]):
        lr, li = lr_ref[g], li_ref[g]
        step = jnp.exp(ls_ref[g])
        kf = lax.broadcasted_iota(jnp.int32, (CHUNK + 1, SSM_N), 0).astype(F32)
        mag = jnp.exp(kf * (lr * step))
        ang = kf * (li * step)
        e_re, e_im = mag * jnp.cos(ang), mag * jnp.sin(ang)
        a1_ref[g] = cat(e_re[CHUNK:], e_re[CHUNK:])
        a2_ref[g] = cat(-e_im[CHUNK:], e_im[CHUNK:])
        nr, ni = e_re[1:2] - 1.0, e_im[1:2]
        den = lr * lr + li * li
        coef_re = (nr * lr + ni * li) / den
        coef_im = (ni * lr - nr * li) / den
        bre, bim = bre_ref[g], bim_ref[g]
        bb_re = coef_re * bre - coef_im * bim
        bb_im = coef_re * bim + coef_im * bre
        cre, cim = cre_ref[g], cim_ref[g]
        ce_re_lags, ce_im_lags = [], []
        for k in range(CHUNK + 1):
            er, ei = e_re[k:k + 1], e_im[k:k + 1]
            ce_re = cre * er - cim * ei
            ce_im = cre * ei + cim * er
            if k >= 1:
                cen_ref[g, k - 1] = cat(ce_re, -ce_im).astype(cen_ref.dtype)
            if k < CHUNK:
                ce_re_lags.append(ce_re)
                ce_im_lags.append(ce_im)
                r0 = (CHUNK - 1 - k) * SSM_P
                win_ref[g, r0:r0 + SSM_P, :] = cat(er * bb_re - ei * bb_im,
                                                   er * bb_im + ei * bb_re).astype(win_ref.dtype)
        tl = (lax.dot_general(jnp.concatenate(ce_re_lags, axis=0), jnp.concatenate([bb_re] * GPT, axis=0),
                              NT_DIMS, precision=lax.Precision.HIGHEST, preferred_element_type=F32)
              - lax.dot_general(jnp.concatenate(ce_im_lags, axis=0), jnp.concatenate([bb_im] * GPT, axis=0),
                                NT_DIMS, precision=lax.Precision.HIGHEST, preferred_element_type=F32))
        tl_ref[g] = tl
        tl_ref[g, :SSM_P, :] = tl[:SSM_P] + jnp.where(on_diag, d_ref[g], 0.0)


def _ssm_params(lam_re, lam_im, log_step, b_re, b_im, c_re, c_im, d, gb):
    g = SSM_G
    row = lambda shape: pl.BlockSpec((gb,) + shape, lambda i: (i,) + (0,) * len(shape))
    outs = [
        ((g, CP, 2 * SSM_N), (CP, 2 * SSM_N), BF16),
        ((g, CHUNK, SSM_P, 2 * SSM_N), (CHUNK, SSM_P, 2 * SSM_N), BF16),
        ((g, CP, LANES), (CP, LANES), F32),
        ((g, 1, 2 * SSM_N), (1, 2 * SSM_N), F32),
        ((g, 1, 2 * SSM_N), (1, 2 * SSM_N), F32),
    ]
    return pl.pallas_call(
        _ssm_param_kernel,
        grid=(g // gb,),
        in_specs=[row((1, SSM_N)), row((1, SSM_N)), row((1, 1)),
                  row((SSM_P, SSM_N)), row((SSM_P, SSM_N)), row((SSM_P, SSM_N)), row((SSM_P, SSM_N)),
                  row((SSM_P, 1))],
        out_specs=[row(blk) for _, blk, _ in outs],
        out_shape=[jax.ShapeDtypeStruct(full, dt) for full, _, dt in outs],
        compiler_params=_cparams("parallel"),
        name="ssm_params",
    )(lam_re.reshape(g, 1, SSM_N), lam_im.reshape(g, 1, SSM_N), log_step.reshape(g, 1, 1),
      jnp.swapaxes(b_re, 1, 2), jnp.swapaxes(b_im, 1, 2), c_re, c_im, d.reshape(g, SSM_P, 1))


def _ssm_kernel(u_ref, um_ref, wc_ref, a1_ref, a2_ref, tl_ref, cen_ref, y_ref,
                s_ref, s0_ref, win_ref, wt_ref, ws_ref, *, bsz, n_chunks):
    i = pl.program_id(0)
    rows = bsz * n_chunks

    @pl.when(i < N_TILE)
    def _state_inputs():
        @pl.when(i == 0)
        def _():
            win_ref[...] = jnp.zeros(win_ref.shape, win_ref.dtype)

        for sig in range(CHUNK):
            for g in range(GPT):
                r0 = sig * LANES + g * SSM_P
                win_ref[r0:r0 + SSM_P, g * 2 * SSM_N:(g + 1) * 2 * SSM_N] = (
                    wc_ref[g, sig * SSM_P:(sig + 1) * SSM_P, :])
        v = jnp.dot(u_ref[0], win_ref[...], preferred_element_type=F32)
        v_meta = jnp.dot(um_ref[0], win_ref[...], preferred_element_type=F32)
        for g in range(GPT):
            s_ref[g, pl.ds(i, rows, stride=N_TILE), :] = v[:, g * 2 * SSM_N:(g + 1) * 2 * SSM_N]
            s0_ref[g, pl.ds(i, 1), :] = v_meta[0:1, g * 2 * SSM_N:(g + 1) * 2 * SSM_N]

    @pl.when(i == N_TILE - 1)
    def _recurrence():
        a1 = [a1_ref[:, g * 2 * SSM_N:(g + 1) * 2 * SSM_N] for g in range(GPT)]
        a2 = [a2_ref[:, g * 2 * SSM_N:(g + 1) * 2 * SSM_N] for g in range(GPT)]
        for b in range(bsz):
            def body(j, carry):
                r = pl.multiple_of((b * n_chunks + j) * N_TILE, N_TILE)
                nxt = []
                for g in range(GPT):
                    s, sw = carry[2 * g], carry[2 * g + 1]
                    v = s_ref[g, pl.ds(r, N_TILE), :]
                    vw = pltpu.roll(v, SSM_N, 1)
                    s_ref[g, pl.ds(r, N_TILE), :] = s
                    nxt.append(a1[g] * s + a2[g] * sw + v)
                    nxt.append(a1[g] * sw - a2[g] * s + vw)
                return tuple(nxt)
            init = []
            for g in range(GPT):
                init += [s0_ref[g], pltpu.roll(s0_ref[g], SSM_N, 1)]
            lax.fori_loop(0, n_chunks, body, tuple(init), unroll=SCAN_UNROLL)

    @pl.when(i >= N_TILE)
    def _outputs():
        tile = i - N_TILE
        row_g = lax.shift_right_logical(lax.broadcasted_iota(jnp.int32, (LANES, LANES), 0), 4)
        col_g = lax.shift_right_logical(lax.broadcasted_iota(jnp.int32, (LANES, LANES), 1), 4)
        same_group = row_g == col_g
        @pl.when(i == N_TILE)
        def _():
            wt_ref[...] = jnp.zeros(wt_ref.shape, wt_ref.dtype)
            ws_ref[...] = jnp.zeros(ws_ref.shape, ws_ref.dtype)

        for k in range(CHUNK):
            rep = jnp.concatenate([tl_ref[g, k * SSM_P:(k + 1) * SSM_P, :] for g in range(GPT)], axis=0)
            blk = jnp.where(same_group, rep, 0.0).astype(BF16)
            for sig in range(CHUNK - k):
                tau = sig + k
                wt_ref[tau * LANES:(tau + 1) * LANES, sig * LANES:(sig + 1) * LANES] = blk
        for g in range(GPT):
            for tau in range(CHUNK):
                r0 = tau * LANES + g * SSM_P
                ws_ref[r0:r0 + SSM_P, g * 2 * SSM_N:(g + 1) * 2 * SSM_N] = cen_ref[g, tau]
        states = jnp.concatenate(
            [s_ref[g, pl.ds(tile, rows, stride=N_TILE), :] for g in range(GPT)], axis=1).astype(BF16)
        u = u_ref[0]
        steps_per_tile = MXU_N // LANES
        for c in range(CW // MXU_N):
            k_hi = (c + 1) * MXU_N
            out_rows = slice(c * MXU_N, k_hi)
            y = _gelu_tanh(
                lax.dot_general(u[:, :k_hi], wt_ref[out_rows, :k_hi], NT_DIMS, preferred_element_type=F32)
                + lax.dot_general(states, ws_ref[out_rows, :], NT_DIMS, preferred_element_type=F32))
            for s in range(steps_per_tile):
                tau = c * steps_per_tile + s
                y_ref[pl.ds(tau, rows, stride=CHUNK), :] = y[:, s * LANES:(s + 1) * LANES]


def _ssm(u, u_meta, w_compact, a1, a2, t_lags, ce_lags, bsz, n_chunks):
    rows = bsz * n_chunks
    in_tile = lambda i: (lax.rem(i, N_TILE), 0, 0)
    out_tile = lambda i: (jnp.maximum(i - N_TILE, 0), 0, 0)
    out_tile4 = lambda i: (jnp.maximum(i - N_TILE, 0), 0, 0, 0)
    whole = lambda i: (0, 0)
    return pl.pallas_call(
        functools.partial(_ssm_kernel, bsz=bsz, n_chunks=n_chunks),
        grid=(2 * N_TILE,),
        in_specs=[pl.BlockSpec((1, rows, CW), in_tile),
                  pl.BlockSpec((1,) + u_meta.shape[1:], in_tile),
                  pl.BlockSpec((GPT, CP, 2 * SSM_N), in_tile),
                  pl.BlockSpec((N_TILE, SW), whole),
                  pl.BlockSpec((N_TILE, SW), whole),
                  pl.BlockSpec((GPT, CP, LANES), out_tile),
                  pl.BlockSpec((GPT, CHUNK, SSM_P, 2 * SSM_N), out_tile4)],
        out_specs=pl.BlockSpec((rows * CHUNK, LANES), lambda i: (0, jnp.maximum(i - N_TILE, 0))),
        out_shape=jax.ShapeDtypeStruct((rows * CHUNK, SSM_WIDTH), F32),
        scratch_shapes=[pltpu.VMEM((GPT, rows * N_TILE, 2 * SSM_N), F32),
                        pltpu.VMEM((GPT, N_TILE, 2 * SSM_N), F32),
                        pltpu.VMEM((CW, SW), BF16),
                        pltpu.VMEM((CW, CW), BF16),
                        pltpu.VMEM((CW, SW), BF16)],
        compiler_params=_cparams("arbitrary"),
        name="ssm",
    )(u, u_meta, w_compact, a1, a2, t_lags, ce_lags)


def _merge_kernel(attn_ref, y_ref, wap_ref, wv_ref, wg_ref, g0_ref, g1_ref, o_ref):
    attn = attn_ref[...]
    y = y_ref[...].astype(BF16)
    for c in range(o_ref.shape[1] // MXU_N):
        cols = slice(c * MXU_N, (c + 1) * MXU_N)
        a = jnp.dot(attn, wap_ref[:, cols], preferred_element_type=F32)
        ssm = (jnp.dot(y, wv_ref[:, cols], preferred_element_type=F32)
               * _sigmoid(jnp.dot(y, wg_ref[:, cols], preferred_element_type=F32)))
        mixed = (_sigmoid(g0_ref[:, cols].astype(F32)) * a
                 + _sigmoid(g1_ref[:, cols].astype(F32)) * ssm)
        o_ref[:, cols] = mixed.astype(o_ref.dtype)


def _merge(attn, y, wap, wv, wg, gates, tm, tn):
    m = attn.shape[0]
    g1_off = D_MODEL // tn
    return pl.pallas_call(
        _merge_kernel,
        grid=(m // tm, D_MODEL // tn),
        in_specs=[
            pl.BlockSpec((tm, attn.shape[1]), lambda i, j: (i, 0)),
            pl.BlockSpec((tm, SSM_WIDTH), lambda i, j: (i, 0)),
            pl.BlockSpec((attn.shape[1], tn), lambda i, j: (0, j)),
            pl.BlockSpec((SSM_WIDTH, tn), lambda i, j: (0, j)),
            pl.BlockSpec((SSM_WIDTH, tn), lambda i, j: (0, j)),
            pl.BlockSpec((tm, tn), lambda i, j: (i, j)),
            pl.BlockSpec((tm, tn), lambda i, j: (i, g1_off + j)),
        ],
        out_specs=pl.BlockSpec((tm, tn), lambda i, j: (i, j)),
        out_shape=jax.ShapeDtypeStruct((m, D_MODEL), BF16),
        compiler_params=_cparams("parallel", "arbitrary"),
        name="gated_merge",
    )(attn, y, wap, wv, wg, gates, gates)


def _out_proj_kernel(x_ref, mx_ref, w_ref, g_ref, h_ref, n_ref):
    mx = mx_ref[...]
    d = h_ref.shape[1]
    sq = None
    for c in range(d // MXU_N):
        cols = slice(c * MXU_N, (c + 1) * MXU_N)
        h = x_ref[:, cols] + jnp.dot(mx, w_ref[:, cols], preferred_element_type=F32)
        h_ref[:, cols] = h
        part = jnp.sum(h * h, axis=-1, keepdims=True)
        sq = part if sq is None else sq + part
    inv = lax.rsqrt(sq * (1.0 / d) + EPS)
    n_ref[...] = (h_ref[...] * inv * g_ref[...]).astype(n_ref.dtype)


def _out_proj(x, mixed, w, g, tm):
    m, d = x.shape
    row = pl.BlockSpec((tm, d), lambda i: (i, 0))
    return pl.pallas_call(
        _out_proj_kernel,
        grid=(m // tm,),
        in_specs=[row, row, pl.BlockSpec((d, d), lambda i: (0, 0)), pl.BlockSpec((1, d), lambda i: (0, 0))],
        out_specs=[row, row],
        out_shape=[jax.ShapeDtypeStruct((m, d), F32), jax.ShapeDtypeStruct((m, d), BF16)],
        compiler_params=_cparams("parallel"),
        name="out_proj",
    )(x, mixed, w, g)


def _ffn_kernel(n_ref, h_ref, wg_ref, wu_ref, wd_ref, gf_ref, o_ref, acc_ref):
    f = pl.program_id(1)
    n = n_ref[...]
    tf = wg_ref.shape[1]
    d = wd_ref.shape[1]
    @pl.when(f == 0)
    def _():
        acc_ref[...] = jnp.zeros(acc_ref.shape, acc_ref.dtype)

    acts = []
    for c in range(tf // MXU_N):
        cols = slice(c * MXU_N, (c + 1) * MXU_N)
        gate = jnp.dot(n, wg_ref[:, cols], preferred_element_type=F32)
        up = jnp.dot(n, wu_ref[:, cols], preferred_element_type=F32)
        acts.append((gate * _sigmoid(gate) * up).astype(BF16))

    for c in range(d // FFN_ACC_COLS):
        cols = slice(c * FFN_ACC_COLS, (c + 1) * FFN_ACC_COLS)
        part = None
        for k, a in enumerate(acts):
            term = jnp.dot(a, wd_ref[k * MXU_N:(k + 1) * MXU_N, cols], preferred_element_type=F32)
            part = term if part is None else part + term
        acc_ref[:, cols] += part

    @pl.when(f == pl.num_programs(1) - 1)
    def _():
        o_ref[...] = _rms(h_ref[...] + acc_ref[...], gf_ref[...])


def _ffn(n, h, wg, wu, wd, gf, tm, tf):
    m, d = h.shape
    row = lambda i, f: (i, 0)
    return pl.pallas_call(
        _ffn_kernel,
        grid=(m // tm, D_FF // tf),
        in_specs=[
            pl.BlockSpec((tm, d), row),
            pl.BlockSpec((tm, d), row),
            pl.BlockSpec((d, tf), lambda i, f: (0, f)),
            pl.BlockSpec((d, tf), lambda i, f: (0, f)),
            pl.BlockSpec((tf, d), lambda i, f: (f, 0)),
            pl.BlockSpec((1, d), lambda i, f: (0, 0)),
        ],
        out_specs=pl.BlockSpec((tm, d), row),
        out_shape=jax.ShapeDtypeStruct((m, d), F32),
        scratch_shapes=[pltpu.VMEM((tm, d), F32)],
        compiler_params=_cparams("parallel", "arbitrary"),
        name="ffn",
    )(n, h, wg, wu, wd, gf)


def _rope_tables(length):
    half = QK_ROPE // 2
    freqs = np.float32(ROPE_THETA) ** (-np.arange(half, dtype=np.float32) / np.float32(half))
    ang = (np.arange(length, dtype=np.float32)[:, None] * freqs[None, :]).astype(np.float32)
    c, s = np.cos(ang).astype(np.float32), np.sin(ang).astype(np.float32)
    z = np.zeros_like(c)
    cosf = np.concatenate([c, c, z, z], axis=1)
    sin1 = np.concatenate([-s, z, z, z], axis=1)
    sin2 = np.concatenate([z, s, z, z], axis=1)
    return cosf, sin1, sin2


def kernel(x, meta_tokens, norm_mix, w_in, norm_q, w_q_up, norm_kv, w_kv_up, w_attn_proj, ssm_lambda_re, ssm_lambda_im, ssm_log_step, ssm_b_re, ssm_b_im, ssm_c_re, ssm_c_im, ssm_d, w_glu_val, w_glu_gate, w_out, norm_ffn, w_ffn_gate, w_ffn_up, w_ffn_down, norm_final):
    bsz, seq, d = x.shape
    assert d == D_MODEL and w_in.shape[0] == 1 and seq % 512 == 0 and N_META == CHUNK
    t_rows = bsz * seq
    n_chunks = seq // CHUNK
    bj = bsz * n_chunks
    x2 = x.reshape(t_rows, d)
    meta = jnp.pad(meta_tokens.astype(x.dtype), ((0, META_ROWS - N_META), (0, 0)))

    w_all = _w_in_prep(jnp.swapaxes(w_in[0], 0, 1), 256)
    scale = (QK_NOPE + QK_ROPE) ** -0.5 * math.log2(math.e)
    wq = jnp.pad((w_q_up[0] * scale).reshape(Q_LORA, HEADS, QK_NOPE + QK_ROPE),
                 ((0, 0), (0, 0), (0, HEAD_PAD - QK_NOPE - QK_ROPE))).reshape(Q_LORA, HEADS * HEAD_PAD).astype(BF16)
    wkv = w_kv_up[0].astype(BF16)
    g_mix = norm_mix[0].reshape(1, d)
    g_q = norm_q[0].reshape(1, Q_LORA)
    g_kv = norm_kv[0].reshape(1, KV_LORA)
    cosf, sin1, sin2 = _rope_tables(N_META + seq)

    lat, n_mix = _norm_matmul(x2, g_mix, w_all, F32, 512, LAT_W, LAT_W, 0, emit_norm=True)
    u_seq, gates = _ug_proj(n_mix, w_all, 1024, SSM_WIDTH, UG_ROW0 // SSM_WIDTH)
    lat_m = _norm_matmul(meta, g_mix, w_all, F32, META_ROWS, LAT_W, LAT_W, 0)
    u_m = _norm_matmul(meta, g_mix, w_all, BF16, META_ROWS, SSM_WIDTH, SSM_WIDTH, UG_ROW0 // SSM_WIDTH)

    q, k, v = _up_rope(lat, g_q, g_kv, wq, wkv, cosf[N_META:], sin1[N_META:], sin2[N_META:], 512)
    _, k_m, v_m = _up_rope(lat_m, g_q, g_kv, wq, wkv, cosf[:META_ROWS], sin1[:META_ROWS],
                            sin2[:META_ROWS], META_ROWS)
    attn, (w_ap, w_gv, w_gg, w_o, w_fg, w_fu, w_fd) = _attention(
        q, k, v, k_m, v_m,
        [w_attn_proj[0], w_glu_val[0], w_glu_gate[0], w_out[0], w_ffn_gate[0], w_ffn_up[0], w_ffn_down[0]],
        bsz, seq, 512)

    w_sin, ce_lags, t_lags, a1, a2 = _ssm_params(
        ssm_lambda_re[0], ssm_lambda_im[0], ssm_log_step[0], ssm_b_re[0], ssm_b_im[0],
        ssm_c_re[0], ssm_c_im[0], ssm_d[0], 8)
    u_meta = jnp.pad(u_m[:N_META].reshape(1, CHUNK, N_TILE, LANES).transpose(2, 0, 1, 3).reshape(N_TILE, 1, CW),
                     ((0, 0), (0, 15), (0, 0)))
    y = _ssm(u_seq, u_meta, w_sin, a1.reshape(N_TILE, SW), a2.reshape(N_TILE, SW), t_lags, ce_lags,
             bsz, n_chunks)

    mixed = _merge(attn, y, w_ap, w_gv, w_gg, gates, 1024, 1024)
    h1, n_ffn = _out_proj(x2, mixed, w_o, norm_ffn[0].reshape(1, d), 512)
    out = _ffn(n_ffn, h1, w_fg, w_fu, w_fd, norm_final.reshape(1, d), 512, 512)
    return out.reshape(bsz, seq, d)
```

```python
import functools
import math

import numpy as np
import jax
import jax.numpy as jnp
from jax import lax
from jax.experimental import pallas as pl
from jax.experimental.pallas import tpu as pltpu

F32 = jnp.float32
BF16 = jnp.bfloat16

D_MODEL = 2048
N_META = 16
EPS = 1e-6
HEADS = 16
Q_LORA = 512
KV_LORA = 512
QK_NOPE = 128
QK_ROPE = 64
V_HEAD = 128
ROPE_THETA = 10000.0
SSM_WIDTH = 1024
SSM_P = 16
SSM_G = 64
SSM_N = 64
D_FF = 5632
LANES = 128
MXU_N = 256
FFN_ACC_COLS = 512
SCAN_UNROLL = 8
HEAD_PAD = 2 * LANES
LAT_W = Q_LORA + KV_LORA + LANES
CHUNK = 16
CP = CHUNK * SSM_P
GPT = LANES // SSM_P
N_TILE = SSM_G // GPT
CW = CHUNK * LANES
SW = GPT * 2 * SSM_N
UG_ROW0 = 2048
NT_DIMS = (((1,), (1,)), ((), ()))
META_ROWS = 128
V_PAD = 2 * LANES
VMEM_LIMIT = 56 * 1024 * 1024
FFN_VMEM_LIMIT = 60 * 1024 * 1024


def _cparams(*sem):
    return pltpu.CompilerParams(dimension_semantics=sem, vmem_limit_bytes=VMEM_LIMIT)


def _sigmoid(x):
    return 0.5 + 0.5 * jnp.tanh(0.5 * x)


def _gelu_tanh(x):
    return x * (0.5 * (1.0 + jnp.tanh(math.sqrt(2.0 / math.pi) * (x + 0.044715 * (x * x * x)))))


def _rms(x, g):
    ms = jnp.mean(x * x, axis=-1, keepdims=True)
    return x * lax.rsqrt(ms + EPS) * g


def _w_in_prep_kernel(w_ref, o_ref):
    off_u = Q_LORA + KV_LORA + QK_ROPE
    o_ref[:off_u] = w_ref[:off_u].astype(BF16)
    o_ref[off_u:UG_ROW0] = jnp.zeros((UG_ROW0 - off_u, o_ref.shape[1]), BF16)
    o_ref[UG_ROW0:] = w_ref[off_u:].astype(BF16)


def _w_in_prep(wt, cb):
    n, k = wt.shape
    n_out = UG_ROW0 + n - (Q_LORA + KV_LORA + QK_ROPE)
    return pl.pallas_call(
        _w_in_prep_kernel,
        grid=(k // cb,),
        in_specs=[pl.BlockSpec((n, cb), lambda i: (0, i))],
        out_specs=pl.BlockSpec((n_out, cb), lambda i: (0, i)),
        out_shape=jax.ShapeDtypeStruct((n_out, k), BF16),
        compiler_params=_cparams("parallel"),
        name="w_in_prep",
    )(wt)


def _norm_matmul_kernel(x_ref, g_ref, w_ref, o_ref, *rest):
    n_ref = rest[-1]
    @pl.when(pl.program_id(1) == 0)
    def _():
        n_ref[...] = _rms(x_ref[...], g_ref[...]).astype(BF16)
        if len(rest) == 2:
            rest[0][...] = n_ref[...]

    n = n_ref[...]
    tn = o_ref.shape[1]
    for c0 in range(0, tn, 2 * MXU_N):
        cols = slice(c0, min(c0 + 2 * MXU_N, tn))
        o_ref[:, cols] = lax.dot_general(n, w_ref[cols, :], NT_DIMS,
                                         preferred_element_type=F32).astype(o_ref.dtype)


def _norm_matmul(x, g, wt, out_dtype, tm, tn, n, row_block0, emit_norm=False):
    m, k = x.shape
    out_specs = [pl.BlockSpec((tm, tn), lambda i, j: (i, j))]
    out_shape = [jax.ShapeDtypeStruct((m, n), out_dtype)]
    if emit_norm:
        out_specs.append(pl.BlockSpec((tm, k), lambda i, j: (i, 0)))
        out_shape.append(jax.ShapeDtypeStruct((m, k), BF16))
    return pl.pallas_call(
        _norm_matmul_kernel,
        grid=(m // tm, n // tn),
        in_specs=[
            pl.BlockSpec((tm, k), lambda i, j: (i, 0)),
            pl.BlockSpec((1, k), lambda i, j: (0, 0)),
            pl.BlockSpec((tn, k), lambda i, j: (row_block0 + j, 0)),
        ],
        out_specs=out_specs if emit_norm else out_specs[0],
        out_shape=out_shape if emit_norm else out_shape[0],
        scratch_shapes=[pltpu.VMEM((tm, k), BF16)],
        compiler_params=_cparams("parallel", "arbitrary"),
        name="norm_matmul",
    )(x, g, wt)


def _ug_proj_kernel(n_ref, w_ref, u_ref, gate_ref, stage_ref):
    j = pl.program_id(1)
    tm = n_ref.shape[0]

    @pl.when(j == 0)
    def _():
        n = n_ref[...]
        for c in range(SSM_WIDTH // MXU_N):
            acc = lax.dot_general(n, w_ref[c * MXU_N:(c + 1) * MXU_N, :], NT_DIMS,
                                  preferred_element_type=F32)
            for s in range(MXU_N // LANES):
                t = c * (MXU_N // LANES) + s
                stage_ref[t] = acc[:, s * LANES:(s + 1) * LANES]
                for sig in range(CHUNK):
                    u_ref[t, :, sig * LANES:(sig + 1) * LANES] = (
                        stage_ref[t, pl.ds(sig, tm // CHUNK, stride=CHUNK), :].astype(u_ref.dtype))

    @pl.when(j > 0)
    def _():
        n = n_ref[...]
        tn = gate_ref.shape[1]
        for c0 in range(0, tn, 2 * MXU_N):
            cols = slice(c0, c0 + 2 * MXU_N)
            gate_ref[:, cols] = lax.dot_general(n, w_ref[cols, :], NT_DIMS,
                                                preferred_element_type=F32).astype(gate_ref.dtype)


def _ug_proj(n, wt, tm, tn, row_block0):
    m, k = n.shape
    n_gate = 2 * D_MODEL
    assert tn == SSM_WIDTH
    return pl.pallas_call(
        _ug_proj_kernel,
        grid=(m // tm, 1 + n_gate // tn),
        in_specs=[
            pl.BlockSpec((tm, k), lambda i, j: (i, 0)),
            pl.BlockSpec((tn, k), lambda i, j: (row_block0 + j, 0)),
        ],
        out_specs=[
            pl.BlockSpec((N_TILE, tm // CHUNK, CW), lambda i, j: (0, i, 0)),
            pl.BlockSpec((tm, tn), lambda i, j: (i, jnp.maximum(j - 1, 0))),
        ],
        out_shape=[jax.ShapeDtypeStruct((N_TILE, m // CHUNK, CW), BF16),
                   jax.ShapeDtypeStruct((m, n_gate), BF16)],
        scratch_shapes=[pltpu.VMEM((N_TILE, tm, LANES), F32)],
        compiler_params=_cparams("parallel", "arbitrary"),
        name="ug_proj",
    )(n, wt)


def _rope_slab(x, cosf, sin1, sin2):
    return (x * cosf + pltpu.roll(x, LANES - QK_ROPE // 2, 1) * sin1
            + pltpu.roll(x, QK_ROPE // 2, 1) * sin2)


def _up_rope_kernel(ql_ref, kvl_ref, kr_ref, gq_ref, gkv_ref, wq_ref, wkv_ref,
                    cos_ref, sin1_ref, sin2_ref, q_ref, k_ref, v_ref):
    cosf, sin1, sin2 = cos_ref[...], sin1_ref[...], sin2_ref[...]
    qn = _rms(ql_ref[...], gq_ref[...]).astype(BF16)
    kvn = _rms(kvl_ref[...], gkv_ref[...]).astype(BF16)
    k_rope = _rope_slab(kr_ref[...], cosf, sin1, sin2).astype(BF16)
    ones_cols = (lax.broadcasted_iota(jnp.int32, (ql_ref.shape[0], V_PAD - V_HEAD), 1) == 0
                 ).astype(F32).astype(BF16)
    for h in range(HEADS):
        acc = jnp.dot(qn, wq_ref[:, h * HEAD_PAD:(h + 1) * HEAD_PAD], preferred_element_type=F32)
        q_ref[h, :, :LANES] = acc[:, :LANES].astype(BF16)
        q_ref[h, :, LANES:] = _rope_slab(acc[:, LANES:], cosf, sin1, sin2).astype(BF16)
    kv_w = QK_NOPE + V_HEAD
    for h in range(HEADS):
        acc = jnp.dot(kvn, wkv_ref[:, h * kv_w:(h + 1) * kv_w], preferred_element_type=F32)
        k_ref[h, :, :LANES] = acc[:, :QK_NOPE].astype(BF16)
        k_ref[h, :, LANES:] = k_rope
        v_ref[h, :, :V_HEAD] = acc[:, QK_NOPE:].astype(BF16)
        v_ref[h, :, V_HEAD:] = ones_cols


def _up_rope(lat, gq, gkv, wq, wkv, cosf, sin1, sin2, tm):
    m = lat.shape[0]
    n_pos_blocks = cosf.shape[0] // tm
    row = lambda i: (i, 0)
    const = lambda i: (0, 0)
    pos = lambda i: (i % n_pos_blocks, 0)
    head_major = lambda i: (0, i, 0)
    return pl.pallas_call(
        _up_rope_kernel,
        grid=(m // tm,),
        in_specs=[
            pl.BlockSpec((tm, Q_LORA), row),
            pl.BlockSpec((tm, KV_LORA), lambda i: (i, 1)),
            pl.BlockSpec((tm, LANES), lambda i: (i, (Q_LORA + KV_LORA) // LANES)),
            pl.BlockSpec((1, Q_LORA), const),
            pl.BlockSpec((1, KV_LORA), const),
            pl.BlockSpec(wq.shape, const),
            pl.BlockSpec(wkv.shape, const),
            pl.BlockSpec((tm, LANES), pos),
            pl.BlockSpec((tm, LANES), pos),
            pl.BlockSpec((tm, LANES), pos),
        ],
        out_specs=[
            pl.BlockSpec((HEADS, tm, HEAD_PAD), head_major),
            pl.BlockSpec((HEADS, tm, HEAD_PAD), head_major),
            pl.BlockSpec((HEADS, tm, V_PAD), head_major),
        ],
        out_shape=[
            jax.ShapeDtypeStruct((HEADS, m, HEAD_PAD), BF16),
            jax.ShapeDtypeStruct((HEADS, m, HEAD_PAD), BF16),
            jax.ShapeDtypeStruct((HEADS, m, V_PAD), BF16),
        ],
        compiler_params=_cparams("parallel"),
        name="up_rope",
    )(lat, lat, lat, gq, gkv, wq, wkv, cosf, sin1, sin2)


def _attn_kernel(qi_ref, ki_ref, q_ref, k_ref, v_ref, km_ref, vm_ref, *refs):
    n_side = (len(refs) - 3) // 2
    side_in, o_ref, side_out = refs[:n_side], refs[n_side], refs[n_side + 1:2 * n_side + 1]
    m_ref, acc_ref = refs[2 * n_side + 1:]
    for w_ref, wb_ref in zip(side_in, side_out):
        wb_ref[...] = w_ref[...].astype(wb_ref.dtype)
    t = pl.program_id(1)
    qi = qi_ref[t]
    ki = ki_ref[t]

    def lane_tiled(x, width):
        return jnp.concatenate([x] * (width // LANES), axis=1)

    @pl.when(ki == 0)
    def _init():
        def body(h, c):
            s = lax.dot_general(q_ref[h], km_ref[h], NT_DIMS, preferred_element_type=F32)
            key = lax.broadcasted_iota(jnp.int32, s.shape, 1)
            s = jnp.where(key < N_META, s, -jnp.inf)
            m = jnp.broadcast_to(jnp.max(s, axis=-1, keepdims=True), s.shape)
            p = jnp.exp2(s - m)
            m_ref[h] = m
            acc_ref[h] = jnp.dot(p.astype(BF16), vm_ref[h], preferred_element_type=F32)
            return c
        lax.fori_loop(0, HEADS, body, 0, unroll=2)

    def tile(masked):
        def scores(h):
            return lax.dot_general(q_ref[h], k_ref[h], NT_DIMS, preferred_element_type=F32)

        def body(h, s):
            if masked:
                qry = lax.broadcasted_iota(jnp.int32, s.shape, 0)
                key = lax.broadcasted_iota(jnp.int32, s.shape, 1)
                s = jnp.where(key <= qry, s, -jnp.inf)
            m_prev = m_ref[h]
            m_new = jnp.maximum(m_prev, jnp.max(s, axis=-1, keepdims=True))
            alpha = jnp.exp2(m_prev - m_new)
            p = jnp.exp2(s - lane_tiled(m_new, s.shape[1]))
            acc_ref[h] = lane_tiled(alpha, V_PAD) * acc_ref[h] + jnp.dot(
                p.astype(BF16), v_ref[h], preferred_element_type=F32)
            m_ref[h] = m_new

        s_next = scores(0)
        for h in range(HEADS):
            s_cur = s_next
            if h + 1 < HEADS:
                s_next = scores(h + 1)
            body(h, s_cur)

    @pl.when(ki < qi)
    def _full():
        tile(False)

    @pl.when(ki == qi)
    def _diag():
        tile(True)
        for h in range(HEADS):
            out = acc_ref[h, :, :V_HEAD] / acc_ref[h, :, V_HEAD:V_HEAD + 1]
            o_ref[:, h * V_HEAD:(h + 1) * V_HEAD] = out.astype(o_ref.dtype)


def _attention(q, k, v, km, vm, side_weights, bsz, seq, tq):
    nq = seq // tq
    pairs = [(a, b) for a in range(nq) for b in range(a + 1)]
    qi_arr = jnp.asarray([p[0] for p in pairs], jnp.int32)
    ki_arr = jnp.asarray([p[1] for p in pairs], jnp.int32)
    n_steps = bsz * len(pairs)
    side_specs = []
    for w in side_weights:
        tiles = w.shape[0] // 16
        n_blk = max(nb for nb in range(1, n_steps + 1) if tiles % nb == 0)
        spec = pl.BlockSpec((w.shape[0] // n_blk, w.shape[1]),
                            lambda b, t, qi, ki, n_blk=n_blk: (jnp.minimum(b * len(pairs) + t, n_blk - 1), 0))
        side_specs.append(spec)
    grid_spec = pltpu.PrefetchScalarGridSpec(
        num_scalar_prefetch=2,
        grid=(bsz, len(pairs)),
        in_specs=[
            pl.BlockSpec((HEADS, tq, HEAD_PAD), lambda b, t, qi, ki: (0, b * nq + qi[t], 0)),
            pl.BlockSpec((HEADS, tq, HEAD_PAD), lambda b, t, qi, ki: (0, b * nq + ki[t], 0)),
            pl.BlockSpec((HEADS, tq, V_PAD), lambda b, t, qi, ki: (0, b * nq + ki[t], 0)),
            pl.BlockSpec(km.shape, lambda b, t, qi, ki: (0, 0, 0)),
            pl.BlockSpec(vm.shape, lambda b, t, qi, ki: (0, 0, 0)),
        ] + side_specs,
        out_specs=[pl.BlockSpec((tq, HEADS * V_HEAD), lambda b, t, qi, ki: (b * nq + qi[t], 0))] + side_specs,
        scratch_shapes=[
            pltpu.VMEM((HEADS, tq, LANES), F32),
            pltpu.VMEM((HEADS, tq, V_PAD), F32),
        ],
    )
    outs = pl.pallas_call(
        _attn_kernel,
        grid_spec=grid_spec,
        out_shape=[jax.ShapeDtypeStruct((bsz * seq, HEADS * V_HEAD), BF16)]
        + [jax.ShapeDtypeStruct(w.shape, BF16) for w in side_weights],
        compiler_params=_cparams("arbitrary", "arbitrary"),
        name="flash_attn",
    )(qi_arr, ki_arr, q, k, v, km, vm, *side_weights)
    return outs[0], outs[1:]


def _ssm_param_kernel(lr_ref, li_ref, ls_ref, bre_ref, bim_ref, cre_ref, cim_ref, d_ref,
                      win_ref, cen_ref, tl_ref, a1_ref, a2_ref):
    lane = lax.broadcasted_iota(jnp.int32, (SSM_P, LANES), 1)
    row = lax.broadcasted_iota(jnp.int32, (SSM_P, LANES), 0)
    on_diag = jnp.bitwise_and(lane, SSM_P - 1) == row
    cat = lambda x, y: jnp.concatenate([x, y], axis=1)
    lr_all, li_all = lr_ref[0], li_ref[0]
    step_all = jnp.exp(ls_ref[0])
    mag = jnp.exp(lr_all * step_all)
    ab_re, ab_im = mag * jnp.cos(li_all * step_all), mag * jnp.sin(li_all * step_all)
    den = lr_all * lr_all + li_all * li_all
    nr, ni = ab_re - 1.0, ab_im
    coef_re_all = (nr * lr_all + ni * li_all) / den
    coef_im_all = (ni * lr_all - nr * li_all) / den
    for g in range(bre_ref.shape[0]):
        a_re, a_im = ab_re[g:g + 1], ab_im[g:g + 1]
        coef_re, coef_im = coef_re_all[g:g + 1], coef_im_all[g:g + 1]
        bre, bim = bre_ref[g], bim_ref[g]
        bb_re = coef_re * bre - coef_im * bim
        bb_im = coef_re * bim + coef_im * bre
        cre, cim = cre_ref[g], cim_ref[g]
        ce_re_lags, ce_im_lags = [], []
        er, ei = jnp.ones_like(a_re), jnp.zeros_like(a_im)
        for k in range(CHUNK + 1):
            if k == CHUNK:
                a1_ref[g] = cat(er, er)
                a2_ref[g] = cat(-ei, ei)
            ce_re = cre * er - cim * ei
            ce_im = cre * ei + cim * er
            if k >= 1:
                cen_ref[g, k - 1] = cat(ce_re, -ce_im).astype(cen_ref.dtype)
            if k < CHUNK:
                ce_re_lags.append(ce_re)
                ce_im_lags.append(ce_im)
                r0 = (CHUNK - 1 - k) * SSM_P
                win_ref[g, r0:r0 + SSM_P, :] = cat(er * bb_re - ei * bb_im,
                                                   er * bb_im + ei * bb_re).astype(win_ref.dtype)
            er, ei = er * a_re - ei * a_im, er * a_im + ei * a_re
        tl = (lax.dot_general(jnp.concatenate(ce_re_lags, axis=0), jnp.concatenate([bb_re] * GPT, axis=0),
                              NT_DIMS, precision=lax.Precision.HIGHEST, preferred_element_type=F32)
              - lax.dot_general(jnp.concatenate(ce_im_lags, axis=0), jnp.concatenate([bb_im] * GPT, axis=0),
                                NT_DIMS, precision=lax.Precision.HIGHEST, preferred_element_type=F32))
        tl_ref[g] = tl
        tl_ref[g, :SSM_P, :] = tl[:SSM_P] + jnp.where(on_diag, d_ref[g], 0.0)


def _ssm_params(lam_re, lam_im, log_step, b_re, b_im, c_re, c_im, d, gb):
    g = SSM_G
    row = lambda shape: pl.BlockSpec((gb,) + shape, lambda i: (i,) + (0,) * len(shape))
    step_blk = lambda width: pl.BlockSpec((1, gb, width), lambda i: (i, 0, 0))
    outs = [
        ((g, CP, 2 * SSM_N), (CP, 2 * SSM_N), BF16),
        ((g, CHUNK, SSM_P, 2 * SSM_N), (CHUNK, SSM_P, 2 * SSM_N), BF16),
        ((g, CP, LANES), (CP, LANES), F32),
        ((g, 1, 2 * SSM_N), (1, 2 * SSM_N), F32),
        ((g, 1, 2 * SSM_N), (1, 2 * SSM_N), F32),
    ]
    return pl.pallas_call(
        _ssm_param_kernel,
        grid=(g // gb,),
        in_specs=[step_blk(SSM_N), step_blk(SSM_N), step_blk(1),
                  row((SSM_P, SSM_N)), row((SSM_P, SSM_N)), row((SSM_P, SSM_N)), row((SSM_P, SSM_N)),
                  row((SSM_P, 1))],
        out_specs=[row(blk) for _, blk, _ in outs],
        out_shape=[jax.ShapeDtypeStruct(full, dt) for full, _, dt in outs],
        compiler_params=_cparams("parallel"),
        name="ssm_params",
    )(lam_re.reshape(g // gb, gb, SSM_N), lam_im.reshape(g // gb, gb, SSM_N), log_step.reshape(g // gb, gb, 1),
      jnp.swapaxes(b_re, 1, 2), jnp.swapaxes(b_im, 1, 2), c_re, c_im, d.reshape(g, SSM_P, 1))


def _ssm_kernel(u_ref, um_ref, wc_ref, a1_ref, a2_ref, tl_ref, cen_ref, y_ref,
                s_ref, s0_ref, win_ref, wt_ref, ws_ref, *, bsz, n_chunks):
    i = pl.program_id(0)
    rows = bsz * n_chunks

    @pl.when(i < N_TILE)
    def _state_inputs():
        @pl.when(i == 0)
        def _():
            win_ref[...] = jnp.zeros(win_ref.shape, win_ref.dtype)

        for sig in range(CHUNK):
            for g in range(GPT):
                r0 = sig * LANES + g * SSM_P
                win_ref[r0:r0 + SSM_P, g * 2 * SSM_N:(g + 1) * 2 * SSM_N] = (
                    wc_ref[g, sig * SSM_P:(sig + 1) * SSM_P, :])
        v = jnp.dot(u_ref[0], win_ref[...], preferred_element_type=F32)
        v_meta = jnp.dot(um_ref[0], win_ref[...], preferred_element_type=F32)
        for g in range(GPT):
            s_ref[g, pl.ds(i, rows, stride=N_TILE), :] = v[:, g * 2 * SSM_N:(g + 1) * 2 * SSM_N]
            s0_ref[g, pl.ds(i, 1), :] = v_meta[0:1, g * 2 * SSM_N:(g + 1) * 2 * SSM_N]

    @pl.when(i == N_TILE - 1)
    def _recurrence():
        a1 = [a1_ref[:, g * 2 * SSM_N:(g + 1) * 2 * SSM_N] for g in range(GPT)]
        a2 = [a2_ref[:, g * 2 * SSM_N:(g + 1) * 2 * SSM_N] for g in range(GPT)]
        for b in range(bsz):
            def body(j, carry):
                r = pl.multiple_of((b * n_chunks + j) * N_TILE, N_TILE)
                nxt = []
                for g in range(GPT):
                    s, sw = carry[2 * g], carry[2 * g + 1]
                    v = s_ref[g, pl.ds(r, N_TILE), :]
                    vw = pltpu.roll(v, SSM_N, 1)
                    s_ref[g, pl.ds(r, N_TILE), :] = s
                    nxt.append(a1[g] * s + a2[g] * sw + v)
                    nxt.append(a1[g] * sw - a2[g] * s + vw)
                return tuple(nxt)
            init = []
            for g in range(GPT):
                init += [s0_ref[g], pltpu.roll(s0_ref[g], SSM_N, 1)]
            lax.fori_loop(0, n_chunks, body, tuple(init), unroll=SCAN_UNROLL)

    @pl.when(i >= N_TILE)
    def _outputs():
        tile = i - N_TILE
        row_g = lax.shift_right_logical(lax.broadcasted_iota(jnp.int32, (LANES, LANES), 0), 4)
        col_g = lax.shift_right_logical(lax.broadcasted_iota(jnp.int32, (LANES, LANES), 1), 4)
        same_group = row_g == col_g
        @pl.when(i == N_TILE)
        def _():
            wt_ref[...] = jnp.zeros(wt_ref.shape, wt_ref.dtype)
            ws_ref[...] = jnp.zeros(ws_ref.shape, ws_ref.dtype)

        for k in range(CHUNK):
            rep = jnp.concatenate([tl_ref[g, k * SSM_P:(k + 1) * SSM_P, :] for g in range(GPT)], axis=0)
            blk = jnp.where(same_group, rep, 0.0).astype(BF16)
            for sig in range(CHUNK - k):
                tau = sig + k
                wt_ref[tau * LANES:(tau + 1) * LANES, sig * LANES:(sig + 1) * LANES] = blk
        for g in range(GPT):
            for tau in range(CHUNK):
                r0 = tau * LANES + g * SSM_P
                ws_ref[r0:r0 + SSM_P, g * 2 * SSM_N:(g + 1) * 2 * SSM_N] = cen_ref[g, tau]
        states = jnp.concatenate(
            [s_ref[g, pl.ds(tile, rows, stride=N_TILE), :] for g in range(GPT)], axis=1).astype(BF16)
        u = u_ref[0]
        steps_per_tile = MXU_N // LANES
        for c in range(CW // MXU_N):
            k_hi = (c + 1) * MXU_N
            out_rows = slice(c * MXU_N, k_hi)
            y = _gelu_tanh(
                lax.dot_general(u[:, :k_hi], wt_ref[out_rows, :k_hi], NT_DIMS, preferred_element_type=F32)
                + lax.dot_general(states, ws_ref[out_rows, :], NT_DIMS, preferred_element_type=F32))
            for s in range(steps_per_tile):
                tau = c * steps_per_tile + s
                y_ref[pl.ds(tau, rows, stride=CHUNK), :] = y[:, s * LANES:(s + 1) * LANES]


def _ssm(u, u_meta, w_compact, a1, a2, t_lags, ce_lags, bsz, n_chunks):
    rows = bsz * n_chunks
    in_tile = lambda i: (lax.rem(i, N_TILE), 0, 0)
    out_tile = lambda i: (jnp.maximum(i - N_TILE, 0), 0, 0)
    out_tile4 = lambda i: (jnp.maximum(i - N_TILE, 0), 0, 0, 0)
    whole = lambda i: (0, 0)
    return pl.pallas_call(
        functools.partial(_ssm_kernel, bsz=bsz, n_chunks=n_chunks),
        grid=(2 * N_TILE,),
        in_specs=[pl.BlockSpec((1, rows, CW), in_tile),
                  pl.BlockSpec((1,) + u_meta.shape[1:], in_tile),
                  pl.BlockSpec((GPT, CP, 2 * SSM_N), in_tile),
                  pl.BlockSpec((N_TILE, SW), whole),
                  pl.BlockSpec((N_TILE, SW), whole),
                  pl.BlockSpec((GPT, CP, LANES), out_tile),
                  pl.BlockSpec((GPT, CHUNK, SSM_P, 2 * SSM_N), out_tile4)],
        out_specs=pl.BlockSpec((rows * CHUNK, LANES), lambda i: (0, jnp.maximum(i - N_TILE, 0))),
        out_shape=jax.ShapeDtypeStruct((rows * CHUNK, SSM_WIDTH), F32),
        scratch_shapes=[pltpu.VMEM((GPT, rows * N_TILE, 2 * SSM_N), F32),
                        pltpu.VMEM((GPT, N_TILE, 2 * SSM_N), F32),
                        pltpu.VMEM((CW, SW), BF16),
                        pltpu.VMEM((CW, CW), BF16),
                        pltpu.VMEM((CW, SW), BF16)],
        compiler_params=_cparams("arbitrary"),
        name="ssm",
    )(u, u_meta, w_compact, a1, a2, t_lags, ce_lags)


def _merge_kernel(attn_ref, y_ref, wap_ref, wv_ref, wg_ref, g0_ref, g1_ref, o_ref):
    attn = attn_ref[...]
    y = y_ref[...].astype(BF16)
    for c in range(o_ref.shape[1] // MXU_N):
        cols = slice(c * MXU_N, (c + 1) * MXU_N)
        a = jnp.dot(attn, wap_ref[:, cols], preferred_element_type=F32)
        ssm = (jnp.dot(y, wv_ref[:, cols], preferred_element_type=F32)
               * _sigmoid(jnp.dot(y, wg_ref[:, cols], preferred_element_type=F32)))
        mixed = (_sigmoid(g0_ref[:, cols].astype(F32)) * a
                 + _sigmoid(g1_ref[:, cols].astype(F32)) * ssm)
        o_ref[:, cols] = mixed.astype(o_ref.dtype)


def _merge(attn, y, wap, wv, wg, gates, tm, tn):
    m = attn.shape[0]
    g1_off = D_MODEL // tn
    return pl.pallas_call(
        _merge_kernel,
        grid=(m // tm, D_MODEL // tn),
        in_specs=[
            pl.BlockSpec((tm, attn.shape[1]), lambda i, j: (i, 0)),
            pl.BlockSpec((tm, SSM_WIDTH), lambda i, j: (i, 0)),
            pl.BlockSpec((attn.shape[1], tn), lambda i, j: (0, j)),
            pl.BlockSpec((SSM_WIDTH, tn), lambda i, j: (0, j)),
            pl.BlockSpec((SSM_WIDTH, tn), lambda i, j: (0, j)),
            pl.BlockSpec((tm, tn), lambda i, j: (i, j)),
            pl.BlockSpec((tm, tn), lambda i, j: (i, g1_off + j)),
        ],
        out_specs=pl.BlockSpec((tm, tn), lambda i, j: (i, j)),
        out_shape=jax.ShapeDtypeStruct((m, D_MODEL), BF16),
        compiler_params=_cparams("parallel", "arbitrary"),
        name="gated_merge",
    )(attn, y, wap, wv, wg, gates, gates)


def _out_proj_kernel(x_ref, mx_ref, w_ref, g_ref, h_ref, n_ref):
    mx = mx_ref[...]
    d = h_ref.shape[1]
    sq = None
    for c in range(d // MXU_N):
        cols = slice(c * MXU_N, (c + 1) * MXU_N)
        h = x_ref[:, cols] + jnp.dot(mx, w_ref[:, cols], preferred_element_type=F32)
        h_ref[:, cols] = h
        part = jnp.sum(h * h, axis=-1, keepdims=True)
        sq = part if sq is None else sq + part
    inv = lax.rsqrt(sq * (1.0 / d) + EPS)
    n_ref[...] = (h_ref[...] * inv * g_ref[...]).astype(n_ref.dtype)


def _out_proj(x, mixed, w, g, tm):
    m, d = x.shape
    row = pl.BlockSpec((tm, d), lambda i: (i, 0))
    return pl.pallas_call(
        _out_proj_kernel,
        grid=(m // tm,),
        in_specs=[row, row, pl.BlockSpec((d, d), lambda i: (0, 0)), pl.BlockSpec((1, d), lambda i: (0, 0))],
        out_specs=[row, row],
        out_shape=[jax.ShapeDtypeStruct((m, d), F32), jax.ShapeDtypeStruct((m, d), BF16)],
        compiler_params=_cparams("parallel"),
        name="out_proj",
    )(x, mixed, w, g)


def _ffn_kernel(n_ref, h_hbm, wg_ref, wu_ref, wd_ref, gf_ref, o_ref, h_buf, h_sem):
    i = pl.program_id(0)
    f = pl.program_id(1)
    tm = n_ref.shape[0]
    tf = wg_ref.shape[1]
    d = wd_ref.shape[1]

    def residual_copy():
        rows = pl.ds(pl.multiple_of(i * tm, tm), tm)
        return pltpu.make_async_copy(h_hbm.at[rows], h_buf, h_sem)

    @pl.when(f == 0)
    def _():
        residual_copy().start()
        o_ref[...] = jnp.zeros(o_ref.shape, o_ref.dtype)

    n = n_ref[...]
    acts = []
    for c in range(tf // MXU_N):
        cols = slice(c * MXU_N, (c + 1) * MXU_N)
        gate = jnp.dot(n, wg_ref[:, cols], preferred_element_type=F32)
        up = jnp.dot(n, wu_ref[:, cols], preferred_element_type=F32)
        acts.append((gate * _sigmoid(gate) * up).astype(BF16))

    for c in range(d // FFN_ACC_COLS):
        cols = slice(c * FFN_ACC_COLS, (c + 1) * FFN_ACC_COLS)
        part = None
        for k, a in enumerate(acts):
            term = jnp.dot(a, wd_ref[k * MXU_N:(k + 1) * MXU_N, cols], preferred_element_type=F32)
            part = term if part is None else part + term
        o_ref[:, cols] += part

    @pl.when(f == pl.num_programs(1) - 1)
    def _():
        residual_copy().wait()
        o_ref[...] = _rms(h_buf[...] + o_ref[...], gf_ref[...])


def _ffn(n, h, wg, wu, wd, gf, tm, tf):
    m, d = h.shape
    row = lambda i, f: (i, 0)
    return pl.pallas_call(
        _ffn_kernel,
        grid=(m // tm, D_FF // tf),
        in_specs=[
            pl.BlockSpec((tm, d), row),
            pl.BlockSpec(memory_space=pl.ANY),
            pl.BlockSpec((d, tf), lambda i, f: (0, f)),
            pl.BlockSpec((d, tf), lambda i, f: (0, f)),
            pl.BlockSpec((tf, d), lambda i, f: (f, 0)),
            pl.BlockSpec((1, d), lambda i, f: (0, 0)),
        ],
        out_specs=pl.BlockSpec((tm, d), row),
        out_shape=jax.ShapeDtypeStruct((m, d), F32),
        scratch_shapes=[pltpu.VMEM((tm, d), F32), pltpu.SemaphoreType.DMA(())],
        compiler_params=pltpu.CompilerParams(dimension_semantics=("arbitrary", "arbitrary"),
                                             vmem_limit_bytes=FFN_VMEM_LIMIT),
        name="ffn",
    )(n, h, wg, wu, wd, gf)


def _rope_tables(length):
    half = QK_ROPE // 2
    freqs = np.float32(ROPE_THETA) ** (-np.arange(half, dtype=np.float32) / np.float32(half))
    ang = (np.arange(length, dtype=np.float32)[:, None] * freqs[None, :]).astype(np.float32)
    c, s = np.cos(ang).astype(np.float32), np.sin(ang).astype(np.float32)
    z = np.zeros_like(c)
    cosf = np.concatenate([c, c, z, z], axis=1)
    sin1 = np.concatenate([-s, z, z, z], axis=1)
    sin2 = np.concatenate([z, s, z, z], axis=1)
    return cosf, sin1, sin2


def kernel(x, meta_tokens, norm_mix, w_in, norm_q, w_q_up, norm_kv, w_kv_up, w_attn_proj, ssm_lambda_re, ssm_lambda_im, ssm_log_step, ssm_b_re, ssm_b_im, ssm_c_re, ssm_c_im, ssm_d, w_glu_val, w_glu_gate, w_out, norm_ffn, w_ffn_gate, w_ffn_up, w_ffn_down, norm_final):
    bsz, seq, d = x.shape
    assert d == D_MODEL and w_in.shape[0] == 1 and seq % 512 == 0 and N_META == CHUNK
    t_rows = bsz * seq
    n_chunks = seq // CHUNK
    bj = bsz * n_chunks
    x2 = x.reshape(t_rows, d)
    meta = jnp.pad(meta_tokens.astype(x.dtype), ((0, META_ROWS - N_META), (0, 0)))

    w_all = _w_in_prep(jnp.swapaxes(w_in[0], 0, 1), 256)
    scale = (QK_NOPE + QK_ROPE) ** -0.5 * math.log2(math.e)
    wq = jnp.pad((w_q_up[0] * scale).reshape(Q_LORA, HEADS, QK_NOPE + QK_ROPE),
                 ((0, 0), (0, 0), (0, HEAD_PAD - QK_NOPE - QK_ROPE))).reshape(Q_LORA, HEADS * HEAD_PAD).astype(BF16)
    wkv = w_kv_up[0].astype(BF16)
    g_mix = norm_mix[0].reshape(1, d)
    g_q = norm_q[0].reshape(1, Q_LORA)
    g_kv = norm_kv[0].reshape(1, KV_LORA)
    cosf, sin1, sin2 = _rope_tables(N_META + seq)

    lat, n_mix = _norm_matmul(x2, g_mix, w_all, F32, 512, LAT_W, LAT_W, 0, emit_norm=True)
    u_seq, gates = _ug_proj(n_mix, w_all, 1024, SSM_WIDTH, UG_ROW0 // SSM_WIDTH)
    lat_m = _norm_matmul(meta, g_mix, w_all, F32, META_ROWS, LAT_W, LAT_W, 0)
    u_m = _norm_matmul(meta, g_mix, w_all, BF16, META_ROWS, SSM_WIDTH, SSM_WIDTH, UG_ROW0 // SSM_WIDTH)

    q, k, v = _up_rope(lat, g_q, g_kv, wq, wkv, cosf[N_META:], sin1[N_META:], sin2[N_META:], 512)
    _, k_m, v_m = _up_rope(lat_m, g_q, g_kv, wq, wkv, cosf[:META_ROWS], sin1[:META_ROWS],
                            sin2[:META_ROWS], META_ROWS)
    attn, (w_ap, w_gv, w_gg, w_o, w_fg, w_fu, w_fd) = _attention(
        q, k, v, k_m, v_m,
        [w_attn_proj[0], w_glu_val[0], w_glu_gate[0], w_out[0], w_ffn_gate[0], w_ffn_up[0], w_ffn_down[0]],
        bsz, seq, 512)

    w_sin, ce_lags, t_lags, a1, a2 = _ssm_params(
        ssm_lambda_re[0], ssm_lambda_im[0], ssm_log_step[0], ssm_b_re[0], ssm_b_im[0],
        ssm_c_re[0], ssm_c_im[0], ssm_d[0], 8)
    u_meta = jnp.pad(u_m[:N_META].reshape(1, CHUNK, N_TILE, LANES).transpose(2, 0, 1, 3).reshape(N_TILE, 1, CW),
                     ((0, 0), (0, 15), (0, 0)))
    y = _ssm(u_seq, u_meta, w_sin, a1.reshape(N_TILE, SW), a2.reshape(N_TILE, SW), t_lags, ce_lags,
             bsz, n_chunks)

    mixed = _merge(attn, y, w_ap, w_gv, w_gg, gates, 1024, 1024)
    h1, n_ffn = _out_proj(x2, mixed, w_o, norm_ffn[0].reshape(1, d), 512)
    out = _ffn(n_ffn, h1, w_fg, w_fu, w_fd, norm_final.reshape(1, d), 1024, 512)
    return out.reshape(bsz, seq, d)
```

```python
import functools
import math

import numpy as np
import jax
import jax.numpy as jnp
from jax import lax
from jax.experimental import pallas as pl
from jax.experimental.pallas import tpu as pltpu

F32 = jnp.float32
BF16 = jnp.bfloat16

D_MODEL = 2048
N_META = 16
EPS = 1e-6
HEADS = 16
Q_LORA = 512
KV_LORA = 512
QK_NOPE = 128
QK_ROPE = 64
V_HEAD = 128
ROPE_THETA = 10000.0
SSM_WIDTH = 1024
SSM_P = 16
SSM_P_LOG2 = 4
SSM_G = 64
SSM_N = 64
D_FF = 5632
LANES = 128
SUBLANES = 8
MXU_N = 256
FFN_ACC_COLS = 512
SCAN_UNROLL = 8
HEAD_PAD = 2 * LANES
LAT_W = Q_LORA + KV_LORA + LANES
CHUNK = 16
CP = CHUNK * SSM_P
GPT = LANES // SSM_P
N_TILE = SSM_G // GPT
CW = CHUNK * LANES
SW = GPT * 2 * SSM_N
NT_DIMS = (((1,), (1,)), ((), ()))
META_ROWS = 128
V_PAD = 2 * LANES
TM_LAT = 512
TM_UG, TN_UG = 1024, SSM_WIDTH
TM_UP = 512
TQ_ATTN = 512
TM_MERGE, TN_MERGE = 1024, 1024
TM_OUT = 512
TM_FFN, TF_FFN = 1024, 512
PARAM_GROUPS = 8
VMEM_LIMIT = 56 * 1024 * 1024
FFN_VMEM_LIMIT = 60 * 1024 * 1024


def _cparams(*sem):
    return pltpu.CompilerParams(dimension_semantics=sem, vmem_limit_bytes=VMEM_LIMIT)


def _sigmoid(x):
    return 0.5 + 0.5 * jnp.tanh(0.5 * x)


def _gelu_tanh(x):
    return x * (0.5 * (1.0 + jnp.tanh(math.sqrt(2.0 / math.pi) * (x + 0.044715 * (x * x * x)))))


def _rms(x, g):
    ms = jnp.mean(x * x, axis=-1, keepdims=True)
    return x * lax.rsqrt(ms + EPS) * g


def _norm_matmul_kernel(x_ref, g_ref, w_ref, o_ref, *rest):
    n_ref = rest[-1]
    @pl.when(pl.program_id(1) == 0)
    def _():
        n_ref[...] = _rms(x_ref[...], g_ref[...]).astype(BF16)
        if len(rest) == 2:
            rest[0][...] = n_ref[...]

    n = n_ref[...]
    tn = o_ref.shape[1]
    for c0 in range(0, tn, 2 * MXU_N):
        cols = slice(c0, min(c0 + 2 * MXU_N, tn))
        o_ref[:, cols] = lax.dot_general(n, w_ref[cols, :].astype(BF16), NT_DIMS,
                                         preferred_element_type=F32).astype(o_ref.dtype)


def _norm_matmul(x, g, wt, out_dtype, tm, tn, n, row0, emit_norm=False):
    m, k = x.shape
    out_specs = [pl.BlockSpec((tm, tn), lambda i, j: (i, j))]
    out_shape = [jax.ShapeDtypeStruct((m, n), out_dtype)]
    if emit_norm:
        out_specs.append(pl.BlockSpec((tm, k), lambda i, j: (i, 0)))
        out_shape.append(jax.ShapeDtypeStruct((m, k), BF16))
    return pl.pallas_call(
        _norm_matmul_kernel,
        grid=(m // tm, n // tn),
        in_specs=[
            pl.BlockSpec((tm, k), lambda i, j: (i, 0)),
            pl.BlockSpec((1, k), lambda i, j: (0, 0)),
            pl.BlockSpec((pl.Element(tn), pl.Element(k)),
                         lambda i, j: (pl.multiple_of(row0 + j * tn, SUBLANES), 0)),
        ],
        out_specs=out_specs if emit_norm else out_specs[0],
        out_shape=out_shape if emit_norm else out_shape[0],
        scratch_shapes=[pltpu.VMEM((tm, k), BF16)],
        compiler_params=_cparams("parallel", "arbitrary"),
        name="norm_matmul",
    )(x, g, wt)


def _ug_proj_kernel(n_ref, w_ref, u_ref, gate_ref, stage_ref):
    j = pl.program_id(1)
    tm = n_ref.shape[0]

    @pl.when(j == 0)
    def _():
        n = n_ref[...]
        for c in range(SSM_WIDTH // MXU_N):
            acc = lax.dot_general(n, w_ref[c * MXU_N:(c + 1) * MXU_N, :].astype(BF16), NT_DIMS,
                                  preferred_element_type=F32)
            for s in range(MXU_N // LANES):
                t = c * (MXU_N // LANES) + s
                stage_ref[t] = acc[:, s * LANES:(s + 1) * LANES]
                for sig in range(CHUNK):
                    u_ref[t, :, sig * LANES:(sig + 1) * LANES] = (
                        stage_ref[t, pl.ds(sig, tm // CHUNK, stride=CHUNK), :].astype(u_ref.dtype))

    @pl.when(j > 0)
    def _():
        n = n_ref[...]
        tn = gate_ref.shape[1]
        for c0 in range(0, tn, 2 * MXU_N):
            cols = slice(c0, c0 + 2 * MXU_N)
            gate_ref[:, cols] = lax.dot_general(n, w_ref[cols, :].astype(BF16), NT_DIMS,
                                                preferred_element_type=F32).astype(gate_ref.dtype)


def _ug_proj(n, wt, tm, tn, row0):
    m, k = n.shape
    n_gate = 2 * D_MODEL
    assert tn == SSM_WIDTH
    return pl.pallas_call(
        _ug_proj_kernel,
        grid=(m // tm, 1 + n_gate // tn),
        in_specs=[
            pl.BlockSpec((tm, k), lambda i, j: (i, 0)),
            pl.BlockSpec((pl.Element(tn), pl.Element(k)),
                         lambda i, j: (pl.multiple_of(row0 + j * tn, SUBLANES), 0)),
        ],
        out_specs=[
            pl.BlockSpec((N_TILE, tm // CHUNK, CW), lambda i, j: (0, i, 0)),
            pl.BlockSpec((tm, tn), lambda i, j: (i, jnp.maximum(j - 1, 0))),
        ],
        out_shape=[jax.ShapeDtypeStruct((N_TILE, m // CHUNK, CW), BF16),
                   jax.ShapeDtypeStruct((m, n_gate), BF16)],
        scratch_shapes=[pltpu.VMEM((N_TILE, tm, LANES), F32)],
        compiler_params=_cparams("parallel", "arbitrary"),
        name="ug_proj",
    )(n, wt)


def _rope_slab(x, cosf, sin1, sin2):
    return (x * cosf + pltpu.roll(x, LANES - QK_ROPE // 2, 1) * sin1
            + pltpu.roll(x, QK_ROPE // 2, 1) * sin2)


def _up_rope_kernel(ql_ref, kvl_ref, kr_ref, gq_ref, gkv_ref, wq_ref, wkv_ref,
                    cos_ref, sin1_ref, sin2_ref, q_ref, kt_ref, v_ref):
    cosf, sin1, sin2 = cos_ref[...], sin1_ref[...], sin2_ref[...]
    qn = _rms(ql_ref[...], gq_ref[...]).astype(BF16)
    kvn = _rms(kvl_ref[...], gkv_ref[...]).astype(BF16)
    k_rope_t = _rope_slab(kr_ref[...], cosf, sin1, sin2).T.astype(BF16)
    ones_cols = (lax.broadcasted_iota(jnp.int32, (ql_ref.shape[0], V_PAD - V_HEAD), 1) == 0
                 ).astype(F32).astype(BF16)
    for h in range(HEADS):
        acc = jnp.dot(qn, wq_ref[:, h * HEAD_PAD:(h + 1) * HEAD_PAD], preferred_element_type=F32)
        q_ref[h, :, :LANES] = acc[:, :LANES].astype(BF16)
        q_ref[h, :, LANES:] = _rope_slab(acc[:, LANES:], cosf, sin1, sin2).astype(BF16)
    kv_w = QK_NOPE + V_HEAD
    for h in range(HEADS):
        acc = jnp.dot(kvn, wkv_ref[:, h * kv_w:(h + 1) * kv_w], preferred_element_type=F32)
        kt_ref[h, :LANES, :] = acc[:, :QK_NOPE].T.astype(BF16)
        kt_ref[h, LANES:, :] = k_rope_t
        v_ref[h, :, :V_HEAD] = acc[:, QK_NOPE:].astype(BF16)
        v_ref[h, :, V_HEAD:] = ones_cols


def _up_rope(lat, gq, gkv, wq, wkv, cosf, sin1, sin2, tm):
    m = lat.shape[0]
    n_pos_blocks = cosf.shape[0] // tm
    row = lambda i: (i, 0)
    const = lambda i: (0, 0)
    pos = lambda i: (i % n_pos_blocks, 0)
    head_major = lambda i: (0, i, 0)
    return pl.pallas_call(
        _up_rope_kernel,
        grid=(m // tm,),
        in_specs=[
            pl.BlockSpec((tm, Q_LORA), row),
            pl.BlockSpec((tm, KV_LORA), lambda i: (i, 1)),
            pl.BlockSpec((tm, LANES), lambda i: (i, (Q_LORA + KV_LORA) // LANES)),
            pl.BlockSpec((1, Q_LORA), const),
            pl.BlockSpec((1, KV_LORA), const),
            pl.BlockSpec(wq.shape, const),
            pl.BlockSpec(wkv.shape, const),
            pl.BlockSpec((tm, LANES), pos),
            pl.BlockSpec((tm, LANES), pos),
            pl.BlockSpec((tm, LANES), pos),
        ],
        out_specs=[
            pl.BlockSpec((HEADS, tm, HEAD_PAD), head_major),
            pl.BlockSpec((HEADS, HEAD_PAD, tm), lambda i: (0, 0, i)),
            pl.BlockSpec((HEADS, tm, V_PAD), head_major),
        ],
        out_shape=[
            jax.ShapeDtypeStruct((HEADS, m, HEAD_PAD), BF16),
            jax.ShapeDtypeStruct((HEADS, HEAD_PAD, m), BF16),
            jax.ShapeDtypeStruct((HEADS, m, V_PAD), BF16),
        ],
        compiler_params=_cparams("parallel"),
        name="up_rope",
    )(lat, lat, lat, gq, gkv, wq, wkv, cosf, sin1, sin2)


def _attn_kernel(qi_ref, ki_ref, q_ref, kt_ref, v_ref, kmt_ref, vm_ref, *refs):
    n_side = (len(refs) - 3) // 2
    side_in, o_ref, side_out = refs[:n_side], refs[n_side], refs[n_side + 1:2 * n_side + 1]
    m_ref, acc_ref = refs[2 * n_side + 1:]
    for w_ref, wb_ref in zip(side_in, side_out):
        wb_ref[...] = w_ref[...].astype(wb_ref.dtype)
    t = pl.program_id(1)
    qi = qi_ref[t]
    ki = ki_ref[t]

    def lane_tiled(x, width):
        return jnp.concatenate([x] * (width // LANES), axis=1)

    @pl.when(ki == 0)
    def _init():
        def body(h, c):
            s = jnp.dot(q_ref[h], kmt_ref[h], preferred_element_type=F32)
            key = lax.broadcasted_iota(jnp.int32, s.shape, 1)
            s = jnp.where(key < N_META, s, -jnp.inf)
            m = jnp.broadcast_to(jnp.max(s, axis=-1, keepdims=True), s.shape)
            p = jnp.exp2(s - m)
            m_ref[h] = m
            acc_ref[h] = jnp.dot(p.astype(BF16), vm_ref[h], preferred_element_type=F32)
            return c
        lax.fori_loop(0, HEADS, body, 0, unroll=2)

    def tile(masked):
        def scores(h):
            return jnp.dot(q_ref[h], kt_ref[h], preferred_element_type=F32)

        def body(h, s):
            if masked:
                qry = lax.broadcasted_iota(jnp.int32, s.shape, 0)
                key = lax.broadcasted_iota(jnp.int32, s.shape, 1)
                s = jnp.where(key <= qry, s, -jnp.inf)
            m_prev = m_ref[h]
            m_new = jnp.maximum(m_prev, jnp.max(s, axis=-1, keepdims=True))
            alpha = jnp.exp2(m_prev - m_new)
            p = jnp.exp2(s - lane_tiled(m_new, s.shape[1]))
            acc_ref[h] = lane_tiled(alpha, V_PAD) * acc_ref[h] + jnp.dot(
                p.astype(BF16), v_ref[h], preferred_element_type=F32)
            m_ref[h] = m_new

        s_next = scores(0)
        for h in range(HEADS):
            s_cur = s_next
            if h + 1 < HEADS:
                s_next = scores(h + 1)
            body(h, s_cur)

    @pl.when(ki < qi)
    def _full():
        tile(False)

    @pl.when(ki == qi)
    def _diag():
        tile(True)
        for h in range(HEADS):
            out = acc_ref[h, :, :V_HEAD] / acc_ref[h, :, V_HEAD:V_HEAD + 1]
            o_ref[:, h * V_HEAD:(h + 1) * V_HEAD] = out.astype(o_ref.dtype)


def _attention(q, k, v, km, vm, side_weights, bsz, seq, tq):
    nq = seq // tq
    pairs = [(a, b) for a in range(nq) for b in range(a + 1)]
    qi_arr = jnp.asarray([p[0] for p in pairs], jnp.int32)
    ki_arr = jnp.asarray([p[1] for p in pairs], jnp.int32)
    n_steps = bsz * len(pairs)
    side_specs = []
    for w in side_weights:
        tiles = w.shape[0] // 16
        n_blk = max(nb for nb in range(1, n_steps + 1) if tiles % nb == 0)
        spec = pl.BlockSpec((w.shape[0] // n_blk, w.shape[1]),
                            lambda b, t, qi, ki, n_blk=n_blk: (jnp.minimum(b * len(pairs) + t, n_blk - 1), 0))
        side_specs.append(spec)
    grid_spec = pltpu.PrefetchScalarGridSpec(
        num_scalar_prefetch=2,
        grid=(bsz, len(pairs)),
        in_specs=[
            pl.BlockSpec((HEADS, tq, HEAD_PAD), lambda b, t, qi, ki: (0, b * nq + qi[t], 0)),
            pl.BlockSpec((HEADS, HEAD_PAD, tq), lambda b, t, qi, ki: (0, 0, b * nq + ki[t])),
            pl.BlockSpec((HEADS, tq, V_PAD), lambda b, t, qi, ki: (0, b * nq + ki[t], 0)),
            pl.BlockSpec(km.shape, lambda b, t, qi, ki: (0, 0, 0)),
            pl.BlockSpec(vm.shape, lambda b, t, qi, ki: (0, 0, 0)),
        ] + side_specs,
        out_specs=[pl.BlockSpec((tq, HEADS * V_HEAD), lambda b, t, qi, ki: (b * nq + qi[t], 0))] + side_specs,
        scratch_shapes=[
            pltpu.VMEM((HEADS, tq, LANES), F32),
            pltpu.VMEM((HEADS, tq, V_PAD), F32),
        ],
    )
    outs = pl.pallas_call(
        _attn_kernel,
        grid_spec=grid_spec,
        out_shape=[jax.ShapeDtypeStruct((bsz * seq, HEADS * V_HEAD), BF16)]
        + [jax.ShapeDtypeStruct(w.shape, BF16) for w in side_weights],
        compiler_params=_cparams("arbitrary", "arbitrary"),
        name="flash_attn",
    )(qi_arr, ki_arr, q, k, v, km, vm, *side_weights)
    return outs[0], outs[1:]


def _ssm_param_kernel(lr_ref, li_ref, ls_ref, bre_ref, bim_ref, cre_ref, cim_ref, d_ref,
                      win_ref, cen_ref, tl_ref, a1_ref, a2_ref):
    lane = lax.broadcasted_iota(jnp.int32, (SSM_P, LANES), 1)
    row = lax.broadcasted_iota(jnp.int32, (SSM_P, LANES), 0)
    on_diag = jnp.bitwise_and(lane, SSM_P - 1) == row
    cat = lambda x, y: jnp.concatenate([x, y], axis=1)
    lr_all, li_all = lr_ref[0], li_ref[0]
    step_all = jnp.exp(ls_ref[0])
    mag = jnp.exp(lr_all * step_all)
    ab_re, ab_im = mag * jnp.cos(li_all * step_all), mag * jnp.sin(li_all * step_all)
    den = lr_all * lr_all + li_all * li_all
    nr, ni = ab_re - 1.0, ab_im
    coef_re_all = (nr * lr_all + ni * li_all) / den
    coef_im_all = (ni * lr_all - nr * li_all) / den
    for g in range(bre_ref.shape[0]):
        a_re, a_im = ab_re[g:g + 1], ab_im[g:g + 1]
        coef_re, coef_im = coef_re_all[g:g + 1], coef_im_all[g:g + 1]
        bre, bim = bre_ref[g], bim_ref[g]
        bb_re = coef_re * bre - coef_im * bim
        bb_im = coef_re * bim + coef_im * bre
        cre, cim = cre_ref[g], cim_ref[g]
        ce_re_lags, ce_im_lags = [], []
        er, ei = jnp.ones_like(a_re), jnp.zeros_like(a_im)
        for k in range(CHUNK + 1):
            if k == CHUNK:
                a1_ref[g] = cat(er, er)
                a2_ref[g] = cat(-ei, ei)
            ce_re = cre * er - cim * ei
            ce_im = cre * ei + cim * er
            if k >= 1:
                cen_ref[g, k - 1] = cat(ce_re, -ce_im).astype(cen_ref.dtype)
            if k < CHUNK:
                ce_re_lags.append(ce_re)
                ce_im_lags.append(ce_im)
                r0 = (CHUNK - 1 - k) * SSM_P
                win_ref[g, r0:r0 + SSM_P, :] = cat(er * bb_re - ei * bb_im,
                                                   er * bb_im + ei * bb_re).astype(win_ref.dtype)
            er, ei = er * a_re - ei * a_im, er * a_im + ei * a_re
        tl = (lax.dot_general(jnp.concatenate(ce_re_lags, axis=0), jnp.concatenate([bb_re] * GPT, axis=0),
                              NT_DIMS, precision=lax.Precision.HIGHEST, preferred_element_type=F32)
              - lax.dot_general(jnp.concatenate(ce_im_lags, axis=0), jnp.concatenate([bb_im] * GPT, axis=0),
                                NT_DIMS, precision=lax.Precision.HIGHEST, preferred_element_type=F32))
        tl_ref[g] = tl
        tl_ref[g, :SSM_P, :] = tl[:SSM_P] + jnp.where(on_diag, d_ref[g], 0.0)


def _ssm_params(lam_re, lam_im, log_step, b_re, b_im, c_re, c_im, d, gb):
    g = SSM_G
    row = lambda shape: pl.BlockSpec((gb,) + shape, lambda i: (i,) + (0,) * len(shape))
    step_blk = lambda width: pl.BlockSpec((1, gb, width), lambda i: (i, 0, 0))
    outs = [
        ((g, CP, 2 * SSM_N), (CP, 2 * SSM_N), BF16),
        ((g, CHUNK, SSM_P, 2 * SSM_N), (CHUNK, SSM_P, 2 * SSM_N), BF16),
        ((g, CP, LANES), (CP, LANES), F32),
        ((g, 1, 2 * SSM_N), (1, 2 * SSM_N), F32),
        ((g, 1, 2 * SSM_N), (1, 2 * SSM_N), F32),
    ]
    return pl.pallas_call(
        _ssm_param_kernel,
        grid=(g // gb,),
        in_specs=[step_blk(SSM_N), step_blk(SSM_N), step_blk(1),
                  row((SSM_P, SSM_N)), row((SSM_P, SSM_N)), row((SSM_P, SSM_N)), row((SSM_P, SSM_N)),
                  row((SSM_P, 1))],
        out_specs=[row(blk) for _, blk, _ in outs],
        out_shape=[jax.ShapeDtypeStruct(full, dt) for full, _, dt in outs],
        compiler_params=_cparams("parallel"),
        name="ssm_params",
    )(lam_re.reshape(g // gb, gb, SSM_N), lam_im.reshape(g // gb, gb, SSM_N), log_step.reshape(g // gb, gb, 1),
      jnp.swapaxes(b_re, 1, 2), jnp.swapaxes(b_im, 1, 2), c_re, c_im, d.reshape(g, SSM_P, 1))


def _ssm_kernel(u_ref, um_ref, wc_ref, a1_ref, a2_ref, tl_ref, cen_ref, y_ref,
                s_ref, s0_ref, win_ref, wt_ref, ws_ref, *, bsz, n_chunks):
    i = pl.program_id(0)
    rows = bsz * n_chunks

    @pl.when(i < N_TILE)
    def _state_inputs():
        @pl.when(i == 0)
        def _():
            win_ref[...] = jnp.zeros(win_ref.shape, win_ref.dtype)

        for sig in range(CHUNK):
            for g in range(GPT):
                r0 = sig * LANES + g * SSM_P
                win_ref[r0:r0 + SSM_P, g * 2 * SSM_N:(g + 1) * 2 * SSM_N] = (
                    wc_ref[g, sig * SSM_P:(sig + 1) * SSM_P, :])
        v = jnp.dot(u_ref[0], win_ref[...], preferred_element_type=F32)
        v_meta = jnp.dot(um_ref[0], win_ref[...], preferred_element_type=F32)
        for g in range(GPT):
            s_ref[g, pl.ds(i, rows, stride=N_TILE), :] = v[:, g * 2 * SSM_N:(g + 1) * 2 * SSM_N]
            s0_ref[g, pl.ds(i, 1), :] = v_meta[0:1, g * 2 * SSM_N:(g + 1) * 2 * SSM_N]

    @pl.when(i == N_TILE - 1)
    def _recurrence():
        a1 = [a1_ref[:, g * 2 * SSM_N:(g + 1) * 2 * SSM_N] for g in range(GPT)]
        a2 = [a2_ref[:, g * 2 * SSM_N:(g + 1) * 2 * SSM_N] for g in range(GPT)]
        for b in range(bsz):
            def body(j, carry):
                r = pl.multiple_of((b * n_chunks + j) * N_TILE, N_TILE)
                nxt = []
                for g in range(GPT):
                    s, sw = carry[2 * g], carry[2 * g + 1]
                    v = s_ref[g, pl.ds(r, N_TILE), :]
                    vw = pltpu.roll(v, SSM_N, 1)
                    s_ref[g, pl.ds(r, N_TILE), :] = s
                    nxt.append(a1[g] * s + a2[g] * sw + v)
                    nxt.append(a1[g] * sw - a2[g] * s + vw)
                return tuple(nxt)
            init = []
            for g in range(GPT):
                init += [s0_ref[g], pltpu.roll(s0_ref[g], SSM_N, 1)]
            lax.fori_loop(0, n_chunks, body, tuple(init), unroll=SCAN_UNROLL)

    @pl.when(i >= N_TILE)
    def _outputs():
        tile = i - N_TILE
        row_g = lax.shift_right_logical(lax.broadcasted_iota(jnp.int32, (LANES, LANES), 0), SSM_P_LOG2)
        col_g = lax.shift_right_logical(lax.broadcasted_iota(jnp.int32, (LANES, LANES), 1), SSM_P_LOG2)
        same_group = row_g == col_g
        @pl.when(i == N_TILE)
        def _():
            wt_ref[...] = jnp.zeros(wt_ref.shape, wt_ref.dtype)
            ws_ref[...] = jnp.zeros(ws_ref.shape, ws_ref.dtype)

        for k in range(CHUNK):
            rep = jnp.concatenate([tl_ref[g, k * SSM_P:(k + 1) * SSM_P, :] for g in range(GPT)], axis=0)
            blk = jnp.where(same_group, rep, 0.0).astype(BF16)
            for sig in range(CHUNK - k):
                tau = sig + k
                wt_ref[tau * LANES:(tau + 1) * LANES, sig * LANES:(sig + 1) * LANES] = blk
        for g in range(GPT):
            for tau in range(CHUNK):
                r0 = tau * LANES + g * SSM_P
                ws_ref[r0:r0 + SSM_P, g * 2 * SSM_N:(g + 1) * 2 * SSM_N] = cen_ref[g, tau]
        states = jnp.concatenate(
            [s_ref[g, pl.ds(tile, rows, stride=N_TILE), :] for g in range(GPT)], axis=1).astype(BF16)
        u = u_ref[0]
        steps_per_tile = MXU_N // LANES
        for c in range(CW // MXU_N):
            k_hi = (c + 1) * MXU_N
            out_rows = slice(c * MXU_N, k_hi)
            y = _gelu_tanh(
                lax.dot_general(u[:, :k_hi], wt_ref[out_rows, :k_hi], NT_DIMS, preferred_element_type=F32)
                + lax.dot_general(states, ws_ref[out_rows, :], NT_DIMS, preferred_element_type=F32))
            for s in range(steps_per_tile):
                tau = c * steps_per_tile + s
                y_ref[pl.ds(tau, rows, stride=CHUNK), :] = y[:, s * LANES:(s + 1) * LANES]


def _ssm(u, u_meta, w_compact, a1, a2, t_lags, ce_lags, bsz, n_chunks):
    rows = bsz * n_chunks
    in_tile = lambda i: (lax.rem(i, N_TILE), 0, 0)
    out_tile = lambda i: (jnp.maximum(i - N_TILE, 0), 0, 0)
    out_tile4 = lambda i: (jnp.maximum(i - N_TILE, 0), 0, 0, 0)
    whole = lambda i: (0, 0)
    return pl.pallas_call(
        functools.partial(_ssm_kernel, bsz=bsz, n_chunks=n_chunks),
        grid=(2 * N_TILE,),
        in_specs=[pl.BlockSpec((1, rows, CW), in_tile),
                  pl.BlockSpec((1,) + u_meta.shape[1:], in_tile),
                  pl.BlockSpec((GPT, CP, 2 * SSM_N), in_tile),
                  pl.BlockSpec((N_TILE, SW), whole),
                  pl.BlockSpec((N_TILE, SW), whole),
                  pl.BlockSpec((GPT, CP, LANES), out_tile),
                  pl.BlockSpec((GPT, CHUNK, SSM_P, 2 * SSM_N), out_tile4)],
        out_specs=pl.BlockSpec((rows * CHUNK, LANES), lambda i: (0, jnp.maximum(i - N_TILE, 0))),
        out_shape=jax.ShapeDtypeStruct((rows * CHUNK, SSM_WIDTH), F32),
        scratch_shapes=[pltpu.VMEM((GPT, rows * N_TILE, 2 * SSM_N), F32),
                        pltpu.VMEM((GPT, N_TILE, 2 * SSM_N), F32),
                        pltpu.VMEM((CW, SW), BF16),
                        pltpu.VMEM((CW, CW), BF16),
                        pltpu.VMEM((CW, SW), BF16)],
        compiler_params=_cparams("arbitrary"),
        name="ssm",
    )(u, u_meta, w_compact, a1, a2, t_lags, ce_lags)


def _merge_kernel(attn_ref, y_ref, wap_ref, wv_ref, wg_ref, g0_ref, g1_ref, o_ref):
    attn = attn_ref[...]
    y = y_ref[...].astype(BF16)
    for c in range(o_ref.shape[1] // MXU_N):
        cols = slice(c * MXU_N, (c + 1) * MXU_N)
        a = jnp.dot(attn, wap_ref[:, cols], preferred_element_type=F32)
        ssm = (jnp.dot(y, wv_ref[:, cols], preferred_element_type=F32)
               * _sigmoid(jnp.dot(y, wg_ref[:, cols], preferred_element_type=F32)))
        mixed = (_sigmoid(g0_ref[:, cols].astype(F32)) * a
                 + _sigmoid(g1_ref[:, cols].astype(F32)) * ssm)
        o_ref[:, cols] = mixed.astype(o_ref.dtype)


def _merge(attn, y, wap, wv, wg, gates, tm, tn):
    m = attn.shape[0]
    g1_off = D_MODEL // tn
    return pl.pallas_call(
        _merge_kernel,
        grid=(m // tm, D_MODEL // tn),
        in_specs=[
            pl.BlockSpec((tm, attn.shape[1]), lambda i, j: (i, 0)),
            pl.BlockSpec((tm, SSM_WIDTH), lambda i, j: (i, 0)),
            pl.BlockSpec((attn.shape[1], tn), lambda i, j: (0, j)),
            pl.BlockSpec((SSM_WIDTH, tn), lambda i, j: (0, j)),
            pl.BlockSpec((SSM_WIDTH, tn), lambda i, j: (0, j)),
            pl.BlockSpec((tm, tn), lambda i, j: (i, j)),
            pl.BlockSpec((tm, tn), lambda i, j: (i, g1_off + j)),
        ],
        out_specs=pl.BlockSpec((tm, tn), lambda i, j: (i, j)),
        out_shape=jax.ShapeDtypeStruct((m, D_MODEL), BF16),
        compiler_params=_cparams("parallel", "arbitrary"),
        name="gated_merge",
    )(attn, y, wap, wv, wg, gates, gates)


def _out_proj_kernel(x_ref, mx_ref, w_ref, g_ref, h_ref, n_ref):
    mx = mx_ref[...]
    d = h_ref.shape[1]
    sq = None
    for c in range(d // MXU_N):
        cols = slice(c * MXU_N, (c + 1) * MXU_N)
        h = x_ref[:, cols] + jnp.dot(mx, w_ref[:, cols], preferred_element_type=F32)
        h_ref[:, cols] = h
        part = jnp.sum(h * h, axis=-1, keepdims=True)
        sq = part if sq is None else sq + part
    inv = lax.rsqrt(sq * (1.0 / d) + EPS)
    n_ref[...] = (h_ref[...] * inv * g_ref[...]).astype(n_ref.dtype)


def _out_proj(x, mixed, w, g, tm):
    m, d = x.shape
    row = pl.BlockSpec((tm, d), lambda i: (i, 0))
    return pl.pallas_call(
        _out_proj_kernel,
        grid=(m // tm,),
        in_specs=[row, row, pl.BlockSpec((d, d), lambda i: (0, 0)), pl.BlockSpec((1, d), lambda i: (0, 0))],
        out_specs=[row, row],
        out_shape=[jax.ShapeDtypeStruct((m, d), F32), jax.ShapeDtypeStruct((m, d), BF16)],
        compiler_params=_cparams("parallel"),
        name="out_proj",
    )(x, mixed, w, g)


def _ffn_kernel(n_ref, h_hbm, wg_ref, wu_ref, wd_ref, gf_ref, o_ref, h_buf, h_sem):
    i = pl.program_id(0)
    f = pl.program_id(1)
    tm = n_ref.shape[0]
    tf = wg_ref.shape[1]
    d = wd_ref.shape[1]

    def residual_copy():
        rows = pl.ds(pl.multiple_of(i * tm, tm), tm)
        return pltpu.make_async_copy(h_hbm.at[rows], h_buf, h_sem)

    @pl.when(f == 0)
    def _():
        residual_copy().start()
        o_ref[...] = jnp.zeros(o_ref.shape, o_ref.dtype)

    n = n_ref[...]
    acts = []
    for c in range(tf // MXU_N):
        cols = slice(c * MXU_N, (c + 1) * MXU_N)
        gate = jnp.dot(n, wg_ref[:, cols], preferred_element_type=F32)
        up = jnp.dot(n, wu_ref[:, cols], preferred_element_type=F32)
        acts.append((gate * _sigmoid(gate) * up).astype(BF16))

    for c in range(d // FFN_ACC_COLS):
        cols = slice(c * FFN_ACC_COLS, (c + 1) * FFN_ACC_COLS)
        part = None
        for k, a in enumerate(acts):
            term = jnp.dot(a, wd_ref[k * MXU_N:(k + 1) * MXU_N, cols], preferred_element_type=F32)
            part = term if part is None else part + term
        o_ref[:, cols] += part

    @pl.when(f == pl.num_programs(1) - 1)
    def _():
        residual_copy().wait()
        o_ref[...] = _rms(h_buf[...] + o_ref[...], gf_ref[...])


def _ffn(n, h, wg, wu, wd, gf, tm, tf):
    m, d = h.shape
    row = lambda i, f: (i, 0)
    return pl.pallas_call(
        _ffn_kernel,
        grid=(m // tm, D_FF // tf),
        in_specs=[
            pl.BlockSpec((tm, d), row),
            pl.BlockSpec(memory_space=pl.ANY),
            pl.BlockSpec((d, tf), lambda i, f: (0, f)),
            pl.BlockSpec((d, tf), lambda i, f: (0, f)),
            pl.BlockSpec((tf, d), lambda i, f: (f, 0)),
            pl.BlockSpec((1, d), lambda i, f: (0, 0)),
        ],
        out_specs=pl.BlockSpec((tm, d), row),
        out_shape=jax.ShapeDtypeStruct((m, d), F32),
        scratch_shapes=[pltpu.VMEM((tm, d), F32), pltpu.SemaphoreType.DMA(())],
        compiler_params=pltpu.CompilerParams(dimension_semantics=("arbitrary", "arbitrary"),
                                             vmem_limit_bytes=FFN_VMEM_LIMIT),
        name="ffn",
    )(n, h, wg, wu, wd, gf)


def _rope_tables(length):
    half = QK_ROPE // 2
    freqs = np.float32(ROPE_THETA) ** (-np.arange(half, dtype=np.float32) / np.float32(half))
    ang = (np.arange(length, dtype=np.float32)[:, None] * freqs[None, :]).astype(np.float32)
    c, s = np.cos(ang).astype(np.float32), np.sin(ang).astype(np.float32)
    z = np.zeros_like(c)
    cosf = np.concatenate([c, c, z, z], axis=1)
    sin1 = np.concatenate([-s, z, z, z], axis=1)
    sin2 = np.concatenate([z, s, z, z], axis=1)
    return cosf, sin1, sin2


def kernel(x, meta_tokens, norm_mix, w_in, norm_q, w_q_up, norm_kv, w_kv_up, w_attn_proj, ssm_lambda_re, ssm_lambda_im, ssm_log_step, ssm_b_re, ssm_b_im, ssm_c_re, ssm_c_im, ssm_d, w_glu_val, w_glu_gate, w_out, norm_ffn, w_ffn_gate, w_ffn_up, w_ffn_down, norm_final):
    bsz, seq, d = x.shape
    assert d == D_MODEL and w_in.shape[0] == 1 and (bsz * seq) % TM_FFN == 0 and seq % TQ_ATTN == 0 and N_META == CHUNK
    t_rows = bsz * seq
    n_chunks = seq // CHUNK
    bj = bsz * n_chunks
    x2 = x.reshape(t_rows, d)
    meta = jnp.pad(meta_tokens.astype(x.dtype), ((0, META_ROWS - N_META), (0, 0)))

    w_in_t = jnp.swapaxes(w_in[0], 0, 1)
    off_u = Q_LORA + KV_LORA + QK_ROPE
    scale = (QK_NOPE + QK_ROPE) ** -0.5 * math.log2(math.e)
    wq = jnp.pad((w_q_up[0] * scale).reshape(Q_LORA, HEADS, QK_NOPE + QK_ROPE),
                 ((0, 0), (0, 0), (0, HEAD_PAD - QK_NOPE - QK_ROPE))).reshape(Q_LORA, HEADS * HEAD_PAD).astype(BF16)
    wkv = w_kv_up[0].astype(BF16)
    g_mix = norm_mix[0].reshape(1, d)
    g_q = norm_q[0].reshape(1, Q_LORA)
    g_kv = norm_kv[0].reshape(1, KV_LORA)
    cosf, sin1, sin2 = _rope_tables(N_META + seq)

    lat, n_mix = _norm_matmul(x2, g_mix, w_in_t, F32, TM_LAT, LAT_W, LAT_W, 0, emit_norm=True)
    u_seq, gates = _ug_proj(n_mix, w_in_t, TM_UG, TN_UG, off_u)
    lat_m = _norm_matmul(meta, g_mix, w_in_t, F32, META_ROWS, LAT_W, LAT_W, 0)
    u_m = _norm_matmul(meta, g_mix, w_in_t, BF16, META_ROWS, SSM_WIDTH, SSM_WIDTH, off_u)

    q, k, v = _up_rope(lat, g_q, g_kv, wq, wkv, cosf[N_META:], sin1[N_META:], sin2[N_META:], TM_UP)
    _, k_m, v_m = _up_rope(lat_m, g_q, g_kv, wq, wkv, cosf[:META_ROWS], sin1[:META_ROWS],
                            sin2[:META_ROWS], META_ROWS)
    attn, (w_ap, w_gv, w_gg, w_o, w_fg, w_fu, w_fd) = _attention(
        q, k, v, k_m, v_m,
        [w_attn_proj[0], w_glu_val[0], w_glu_gate[0], w_out[0], w_ffn_gate[0], w_ffn_up[0], w_ffn_down[0]],
        bsz, seq, TQ_ATTN)

    w_sin, ce_lags, t_lags, a1, a2 = _ssm_params(
        ssm_lambda_re[0], ssm_lambda_im[0], ssm_log_step[0], ssm_b_re[0], ssm_b_im[0],
        ssm_c_re[0], ssm_c_im[0], ssm_d[0], PARAM_GROUPS)
    u_meta = jnp.pad(u_m[:N_META].reshape(1, CHUNK, N_TILE, LANES).transpose(2, 0, 1, 3).reshape(N_TILE, 1, CW),
                     ((0, 0), (0, 15), (0, 0)))
    y = _ssm(u_seq, u_meta, w_sin, a1.reshape(N_TILE, SW), a2.reshape(N_TILE, SW), t_lags, ce_lags,
             bsz, n_chunks)

    mixed = _merge(attn, y, w_ap, w_gv, w_gg, gates, TM_MERGE, TN_MERGE)
    h1, n_ffn = _out_proj(x2, mixed, w_o, norm_ffn[0].reshape(1, d), TM_OUT)
    out = _ffn(n_ffn, h1, w_fg, w_fu, w_fd, norm_final.reshape(1, d), TM_FFN, TF_FFN)
    return out.reshape(bsz, seq, d)
```

```python
import functools
import math

import numpy as np
import jax
import jax.numpy as jnp
from jax import lax
from jax.experimental import pallas as pl
from jax.experimental.pallas import tpu as pltpu

F32 = jnp.float32
BF16 = jnp.bfloat16

D_MODEL = 2048
N_META = 16
EPS = 1e-6
HEADS = 16
Q_LORA = 512
KV_LORA = 512
QK_NOPE = 128
QK_ROPE = 64
V_HEAD = 128
ROPE_THETA = 10000.0
SSM_WIDTH = 1024
SSM_P = 16
SSM_P_LOG2 = 4
SSM_G = 64
SSM_N = 64
D_FF = 5632
LANES = 128
SUBLANES = 8
MXU_N = 256
FFN_ACC_COLS = 512
SCAN_UNROLL = 8
HEAD_PAD = 2 * LANES
LAT_W = Q_LORA + KV_LORA + LANES
CHUNK = 16
CP = CHUNK * SSM_P
GPT = LANES // SSM_P
N_TILE = SSM_G // GPT
CW = CHUNK * LANES
SW = GPT * 2 * SSM_N
NT_DIMS = (((1,), (1,)), ((), ()))
META_ROWS = 128
V_PAD = 2 * LANES
TM_LAT = 512
TM_UG, TN_UG = 1024, SSM_WIDTH
TM_UP = 512
TQ_ATTN = 512
TM_MERGE, TN_MERGE = 1024, 1024
TM_OUT = 512
TM_FFN, TF_FFN = 1024, 512
PARAM_GROUPS = 8
VMEM_LIMIT = 56 * 1024 * 1024
FFN_VMEM_LIMIT = 60 * 1024 * 1024


def _cparams(*sem):
    return pltpu.CompilerParams(dimension_semantics=sem, vmem_limit_bytes=VMEM_LIMIT)


def _sigmoid(x):
    return 0.5 + 0.5 * jnp.tanh(0.5 * x)


def _gelu_tanh(x):
    return x * (0.5 * (1.0 + jnp.tanh(math.sqrt(2.0 / math.pi) * (x + 0.044715 * (x * x * x)))))


def _rms(x, g):
    ms = jnp.mean(x * x, axis=-1, keepdims=True)
    return x * lax.rsqrt(ms + EPS) * g


def _norm_matmul_kernel(x_ref, g_ref, w_ref, o_ref, *rest):
    n_ref = rest[-1]
    @pl.when(pl.program_id(1) == 0)
    def _():
        n_ref[...] = _rms(x_ref[...], g_ref[...]).astype(BF16)
        if len(rest) == 2:
            rest[0][...] = n_ref[...]

    n = n_ref[...]
    tn = o_ref.shape[1]
    for c0 in range(0, tn, 2 * MXU_N):
        cols = slice(c0, min(c0 + 2 * MXU_N, tn))
        o_ref[:, cols] = lax.dot_general(n, w_ref[cols, :].astype(BF16), NT_DIMS,
                                         preferred_element_type=F32).astype(o_ref.dtype)


def _norm_matmul(x, g, wt, out_dtype, tm, tn, n, row0, emit_norm=False):
    m, k = x.shape
    out_specs = [pl.BlockSpec((tm, tn), lambda i, j: (i, j))]
    out_shape = [jax.ShapeDtypeStruct((m, n), out_dtype)]
    if emit_norm:
        out_specs.append(pl.BlockSpec((tm, k), lambda i, j: (i, 0)))
        out_shape.append(jax.ShapeDtypeStruct((m, k), BF16))
    return pl.pallas_call(
        _norm_matmul_kernel,
        grid=(m // tm, n // tn),
        in_specs=[
            pl.BlockSpec((tm, k), lambda i, j: (i, 0)),
            pl.BlockSpec((1, k), lambda i, j: (0, 0)),
            pl.BlockSpec((pl.Element(tn), pl.Element(k)),
                         lambda i, j: (pl.multiple_of(row0 + j * tn, SUBLANES), 0)),
        ],
        out_specs=out_specs if emit_norm else out_specs[0],
        out_shape=out_shape if emit_norm else out_shape[0],
        scratch_shapes=[pltpu.VMEM((tm, k), BF16)],
        compiler_params=_cparams("parallel", "arbitrary"),
        name="norm_matmul",
    )(x, g, wt)


def _ug_proj_kernel(n_ref, w_ref, u_ref, gate_ref, stage_ref):
    j = pl.program_id(1)
    tm = n_ref.shape[0]

    @pl.when(j == 0)
    def _():
        n = n_ref[...]
        for c in range(SSM_WIDTH // MXU_N):
            acc = lax.dot_general(n, w_ref[c * MXU_N:(c + 1) * MXU_N, :].astype(BF16), NT_DIMS,
                                  preferred_element_type=F32)
            for s in range(MXU_N // LANES):
                t = c * (MXU_N // LANES) + s
                stage_ref[t] = acc[:, s * LANES:(s + 1) * LANES]
                for sig in range(CHUNK):
                    u_ref[t, :, sig * LANES:(sig + 1) * LANES] = (
                        stage_ref[t, pl.ds(sig, tm // CHUNK, stride=CHUNK), :].astype(u_ref.dtype))

    @pl.when(j > 0)
    def _():
        n = n_ref[...]
        tn = gate_ref.shape[1]
        for c0 in range(0, tn, 2 * MXU_N):
            cols = slice(c0, c0 + 2 * MXU_N)
            gate_ref[:, cols] = lax.dot_general(n, w_ref[cols, :].astype(BF16), NT_DIMS,
                                                preferred_element_type=F32).astype(gate_ref.dtype)


def _ug_proj(n, wt, tm, tn, row0):
    m, k = n.shape
    n_gate = 2 * D_MODEL
    assert tn == SSM_WIDTH
    return pl.pallas_call(
        _ug_proj_kernel,
        grid=(m // tm, 1 + n_gate // tn),
        in_specs=[
            pl.BlockSpec((tm, k), lambda i, j: (i, 0)),
            pl.BlockSpec((pl.Element(tn), pl.Element(k)),
                         lambda i, j: (pl.multiple_of(row0 + j * tn, SUBLANES), 0)),
        ],
        out_specs=[
            pl.BlockSpec((N_TILE, tm // CHUNK, CW), lambda i, j: (0, i, 0)),
            pl.BlockSpec((tm, tn), lambda i, j: (i, jnp.maximum(j - 1, 0))),
        ],
        out_shape=[jax.ShapeDtypeStruct((N_TILE, m // CHUNK, CW), BF16),
                   jax.ShapeDtypeStruct((m, n_gate), BF16)],
        scratch_shapes=[pltpu.VMEM((N_TILE, tm, LANES), F32)],
        compiler_params=_cparams("parallel", "arbitrary"),
        name="ug_proj",
    )(n, wt)


def _rope_slab(x, cosf, sin1, sin2):
    return (x * cosf + pltpu.roll(x, LANES - QK_ROPE // 2, 1) * sin1
            + pltpu.roll(x, QK_ROPE // 2, 1) * sin2)


def _up_rope_kernel(ql_ref, kvl_ref, kr_ref, gq_ref, gkv_ref, wq_ref, wkv_ref,
                    cos_ref, sin1_ref, sin2_ref, q_ref, kt_ref, v_ref):
    cosf, sin1, sin2 = cos_ref[...], sin1_ref[...], sin2_ref[...]
    qn = _rms(ql_ref[...].astype(F32), gq_ref[...]).astype(BF16)
    kvn = _rms(kvl_ref[...].astype(F32), gkv_ref[...]).astype(BF16)
    k_rope_t = _rope_slab(kr_ref[...].astype(F32), cosf, sin1, sin2).T.astype(BF16)
    ones_cols = (lax.broadcasted_iota(jnp.int32, (ql_ref.shape[0], V_PAD - V_HEAD), 1) == 0
                 ).astype(F32).astype(BF16)
    for h in range(HEADS):
        acc = jnp.dot(qn, wq_ref[:, h * HEAD_PAD:(h + 1) * HEAD_PAD], preferred_element_type=F32)
        q_ref[h, :, :LANES] = acc[:, :LANES].astype(BF16)
        q_ref[h, :, LANES:] = _rope_slab(acc[:, LANES:], cosf, sin1, sin2).astype(BF16)
    kv_w = QK_NOPE + V_HEAD
    for h in range(HEADS):
        acc = jnp.dot(kvn, wkv_ref[:, h * kv_w:(h + 1) * kv_w], preferred_element_type=F32)
        kt_ref[h, :LANES, :] = acc[:, :QK_NOPE].T.astype(BF16)
        kt_ref[h, LANES:, :] = k_rope_t
        v_ref[h, :, :V_HEAD] = acc[:, QK_NOPE:].astype(BF16)
        v_ref[h, :, V_HEAD:] = ones_cols


def _up_rope(lat, gq, gkv, wq, wkv, cosf, sin1, sin2, tm):
    m = lat.shape[0]
    n_pos_blocks = cosf.shape[0] // tm
    row = lambda i: (i, 0)
    const = lambda i: (0, 0)
    pos = lambda i: (i % n_pos_blocks, 0)
    head_major = lambda i: (0, i, 0)
    return pl.pallas_call(
        _up_rope_kernel,
        grid=(m // tm,),
        in_specs=[
            pl.BlockSpec((tm, Q_LORA), row),
            pl.BlockSpec((tm, KV_LORA), lambda i: (i, 1)),
            pl.BlockSpec((tm, LANES), lambda i: (i, (Q_LORA + KV_LORA) // LANES)),
            pl.BlockSpec((1, Q_LORA), const),
            pl.BlockSpec((1, KV_LORA), const),
            pl.BlockSpec(wq.shape, const),
            pl.BlockSpec(wkv.shape, const),
            pl.BlockSpec((tm, LANES), pos),
            pl.BlockSpec((tm, LANES), pos),
            pl.BlockSpec((tm, LANES), pos),
        ],
        out_specs=[
            pl.BlockSpec((HEADS, tm, HEAD_PAD), head_major),
            pl.BlockSpec((HEADS, HEAD_PAD, tm), lambda i: (0, 0, i)),
            pl.BlockSpec((HEADS, tm, V_PAD), head_major),
        ],
        out_shape=[
            jax.ShapeDtypeStruct((HEADS, m, HEAD_PAD), BF16),
            jax.ShapeDtypeStruct((HEADS, HEAD_PAD, m), BF16),
            jax.ShapeDtypeStruct((HEADS, m, V_PAD), BF16),
        ],
        compiler_params=_cparams("parallel"),
        name="up_rope",
    )(lat, lat, lat, gq, gkv, wq, wkv, cosf, sin1, sin2)


def _attn_kernel(qi_ref, ki_ref, q_ref, kt_ref, v_ref, kmt_ref, vm_ref, *refs):
    n_side = (len(refs) - 3) // 2
    side_in, o_ref, side_out = refs[:n_side], refs[n_side], refs[n_side + 1:2 * n_side + 1]
    m_ref, acc_ref = refs[2 * n_side + 1:]
    for w_ref, wb_ref in zip(side_in, side_out):
        wb_ref[...] = w_ref[...].astype(wb_ref.dtype)
    t = pl.program_id(1)
    qi = qi_ref[t]
    ki = ki_ref[t]

    def lane_tiled(x, width):
        return jnp.concatenate([x] * (width // LANES), axis=1)

    @pl.when(ki == 0)
    def _init():
        def body(h, c):
            s = jnp.dot(q_ref[h], kmt_ref[h], preferred_element_type=F32)
            key = lax.broadcasted_iota(jnp.int32, s.shape, 1)
            s = jnp.where(key < N_META, s, -jnp.inf)
            m = jnp.broadcast_to(jnp.max(s, axis=-1, keepdims=True), s.shape)
            p = jnp.exp2(s - m)
            m_ref[h] = m
            acc_ref[h] = jnp.dot(p.astype(BF16), vm_ref[h], preferred_element_type=F32)
            return c
        lax.fori_loop(0, HEADS, body, 0, unroll=2)

    def tile(masked):
        def scores(h):
            return jnp.dot(q_ref[h], kt_ref[h], preferred_element_type=F32)

        def body(h, s):
            if masked:
                qry = lax.broadcasted_iota(jnp.int32, s.shape, 0)
                key = lax.broadcasted_iota(jnp.int32, s.shape, 1)
                s = jnp.where(key <= qry, s, -jnp.inf)
            m_prev = m_ref[h]
            m_new = jnp.maximum(m_prev, jnp.max(s, axis=-1, keepdims=True))
            alpha = jnp.exp2(m_prev - m_new)
            p = jnp.exp2(s - lane_tiled(m_new, s.shape[1]))
            acc_ref[h] = lane_tiled(alpha, V_PAD) * acc_ref[h] + jnp.dot(
                p.astype(BF16), v_ref[h], preferred_element_type=F32)
            m_ref[h] = m_new

        s_next = scores(0)
        for h in range(HEADS):
            s_cur = s_next
            if h + 1 < HEADS:
                s_next = scores(h + 1)
            body(h, s_cur)

    @pl.when(ki < qi)
    def _full():
        tile(False)

    @pl.when(ki == qi)
    def _diag():
        tile(True)
        for h in range(HEADS):
            out = acc_ref[h, :, :V_HEAD] / acc_ref[h, :, V_HEAD:V_HEAD + 1]
            o_ref[:, h * V_HEAD:(h + 1) * V_HEAD] = out.astype(o_ref.dtype)


def _attention(q, k, v, km, vm, side_weights, bsz, seq, tq):
    nq = seq // tq
    pairs = [(a, b) for a in range(nq) for b in range(a + 1)]
    qi_arr = jnp.asarray([p[0] for p in pairs], jnp.int32)
    ki_arr = jnp.asarray([p[1] for p in pairs], jnp.int32)
    n_steps = bsz * len(pairs)
    side_specs = []
    for w in side_weights:
        tiles = w.shape[0] // 16
        n_blk = max(nb for nb in range(1, n_steps + 1) if tiles % nb == 0)
        spec = pl.BlockSpec((w.shape[0] // n_blk, w.shape[1]),
                            lambda b, t, qi, ki, n_blk=n_blk: (jnp.minimum(b * len(pairs) + t, n_blk - 1), 0))
        side_specs.append(spec)
    grid_spec = pltpu.PrefetchScalarGridSpec(
        num_scalar_prefetch=2,
        grid=(bsz, len(pairs)),
        in_specs=[
            pl.BlockSpec((HEADS, tq, HEAD_PAD), lambda b, t, qi, ki: (0, b * nq + qi[t], 0)),
            pl.BlockSpec((HEADS, HEAD_PAD, tq), lambda b, t, qi, ki: (0, 0, b * nq + ki[t])),
            pl.BlockSpec((HEADS, tq, V_PAD), lambda b, t, qi, ki: (0, b * nq + ki[t], 0)),
            pl.BlockSpec(km.shape, lambda b, t, qi, ki: (0, 0, 0)),
            pl.BlockSpec(vm.shape, lambda b, t, qi, ki: (0, 0, 0)),
        ] + side_specs,
        out_specs=[pl.BlockSpec((tq, HEADS * V_HEAD), lambda b, t, qi, ki: (b * nq + qi[t], 0))] + side_specs,
        scratch_shapes=[
            pltpu.VMEM((HEADS, tq, LANES), F32),
            pltpu.VMEM((HEADS, tq, V_PAD), F32),
        ],
    )
    outs = pl.pallas_call(
        _attn_kernel,
        grid_spec=grid_spec,
        out_shape=[jax.ShapeDtypeStruct((bsz * seq, HEADS * V_HEAD), BF16)]
        + [jax.ShapeDtypeStruct(w.shape, BF16) for w in side_weights],
        compiler_params=_cparams("arbitrary", "arbitrary"),
        name="flash_attn",
    )(qi_arr, ki_arr, q, k, v, km, vm, *side_weights)
    return outs[0], outs[1:]


def _ssm_param_kernel(lr_ref, li_ref, ls_ref, bre_ref, bim_ref, cre_ref, cim_ref, d_ref,
                      win_ref, cen_ref, tl_ref, a1_ref, a2_ref):
    lane = lax.broadcasted_iota(jnp.int32, (SSM_P, LANES), 1)
    row = lax.broadcasted_iota(jnp.int32, (SSM_P, LANES), 0)
    on_diag = jnp.bitwise_and(lane, SSM_P - 1) == row
    cat = lambda x, y: jnp.concatenate([x, y], axis=1)
    lr_all, li_all = lr_ref[0], li_ref[0]
    step_all = jnp.exp(ls_ref[0])
    mag = jnp.exp(lr_all * step_all)
    ab_re, ab_im = mag * jnp.cos(li_all * step_all), mag * jnp.sin(li_all * step_all)
    den = lr_all * lr_all + li_all * li_all
    nr, ni = ab_re - 1.0, ab_im
    coef_re_all = (nr * lr_all + ni * li_all) / den
    coef_im_all = (ni * lr_all - nr * li_all) / den
    for g in range(bre_ref.shape[0]):
        a_re, a_im = ab_re[g:g + 1], ab_im[g:g + 1]
        coef_re, coef_im = coef_re_all[g:g + 1], coef_im_all[g:g + 1]
        bre, bim = bre_ref[g], bim_ref[g]
        bb_re = coef_re * bre - coef_im * bim
        bb_im = coef_re * bim + coef_im * bre
        cre, cim = cre_ref[g], cim_ref[g]
        ce_re_lags, ce_im_lags = [], []
        er, ei = jnp.ones_like(a_re), jnp.zeros_like(a_im)
        for k in range(CHUNK + 1):
            if k == CHUNK:
                a1_ref[g] = cat(er, er)
                a2_ref[g] = cat(-ei, ei)
            ce_re = cre * er - cim * ei
            ce_im = cre * ei + cim * er
            if k >= 1:
                cen_ref[g, k - 1] = cat(ce_re, -ce_im).astype(cen_ref.dtype)
            if k < CHUNK:
                ce_re_lags.append(ce_re)
                ce_im_lags.append(ce_im)
                r0 = (CHUNK - 1 - k) * SSM_P
                win_ref[g, r0:r0 + SSM_P, :] = cat(er * bb_re - ei * bb_im,
                                                   er * bb_im + ei * bb_re).astype(win_ref.dtype)
            er, ei = er * a_re - ei * a_im, er * a_im + ei * a_re
        tl = (lax.dot_general(jnp.concatenate(ce_re_lags, axis=0), jnp.concatenate([bb_re] * GPT, axis=0),
                              NT_DIMS, precision=lax.Precision.HIGHEST, preferred_element_type=F32)
              - lax.dot_general(jnp.concatenate(ce_im_lags, axis=0), jnp.concatenate([bb_im] * GPT, axis=0),
                                NT_DIMS, precision=lax.Precision.HIGHEST, preferred_element_type=F32))
        tl_ref[g] = tl
        tl_ref[g, :SSM_P, :] = tl[:SSM_P] + jnp.where(on_diag, d_ref[g], 0.0)


def _ssm_params(lam_re, lam_im, log_step, b_re, b_im, c_re, c_im, d, gb):
    g = SSM_G
    row = lambda shape: pl.BlockSpec((gb,) + shape, lambda i: (i,) + (0,) * len(shape))
    step_blk = lambda width: pl.BlockSpec((1, gb, width), lambda i: (i, 0, 0))
    outs = [
        ((g, CP, 2 * SSM_N), (CP, 2 * SSM_N), BF16),
        ((g, CHUNK, SSM_P, 2 * SSM_N), (CHUNK, SSM_P, 2 * SSM_N), BF16),
        ((g, CP, LANES), (CP, LANES), F32),
        ((g, 1, 2 * SSM_N), (1, 2 * SSM_N), F32),
        ((g, 1, 2 * SSM_N), (1, 2 * SSM_N), F32),
    ]
    return pl.pallas_call(
        _ssm_param_kernel,
        grid=(g // gb,),
        in_specs=[step_blk(SSM_N), step_blk(SSM_N), step_blk(1),
                  row((SSM_P, SSM_N)), row((SSM_P, SSM_N)), row((SSM_P, SSM_N)), row((SSM_P, SSM_N)),
                  row((SSM_P, 1))],
        out_specs=[row(blk) for _, blk, _ in outs],
        out_shape=[jax.ShapeDtypeStruct(full, dt) for full, _, dt in outs],
        compiler_params=_cparams("parallel"),
        name="ssm_params",
    )(lam_re.reshape(g // gb, gb, SSM_N), lam_im.reshape(g // gb, gb, SSM_N), log_step.reshape(g // gb, gb, 1),
      jnp.swapaxes(b_re, 1, 2), jnp.swapaxes(b_im, 1, 2), c_re, c_im, d.reshape(g, SSM_P, 1))


def _ssm_kernel(u_ref, um_ref, wc_ref, a1_ref, a2_ref, tl_ref, cen_ref, y_ref,
                s_ref, s0_ref, win_ref, wt_ref, ws_ref, *, bsz, n_chunks):
    i = pl.program_id(0)
    rows = bsz * n_chunks

    @pl.when(i < N_TILE)
    def _state_inputs():
        @pl.when(i == 0)
        def _():
            win_ref[...] = jnp.zeros(win_ref.shape, win_ref.dtype)

        for sig in range(CHUNK):
            for g in range(GPT):
                r0 = sig * LANES + g * SSM_P
                win_ref[r0:r0 + SSM_P, g * 2 * SSM_N:(g + 1) * 2 * SSM_N] = (
                    wc_ref[g, sig * SSM_P:(sig + 1) * SSM_P, :])
        v = jnp.dot(u_ref[0], win_ref[...], preferred_element_type=F32)
        v_meta = jnp.dot(um_ref[0], win_ref[...], preferred_element_type=F32)
        for g in range(GPT):
            s_ref[g, pl.ds(i, rows, stride=N_TILE), :] = v[:, g * 2 * SSM_N:(g + 1) * 2 * SSM_N]
            s0_ref[g, pl.ds(i, 1), :] = v_meta[0:1, g * 2 * SSM_N:(g + 1) * 2 * SSM_N]

    @pl.when(i == N_TILE - 1)
    def _recurrence():
        a1 = [a1_ref[:, g * 2 * SSM_N:(g + 1) * 2 * SSM_N] for g in range(GPT)]
        a2 = [a2_ref[:, g * 2 * SSM_N:(g + 1) * 2 * SSM_N] for g in range(GPT)]
        for b in range(bsz):
            def body(j, carry):
                r = pl.multiple_of((b * n_chunks + j) * N_TILE, N_TILE)
                nxt = []
                for g in range(GPT):
                    s, sw = carry[2 * g], carry[2 * g + 1]
                    v = s_ref[g, pl.ds(r, N_TILE), :]
                    vw = pltpu.roll(v, SSM_N, 1)
                    s_ref[g, pl.ds(r, N_TILE), :] = s
                    nxt.append(a1[g] * s + a2[g] * sw + v)
                    nxt.append(a1[g] * sw - a2[g] * s + vw)
                return tuple(nxt)
            init = []
            for g in range(GPT):
                init += [s0_ref[g], pltpu.roll(s0_ref[g], SSM_N, 1)]
            lax.fori_loop(0, n_chunks, body, tuple(init), unroll=SCAN_UNROLL)

    @pl.when(i >= N_TILE)
    def _outputs():
        tile = i - N_TILE
        row_g = lax.shift_right_logical(lax.broadcasted_iota(jnp.int32, (LANES, LANES), 0), SSM_P_LOG2)
        col_g = lax.shift_right_logical(lax.broadcasted_iota(jnp.int32, (LANES, LANES), 1), SSM_P_LOG2)
        same_group = row_g == col_g
        @pl.when(i == N_TILE)
        def _():
            wt_ref[...] = jnp.zeros(wt_ref.shape, wt_ref.dtype)
            ws_ref[...] = jnp.zeros(ws_ref.shape, ws_ref.dtype)

        for k in range(CHUNK):
            rep = jnp.concatenate([tl_ref[g, k * SSM_P:(k + 1) * SSM_P, :] for g in range(GPT)], axis=0)
            blk = jnp.where(same_group, rep, 0.0).astype(BF16)
            for sig in range(CHUNK - k):
                tau = sig + k
                wt_ref[tau * LANES:(tau + 1) * LANES, sig * LANES:(sig + 1) * LANES] = blk
        for g in range(GPT):
            for tau in range(CHUNK):
                r0 = tau * LANES + g * SSM_P
                ws_ref[r0:r0 + SSM_P, g * 2 * SSM_N:(g + 1) * 2 * SSM_N] = cen_ref[g, tau]
        states = jnp.concatenate(
            [s_ref[g, pl.ds(tile, rows, stride=N_TILE), :] for g in range(GPT)], axis=1).astype(BF16)
        u = u_ref[0]
        steps_per_tile = MXU_N // LANES
        for c in range(CW // MXU_N):
            k_hi = (c + 1) * MXU_N
            out_rows = slice(c * MXU_N, k_hi)
            y = _gelu_tanh(
                lax.dot_general(u[:, :k_hi], wt_ref[out_rows, :k_hi], NT_DIMS, preferred_element_type=F32)
                + lax.dot_general(states, ws_ref[out_rows, :], NT_DIMS, preferred_element_type=F32))
            for s in range(steps_per_tile):
                tau = c * steps_per_tile + s
                y_ref[pl.ds(tau, rows, stride=CHUNK), :] = y[:, s * LANES:(s + 1) * LANES]


def _ssm(u, u_meta, w_compact, a1, a2, t_lags, ce_lags, bsz, n_chunks):
    rows = bsz * n_chunks
    in_tile = lambda i: (lax.rem(i, N_TILE), 0, 0)
    out_tile = lambda i: (jnp.maximum(i - N_TILE, 0), 0, 0)
    out_tile4 = lambda i: (jnp.maximum(i - N_TILE, 0), 0, 0, 0)
    whole = lambda i: (0, 0)
    return pl.pallas_call(
        functools.partial(_ssm_kernel, bsz=bsz, n_chunks=n_chunks),
        grid=(2 * N_TILE,),
        in_specs=[pl.BlockSpec((1, rows, CW), in_tile),
                  pl.BlockSpec((1,) + u_meta.shape[1:], in_tile),
                  pl.BlockSpec((GPT, CP, 2 * SSM_N), in_tile),
                  pl.BlockSpec((N_TILE, SW), whole),
                  pl.BlockSpec((N_TILE, SW), whole),
                  pl.BlockSpec((GPT, CP, LANES), out_tile),
                  pl.BlockSpec((GPT, CHUNK, SSM_P, 2 * SSM_N), out_tile4)],
        out_specs=pl.BlockSpec((rows * CHUNK, LANES), lambda i: (0, jnp.maximum(i - N_TILE, 0))),
        out_shape=jax.ShapeDtypeStruct((rows * CHUNK, SSM_WIDTH), F32),
        scratch_shapes=[pltpu.VMEM((GPT, rows * N_TILE, 2 * SSM_N), F32),
                        pltpu.VMEM((GPT, N_TILE, 2 * SSM_N), F32),
                        pltpu.VMEM((CW, SW), BF16),
                        pltpu.VMEM((CW, CW), BF16),
                        pltpu.VMEM((CW, SW), BF16)],
        compiler_params=_cparams("arbitrary"),
        name="ssm",
    )(u, u_meta, w_compact, a1, a2, t_lags, ce_lags)


def _merge_kernel(attn_ref, y_ref, wap_ref, wv_ref, wg_ref, g0_ref, g1_ref, o_ref):
    attn = attn_ref[...]
    y = y_ref[...].astype(BF16)
    for c in range(o_ref.shape[1] // MXU_N):
        cols = slice(c * MXU_N, (c + 1) * MXU_N)
        a = jnp.dot(attn, wap_ref[:, cols], preferred_element_type=F32)
        ssm = (jnp.dot(y, wv_ref[:, cols], preferred_element_type=F32)
               * _sigmoid(jnp.dot(y, wg_ref[:, cols], preferred_element_type=F32)))
        mixed = (_sigmoid(g0_ref[:, cols].astype(F32)) * a
                 + _sigmoid(g1_ref[:, cols].astype(F32)) * ssm)
        o_ref[:, cols] = mixed.astype(o_ref.dtype)


def _merge(attn, y, wap, wv, wg, gates, tm, tn):
    m = attn.shape[0]
    g1_off = D_MODEL // tn
    return pl.pallas_call(
        _merge_kernel,
        grid=(m // tm, D_MODEL // tn),
        in_specs=[
            pl.BlockSpec((tm, attn.shape[1]), lambda i, j: (i, 0)),
            pl.BlockSpec((tm, SSM_WIDTH), lambda i, j: (i, 0)),
            pl.BlockSpec((attn.shape[1], tn), lambda i, j: (0, j)),
            pl.BlockSpec((SSM_WIDTH, tn), lambda i, j: (0, j)),
            pl.BlockSpec((SSM_WIDTH, tn), lambda i, j: (0, j)),
            pl.BlockSpec((tm, tn), lambda i, j: (i, j)),
            pl.BlockSpec((tm, tn), lambda i, j: (i, g1_off + j)),
        ],
        out_specs=pl.BlockSpec((tm, tn), lambda i, j: (i, j)),
        out_shape=jax.ShapeDtypeStruct((m, D_MODEL), BF16),
        compiler_params=_cparams("parallel", "arbitrary"),
        name="gated_merge",
    )(attn, y, wap, wv, wg, gates, gates)


def _out_proj_kernel(x_ref, mx_ref, w_ref, g_ref, h_ref, n_ref):
    mx = mx_ref[...]
    d = h_ref.shape[1]
    sq = None
    for c in range(d // MXU_N):
        cols = slice(c * MXU_N, (c + 1) * MXU_N)
        h = x_ref[:, cols] + jnp.dot(mx, w_ref[:, cols], preferred_element_type=F32)
        h_ref[:, cols] = h
        part = jnp.sum(h * h, axis=-1, keepdims=True)
        sq = part if sq is None else sq + part
    inv = lax.rsqrt(sq * (1.0 / d) + EPS)
    n_ref[...] = (h_ref[...] * inv * g_ref[...]).astype(n_ref.dtype)


def _out_proj(x, mixed, w, g, tm):
    m, d = x.shape
    row = pl.BlockSpec((tm, d), lambda i: (i, 0))
    return pl.pallas_call(
        _out_proj_kernel,
        grid=(m // tm,),
        in_specs=[row, row, pl.BlockSpec((d, d), lambda i: (0, 0)), pl.BlockSpec((1, d), lambda i: (0, 0))],
        out_specs=[row, row],
        out_shape=[jax.ShapeDtypeStruct((m, d), F32), jax.ShapeDtypeStruct((m, d), BF16)],
        compiler_params=_cparams("parallel"),
        name="out_proj",
    )(x, mixed, w, g)


def _ffn_kernel(n_ref, h_hbm, wg_ref, wu_ref, wd_ref, gf_ref, o_ref, h_buf, h_sem):
    i = pl.program_id(0)
    f = pl.program_id(1)
    tm = n_ref.shape[0]
    tf = wg_ref.shape[1]
    d = wd_ref.shape[1]

    def residual_copy():
        rows = pl.ds(pl.multiple_of(i * tm, tm), tm)
        return pltpu.make_async_copy(h_hbm.at[rows], h_buf, h_sem)

    @pl.when(f == 0)
    def _():
        residual_copy().start()
        o_ref[...] = jnp.zeros(o_ref.shape, o_ref.dtype)

    n = n_ref[...]
    acts = []
    for c in range(tf // MXU_N):
        cols = slice(c * MXU_N, (c + 1) * MXU_N)
        gate = jnp.dot(n, wg_ref[:, cols], preferred_element_type=F32)
        up = jnp.dot(n, wu_ref[:, cols], preferred_element_type=F32)
        acts.append((gate * _sigmoid(gate) * up).astype(BF16))

    for c in range(d // FFN_ACC_COLS):
        cols = slice(c * FFN_ACC_COLS, (c + 1) * FFN_ACC_COLS)
        part = None
        for k, a in enumerate(acts):
            term = jnp.dot(a, wd_ref[k * MXU_N:(k + 1) * MXU_N, cols], preferred_element_type=F32)
            part = term if part is None else part + term
        o_ref[:, cols] += part

    @pl.when(f == pl.num_programs(1) - 1)
    def _():
        residual_copy().wait()
        o_ref[...] = _rms(h_buf[...] + o_ref[...], gf_ref[...])


def _ffn(n, h, wg, wu, wd, gf, tm, tf):
    m, d = h.shape
    row = lambda i, f: (i, 0)
    return pl.pallas_call(
        _ffn_kernel,
        grid=(m // tm, D_FF // tf),
        in_specs=[
            pl.BlockSpec((tm, d), row),
            pl.BlockSpec(memory_space=pl.ANY),
            pl.BlockSpec((d, tf), lambda i, f: (0, f)),
            pl.BlockSpec((d, tf), lambda i, f: (0, f)),
            pl.BlockSpec((tf, d), lambda i, f: (f, 0)),
            pl.BlockSpec((1, d), lambda i, f: (0, 0)),
        ],
        out_specs=pl.BlockSpec((tm, d), row),
        out_shape=jax.ShapeDtypeStruct((m, d), F32),
        scratch_shapes=[pltpu.VMEM((tm, d), F32), pltpu.SemaphoreType.DMA(())],
        compiler_params=pltpu.CompilerParams(dimension_semantics=("arbitrary", "arbitrary"),
                                             vmem_limit_bytes=FFN_VMEM_LIMIT),
        name="ffn",
    )(n, h, wg, wu, wd, gf)


def _rope_tables(length):
    half = QK_ROPE // 2
    freqs = np.float32(ROPE_THETA) ** (-np.arange(half, dtype=np.float32) / np.float32(half))
    ang = (np.arange(length, dtype=np.float32)[:, None] * freqs[None, :]).astype(np.float32)
    c, s = np.cos(ang).astype(np.float32), np.sin(ang).astype(np.float32)
    z = np.zeros_like(c)
    cosf = np.concatenate([c, c, z, z], axis=1)
    sin1 = np.concatenate([-s, z, z, z], axis=1)
    sin2 = np.concatenate([z, s, z, z], axis=1)
    return cosf, sin1, sin2


def kernel(x, meta_tokens, norm_mix, w_in, norm_q, w_q_up, norm_kv, w_kv_up, w_attn_proj, ssm_lambda_re, ssm_lambda_im, ssm_log_step, ssm_b_re, ssm_b_im, ssm_c_re, ssm_c_im, ssm_d, w_glu_val, w_glu_gate, w_out, norm_ffn, w_ffn_gate, w_ffn_up, w_ffn_down, norm_final):
    bsz, seq, d = x.shape
    assert d == D_MODEL and w_in.shape[0] == 1 and (bsz * seq) % TM_FFN == 0 and seq % TQ_ATTN == 0 and N_META == CHUNK
    t_rows = bsz * seq
    n_chunks = seq // CHUNK
    bj = bsz * n_chunks
    x2 = x.reshape(t_rows, d)
    meta = jnp.pad(meta_tokens.astype(x.dtype), ((0, META_ROWS - N_META), (0, 0)))

    w_in_t = jnp.swapaxes(w_in[0], 0, 1)
    off_u = Q_LORA + KV_LORA + QK_ROPE
    scale = (QK_NOPE + QK_ROPE) ** -0.5 * math.log2(math.e)
    wq = jnp.pad((w_q_up[0] * scale).reshape(Q_LORA, HEADS, QK_NOPE + QK_ROPE),
                 ((0, 0), (0, 0), (0, HEAD_PAD - QK_NOPE - QK_ROPE))).reshape(Q_LORA, HEADS * HEAD_PAD).astype(BF16)
    wkv = w_kv_up[0].astype(BF16)
    g_mix = norm_mix[0].reshape(1, d)
    g_q = norm_q[0].reshape(1, Q_LORA)
    g_kv = norm_kv[0].reshape(1, KV_LORA)
    cosf, sin1, sin2 = _rope_tables(N_META + seq)

    lat, n_mix = _norm_matmul(x2, g_mix, w_in_t, BF16, TM_LAT, LAT_W, LAT_W, 0, emit_norm=True)
    u_seq, gates = _ug_proj(n_mix, w_in_t, TM_UG, TN_UG, off_u)
    lat_m = _norm_matmul(meta, g_mix, w_in_t, BF16, META_ROWS, LAT_W, LAT_W, 0)
    u_m = _norm_matmul(meta, g_mix, w_in_t, BF16, META_ROWS, SSM_WIDTH, SSM_WIDTH, off_u)

    q, k, v = _up_rope(lat, g_q, g_kv, wq, wkv, cosf[N_META:], sin1[N_META:], sin2[N_META:], TM_UP)
    _, k_m, v_m = _up_rope(lat_m, g_q, g_kv, wq, wkv, cosf[:META_ROWS], sin1[:META_ROWS],
                            sin2[:META_ROWS], META_ROWS)
    attn, (w_ap, w_gv, w_gg, w_o, w_fg, w_fu, w_fd) = _attention(
        q, k, v, k_m, v_m,
        [w_attn_proj[0], w_glu_val[0], w_glu_gate[0], w_out[0], w_ffn_gate[0], w_ffn_up[0], w_ffn_down[0]],
        bsz, seq, TQ_ATTN)

    w_sin, ce_lags, t_lags, a1, a2 = _ssm_params(
        ssm_lambda_re[0], ssm_lambda_im[0], ssm_log_step[0], ssm_b_re[0], ssm_b_im[0],
        ssm_c_re[0], ssm_c_im[0], ssm_d[0], PARAM_GROUPS)
    u_meta = jnp.pad(u_m[:N_META].reshape(1, CHUNK, N_TILE, LANES).transpose(2, 0, 1, 3).reshape(N_TILE, 1, CW),
                     ((0, 0), (0, 15), (0, 0)))
    y = _ssm(u_seq, u_meta, w_sin, a1.reshape(N_TILE, SW), a2.reshape(N_TILE, SW), t_lags, ce_lags,
             bsz, n_chunks)

    mixed = _merge(attn, y, w_ap, w_gv, w_gg, gates, TM_MERGE, TN_MERGE)
    h1, n_ffn = _out_proj(x2, mixed, w_o, norm_ffn[0].reshape(1, d), TM_OUT)
    out = _ffn(n_ffn, h1, w_fg, w_fu, w_fd, norm_final.reshape(1, d), TM_FFN, TF_FFN)
    return out.reshape(bsz, seq, d)
```

```python
import functools
import math

import numpy as np
import jax
import jax.numpy as jnp
from jax import lax
from jax.experimental import pallas as pl
from jax.experimental.pallas import tpu as pltpu

F32 = jnp.float32
BF16 = jnp.bfloat16

D_MODEL = 2048
N_META = 16
EPS = 1e-6
HEADS = 16
Q_LORA = 512
KV_LORA = 512
QK_NOPE = 128
QK_ROPE = 64
V_HEAD = 128
ROPE_THETA = 10000.0
SSM_WIDTH = 1024
SSM_P = 16
SSM_P_LOG2 = 4
SSM_G = 64
SSM_N = 64
D_FF = 5632
LANES = 128
SUBLANES = 8
MXU_N = 256
FFN_ACC_COLS = 512
SCAN_UNROLL = 8
HEAD_PAD = 2 * LANES
LAT_W = Q_LORA + KV_LORA + LANES
CHUNK = 16
CP = CHUNK * SSM_P
GPT = LANES // SSM_P
N_TILE = SSM_G // GPT
CW = CHUNK * LANES
SW = GPT * 2 * SSM_N
NT_DIMS = (((1,), (1,)), ((), ()))
META_ROWS = 128
V_PAD = 2 * LANES
TM_LAT = 512
TM_UG, TN_UG = 1024, SSM_WIDTH
TM_UP = 512
TQ_ATTN = 512
TM_MERGE, TN_MERGE = 1024, 1024
TM_OUT = 512
TM_FFN, TF_FFN = 1024, 512
PARAM_GROUPS = 8
VMEM_LIMIT = 56 * 1024 * 1024
FFN_VMEM_LIMIT = 60 * 1024 * 1024


def _cparams(*sem):
    return pltpu.CompilerParams(dimension_semantics=sem, vmem_limit_bytes=VMEM_LIMIT)


def _sigmoid(x):
    return 0.5 + 0.5 * jnp.tanh(0.5 * x)


def _gelu_tanh(x):
    return x * (0.5 * (1.0 + jnp.tanh(math.sqrt(2.0 / math.pi) * (x + 0.044715 * (x * x * x)))))


def _rms(x, g):
    ms = jnp.mean(x * x, axis=-1, keepdims=True)
    return x * lax.rsqrt(ms + EPS) * g


def _norm_matmul_kernel(x_ref, g_ref, w_ref, o_ref, *rest):
    n_ref = rest[-1]
    @pl.when(pl.program_id(1) == 0)
    def _():
        n_ref[...] = _rms(x_ref[...], g_ref[...]).astype(BF16)
        if len(rest) == 2:
            rest[0][...] = n_ref[...]

    n = n_ref[...]
    tn = o_ref.shape[1]
    for c0 in range(0, tn, 2 * MXU_N):
        cols = slice(c0, min(c0 + 2 * MXU_N, tn))
        o_ref[:, cols] = lax.dot_general(n, w_ref[cols, :].astype(BF16), NT_DIMS,
                                         preferred_element_type=F32).astype(o_ref.dtype)


def _norm_matmul(x, g, wt, out_dtype, tm, tn, n, row0, emit_norm=False):
    m, k = x.shape
    out_specs = [pl.BlockSpec((tm, tn), lambda i, j: (i, j))]
    out_shape = [jax.ShapeDtypeStruct((m, n), out_dtype)]
    if emit_norm:
        out_specs.append(pl.BlockSpec((tm, k), lambda i, j: (i, 0)))
        out_shape.append(jax.ShapeDtypeStruct((m, k), BF16))
    return pl.pallas_call(
        _norm_matmul_kernel,
        grid=(m // tm, n // tn),
        in_specs=[
            pl.BlockSpec((tm, k), lambda i, j: (i, 0)),
            pl.BlockSpec((1, k), lambda i, j: (0, 0)),
            pl.BlockSpec((pl.Element(tn), pl.Element(k)),
                         lambda i, j: (pl.multiple_of(row0 + j * tn, SUBLANES), 0)),
        ],
        out_specs=out_specs if emit_norm else out_specs[0],
        out_shape=out_shape if emit_norm else out_shape[0],
        scratch_shapes=[pltpu.VMEM((tm, k), BF16)],
        compiler_params=_cparams("parallel", "arbitrary"),
        name="norm_matmul",
    )(x, g, wt)


def _ug_proj_kernel(n_ref, w_ref, u_ref, stage_ref):
    tm = n_ref.shape[0]
    n = n_ref[...]
    for c in range(SSM_WIDTH // MXU_N):
        acc = lax.dot_general(n, w_ref[c * MXU_N:(c + 1) * MXU_N, :].astype(BF16), NT_DIMS,
                              preferred_element_type=F32)
        for s in range(MXU_N // LANES):
            t = c * (MXU_N // LANES) + s
            stage_ref[t] = acc[:, s * LANES:(s + 1) * LANES]
            for sig in range(CHUNK):
                u_ref[t, :, sig * LANES:(sig + 1) * LANES] = (
                    stage_ref[t, pl.ds(sig, tm // CHUNK, stride=CHUNK), :].astype(u_ref.dtype))


def _u_proj(n, wt, tm, tn, row0):
    m, k = n.shape
    assert tn == SSM_WIDTH
    return pl.pallas_call(
        _ug_proj_kernel,
        grid=(m // tm, 1),
        in_specs=[
            pl.BlockSpec((tm, k), lambda i, j: (i, 0)),
            pl.BlockSpec((pl.Element(tn), pl.Element(k)), lambda i, j: (row0, 0)),
        ],
        out_specs=pl.BlockSpec((N_TILE, tm // CHUNK, CW), lambda i, j: (0, i, 0)),
        out_shape=jax.ShapeDtypeStruct((N_TILE, m // CHUNK, CW), BF16),
        scratch_shapes=[pltpu.VMEM((N_TILE, tm, LANES), F32)],
        compiler_params=_cparams("parallel", "arbitrary"),
        name="u_proj",
    )(n, wt)


def _gates_proj_kernel(n_ref, w_ref, gate_ref, wb_ref):
    @pl.when(pl.program_id(1) == 0)
    def _():
        wb_ref[...] = w_ref[...].astype(BF16)

    n = n_ref[...]
    tn = gate_ref.shape[1]
    for c0 in range(0, tn, 2 * MXU_N):
        cols = slice(c0, c0 + 2 * MXU_N)
        gate_ref[:, cols] = lax.dot_general(n, wb_ref[cols, :], NT_DIMS,
                                            preferred_element_type=F32).astype(gate_ref.dtype)


def _gates_proj(n, wt, tm, tn, row0):
    m, k = n.shape
    n_gate = 2 * D_MODEL
    return pl.pallas_call(
        _gates_proj_kernel,
        grid=(n_gate // tn, m // tm),
        in_specs=[
            pl.BlockSpec((tm, k), lambda j, i: (i, 0)),
            pl.BlockSpec((pl.Element(tn), pl.Element(k)),
                         lambda j, i: (pl.multiple_of(row0 + j * tn, SUBLANES), 0)),
        ],
        out_specs=pl.BlockSpec((tm, tn), lambda j, i: (i, j)),
        out_shape=jax.ShapeDtypeStruct((m, n_gate), BF16),
        scratch_shapes=[pltpu.VMEM((tn, k), BF16)],
        compiler_params=_cparams("arbitrary", "arbitrary"),
        name="gates_proj",
    )(n, wt)


def _rope_slab(x, cosf, sin1, sin2):
    return (x * cosf + pltpu.roll(x, LANES - QK_ROPE // 2, 1) * sin1
            + pltpu.roll(x, QK_ROPE // 2, 1) * sin2)


def _up_rope_kernel(ql_ref, kvl_ref, kr_ref, gq_ref, gkv_ref, wq_ref, wkv_ref,
                    cos_ref, sin1_ref, sin2_ref, q_ref, kt_ref, v_ref):
    cosf, sin1, sin2 = cos_ref[...], sin1_ref[...], sin2_ref[...]
    qn = _rms(ql_ref[...], gq_ref[...]).astype(BF16)
    kvn = _rms(kvl_ref[...], gkv_ref[...]).astype(BF16)
    k_rope_t = _rope_slab(kr_ref[...], cosf, sin1, sin2).T.astype(BF16)
    ones_cols = (lax.broadcasted_iota(jnp.int32, (ql_ref.shape[0], V_PAD - V_HEAD), 1) == 0
                 ).astype(F32).astype(BF16)
    for h in range(HEADS):
        acc = jnp.dot(qn, wq_ref[:, h * HEAD_PAD:(h + 1) * HEAD_PAD], preferred_element_type=F32)
        q_ref[h, :, :LANES] = acc[:, :LANES].astype(BF16)
        q_ref[h, :, LANES:] = _rope_slab(acc[:, LANES:], cosf, sin1, sin2).astype(BF16)
    kv_w = QK_NOPE + V_HEAD
    for h in range(HEADS):
        acc = jnp.dot(kvn, wkv_ref[:, h * kv_w:(h + 1) * kv_w], preferred_element_type=F32)
        kt_ref[h, :LANES, :] = acc[:, :QK_NOPE].T.astype(BF16)
        kt_ref[h, LANES:, :] = k_rope_t
        v_ref[h, :, :V_HEAD] = acc[:, QK_NOPE:].astype(BF16)
        v_ref[h, :, V_HEAD:] = ones_cols


def _up_rope(lat, gq, gkv, wq, wkv, cosf, sin1, sin2, tm):
    m = lat.shape[0]
    n_pos_blocks = cosf.shape[0] // tm
    row = lambda i: (i, 0)
    const = lambda i: (0, 0)
    pos = lambda i: (i % n_pos_blocks, 0)
    head_major = lambda i: (0, i, 0)
    return pl.pallas_call(
        _up_rope_kernel,
        grid=(m // tm,),
        in_specs=[
            pl.BlockSpec((tm, Q_LORA), row),
            pl.BlockSpec((tm, KV_LORA), lambda i: (i, 1)),
            pl.BlockSpec((tm, LANES), lambda i: (i, (Q_LORA + KV_LORA) // LANES)),
            pl.BlockSpec((1, Q_LORA), const),
            pl.BlockSpec((1, KV_LORA), const),
            pl.BlockSpec(wq.shape, const),
            pl.BlockSpec(wkv.shape, const),
            pl.BlockSpec((tm, LANES), pos),
            pl.BlockSpec((tm, LANES), pos),
            pl.BlockSpec((tm, LANES), pos),
        ],
        out_specs=[
            pl.BlockSpec((HEADS, tm, HEAD_PAD), head_major),
            pl.BlockSpec((HEADS, HEAD_PAD, tm), lambda i: (0, 0, i)),
            pl.BlockSpec((HEADS, tm, V_PAD), head_major),
        ],
        out_shape=[
            jax.ShapeDtypeStruct((HEADS, m, HEAD_PAD), BF16),
            jax.ShapeDtypeStruct((HEADS, HEAD_PAD, m), BF16),
            jax.ShapeDtypeStruct((HEADS, m, V_PAD), BF16),
        ],
        compiler_params=_cparams("parallel"),
        name="up_rope",
    )(lat, lat, lat, gq, gkv, wq, wkv, cosf, sin1, sin2)


def _attn_kernel(qi_ref, ki_ref, q_ref, kt_ref, v_ref, kmt_ref, vm_ref, *refs):
    n_side = (len(refs) - 3) // 2
    side_in, o_ref, side_out = refs[:n_side], refs[n_side], refs[n_side + 1:2 * n_side + 1]
    m_ref, acc_ref = refs[2 * n_side + 1:]
    for w_ref, wb_ref in zip(side_in, side_out):
        wb_ref[...] = w_ref[...].astype(wb_ref.dtype)
    t = pl.program_id(1)
    qi = qi_ref[t]
    ki = ki_ref[t]

    def lane_tiled(x, width):
        return jnp.concatenate([x] * (width // LANES), axis=1)

    @pl.when(ki == 0)
    def _init():
        def body(h, c):
            s = jnp.dot(q_ref[h], kmt_ref[h], preferred_element_type=F32)
            key = lax.broadcasted_iota(jnp.int32, s.shape, 1)
            s = jnp.where(key < N_META, s, -jnp.inf)
            m = jnp.broadcast_to(jnp.max(s, axis=-1, keepdims=True), s.shape)
            p = jnp.exp2(s - m)
            m_ref[h] = m
            acc_ref[h] = jnp.dot(p.astype(BF16), vm_ref[h], preferred_element_type=F32)
            return c
        lax.fori_loop(0, HEADS, body, 0, unroll=2)

    def tile(masked):
        def scores(h):
            return jnp.dot(q_ref[h], kt_ref[h], preferred_element_type=F32)

        def body(h, s):
            if masked:
                qry = lax.broadcasted_iota(jnp.int32, s.shape, 0)
                key = lax.broadcasted_iota(jnp.int32, s.shape, 1)
                s = jnp.where(key <= qry, s, -jnp.inf)
            m_prev = m_ref[h]
            m_new = jnp.maximum(m_prev, jnp.max(s, axis=-1, keepdims=True))
            alpha = jnp.exp2(m_prev - m_new)
            p = jnp.exp2(s - lane_tiled(m_new, s.shape[1]))
            acc_ref[h] = lane_tiled(alpha, V_PAD) * acc_ref[h] + jnp.dot(
                p.astype(BF16), v_ref[h], preferred_element_type=F32)
            m_ref[h] = m_new

        s_next = scores(0)
        for h in range(HEADS):
            s_cur = s_next
            if h + 1 < HEADS:
                s_next = scores(h + 1)
            body(h, s_cur)

    @pl.when(ki < qi)
    def _full():
        tile(False)

    @pl.when(ki == qi)
    def _diag():
        tile(True)
        for h in range(HEADS):
            out = acc_ref[h, :, :V_HEAD] / acc_ref[h, :, V_HEAD:V_HEAD + 1]
            o_ref[:, h * V_HEAD:(h + 1) * V_HEAD] = out.astype(o_ref.dtype)


def _attention(q, k, v, km, vm, side_weights, bsz, seq, tq):
    nq = seq // tq
    pairs = [(a, b) for a in range(nq) for b in range(a + 1)]
    qi_arr = jnp.asarray([p[0] for p in pairs], jnp.int32)
    ki_arr = jnp.asarray([p[1] for p in pairs], jnp.int32)
    n_steps = bsz * len(pairs)
    side_specs = []
    for w in side_weights:
        tiles = w.shape[0] // 16
        n_blk = max(nb for nb in range(1, n_steps + 1) if tiles % nb == 0)
        spec = pl.BlockSpec((w.shape[0] // n_blk, w.shape[1]),
                            lambda b, t, qi, ki, n_blk=n_blk: (jnp.minimum(b * len(pairs) + t, n_blk - 1), 0))
        side_specs.append(spec)
    grid_spec = pltpu.PrefetchScalarGridSpec(
        num_scalar_prefetch=2,
        grid=(bsz, len(pairs)),
        in_specs=[
            pl.BlockSpec((HEADS, tq, HEAD_PAD), lambda b, t, qi, ki: (0, b * nq + qi[t], 0)),
            pl.BlockSpec((HEADS, HEAD_PAD, tq), lambda b, t, qi, ki: (0, 0, b * nq + ki[t])),
            pl.BlockSpec((HEADS, tq, V_PAD), lambda b, t, qi, ki: (0, b * nq + ki[t], 0)),
            pl.BlockSpec(km.shape, lambda b, t, qi, ki: (0, 0, 0)),
            pl.BlockSpec(vm.shape, lambda b, t, qi, ki: (0, 0, 0)),
        ] + side_specs,
        out_specs=[pl.BlockSpec((tq, HEADS * V_HEAD), lambda b, t, qi, ki: (b * nq + qi[t], 0))] + side_specs,
        scratch_shapes=[
            pltpu.VMEM((HEADS, tq, LANES), F32),
            pltpu.VMEM((HEADS, tq, V_PAD), F32),
        ],
    )
    outs = pl.pallas_call(
        _attn_kernel,
        grid_spec=grid_spec,
        out_shape=[jax.ShapeDtypeStruct((bsz * seq, HEADS * V_HEAD), BF16)]
        + [jax.ShapeDtypeStruct(w.shape, BF16) for w in side_weights],
        compiler_params=_cparams("arbitrary", "arbitrary"),
        name="flash_attn",
    )(qi_arr, ki_arr, q, k, v, km, vm, *side_weights)
    return outs[0], outs[1:]


def _ssm_param_kernel(lr_ref, li_ref, ls_ref, bre_ref, bim_ref, cre_ref, cim_ref, d_ref,
                      win_ref, cen_ref, tl_ref, a1_ref, a2_ref):
    lane = lax.broadcasted_iota(jnp.int32, (SSM_P, LANES), 1)
    row = lax.broadcasted_iota(jnp.int32, (SSM_P, LANES), 0)
    on_diag = jnp.bitwise_and(lane, SSM_P - 1) == row
    cat = lambda x, y: jnp.concatenate([x, y], axis=1)
    lr_all, li_all = lr_ref[0], li_ref[0]
    step_all = jnp.exp(ls_ref[0])
    mag = jnp.exp(lr_all * step_all)
    ab_re, ab_im = mag * jnp.cos(li_all * step_all), mag * jnp.sin(li_all * step_all)
    den = lr_all * lr_all + li_all * li_all
    nr, ni = ab_re - 1.0, ab_im
    coef_re_all = (nr * lr_all + ni * li_all) / den
    coef_im_all = (ni * lr_all - nr * li_all) / den
    for g in range(bre_ref.shape[0]):
        a_re, a_im = ab_re[g:g + 1], ab_im[g:g + 1]
        coef_re, coef_im = coef_re_all[g:g + 1], coef_im_all[g:g + 1]
        bre, bim = bre_ref[g], bim_ref[g]
        bb_re = coef_re * bre - coef_im * bim
        bb_im = coef_re * bim + coef_im * bre
        cre, cim = cre_ref[g], cim_ref[g]
        ce_re_lags, ce_im_lags = [], []
        er, ei = jnp.ones_like(a_re), jnp.zeros_like(a_im)
        for k in range(CHUNK + 1):
            if k == CHUNK:
                a1_ref[g] = cat(er, er)
                a2_ref[g] = cat(-ei, ei)
            ce_re = cre * er - cim * ei
            ce_im = cre * ei + cim * er
            if k >= 1:
                cen_ref[g, k - 1] = cat(ce_re, -ce_im).astype(cen_ref.dtype)
            if k < CHUNK:
                ce_re_lags.append(ce_re)
                ce_im_lags.append(ce_im)
                r0 = (CHUNK - 1 - k) * SSM_P
                win_ref[g, r0:r0 + SSM_P, :] = cat(er * bb_re - ei * bb_im,
                                                   er * bb_im + ei * bb_re).astype(win_ref.dtype)
            er, ei = er * a_re - ei * a_im, er * a_im + ei * a_re
        tl = (lax.dot_general(jnp.concatenate(ce_re_lags, axis=0), jnp.concatenate([bb_re] * GPT, axis=0),
                              NT_DIMS, precision=lax.Precision.HIGHEST, preferred_element_type=F32)
              - lax.dot_general(jnp.concatenate(ce_im_lags, axis=0), jnp.concatenate([bb_im] * GPT, axis=0),
                                NT_DIMS, precision=lax.Precision.HIGHEST, preferred_element_type=F32))
        tl_ref[g] = tl
        tl_ref[g, :SSM_P, :] = tl[:SSM_P] + jnp.where(on_diag, d_ref[g], 0.0)


def _ssm_params(lam_re, lam_im, log_step, b_re, b_im, c_re, c_im, d, gb):
    g = SSM_G
    row = lambda shape: pl.BlockSpec((gb,) + shape, lambda i: (i,) + (0,) * len(shape))
    step_blk = lambda width: pl.BlockSpec((1, gb, width), lambda i: (i, 0, 0))
    outs = [
        ((g, CP, 2 * SSM_N), (CP, 2 * SSM_N), BF16),
        ((g, CHUNK, SSM_P, 2 * SSM_N), (CHUNK, SSM_P, 2 * SSM_N), BF16),
        ((g, CP, LANES), (CP, LANES), F32),
        ((g, 1, 2 * SSM_N), (1, 2 * SSM_N), F32),
        ((g, 1, 2 * SSM_N), (1, 2 * SSM_N), F32),
    ]
    return pl.pallas_call(
        _ssm_param_kernel,
        grid=(g // gb,),
        in_specs=[step_blk(SSM_N), step_blk(SSM_N), step_blk(1),
                  row((SSM_P, SSM_N)), row((SSM_P, SSM_N)), row((SSM_P, SSM_N)), row((SSM_P, SSM_N)),
                  row((SSM_P, 1))],
        out_specs=[row(blk) for _, blk, _ in outs],
        out_shape=[jax.ShapeDtypeStruct(full, dt) for full, _, dt in outs],
        compiler_params=_cparams("parallel"),
        name="ssm_params",
    )(lam_re.reshape(g // gb, gb, SSM_N), lam_im.reshape(g // gb, gb, SSM_N), log_step.reshape(g // gb, gb, 1),
      jnp.swapaxes(b_re, 1, 2), jnp.swapaxes(b_im, 1, 2), c_re, c_im, d.reshape(g, SSM_P, 1))


def _ssm_kernel(u_ref, um_ref, wc_ref, a1_ref, a2_ref, tl_ref, cen_ref, y_ref,
                s_ref, s0_ref, win_ref, wt_ref, ws_ref, *, bsz, n_chunks):
    i = pl.program_id(0)
    rows = bsz * n_chunks

    @pl.when(i < N_TILE)
    def _state_inputs():
        @pl.when(i == 0)
        def _():
            win_ref[...] = jnp.zeros(win_ref.shape, win_ref.dtype)

        for sig in range(CHUNK):
            for g in range(GPT):
                r0 = sig * LANES + g * SSM_P
                win_ref[r0:r0 + SSM_P, g * 2 * SSM_N:(g + 1) * 2 * SSM_N] = (
                    wc_ref[g, sig * SSM_P:(sig + 1) * SSM_P, :])
        v = jnp.dot(u_ref[0], win_ref[...], preferred_element_type=F32)
        v_meta = jnp.dot(um_ref[0], win_ref[...], preferred_element_type=F32)
        for g in range(GPT):
            s_ref[g, pl.ds(i, rows, stride=N_TILE), :] = v[:, g * 2 * SSM_N:(g + 1) * 2 * SSM_N]
            s0_ref[g, pl.ds(i, 1), :] = v_meta[0:1, g * 2 * SSM_N:(g + 1) * 2 * SSM_N]

    @pl.when(i == N_TILE - 1)
    def _recurrence():
        a1 = [a1_ref[:, g * 2 * SSM_N:(g + 1) * 2 * SSM_N] for g in range(GPT)]
        a2 = [a2_ref[:, g * 2 * SSM_N:(g + 1) * 2 * SSM_N] for g in range(GPT)]
        for b in range(bsz):
            def body(j, carry):
                r = pl.multiple_of((b * n_chunks + j) * N_TILE, N_TILE)
                nxt = []
                for g in range(GPT):
                    s, sw = carry[2 * g], carry[2 * g + 1]
                    v = s_ref[g, pl.ds(r, N_TILE), :]
                    vw = pltpu.roll(v, SSM_N, 1)
                    s_ref[g, pl.ds(r, N_TILE), :] = s
                    nxt.append(a1[g] * s + a2[g] * sw + v)
                    nxt.append(a1[g] * sw - a2[g] * s + vw)
                return tuple(nxt)
            init = []
            for g in range(GPT):
                init += [s0_ref[g], pltpu.roll(s0_ref[g], SSM_N, 1)]
            lax.fori_loop(0, n_chunks, body, tuple(init), unroll=SCAN_UNROLL)

    @pl.when(i >= N_TILE)
    def _outputs():
        tile = i - N_TILE
        row_g = lax.shift_right_logical(lax.broadcasted_iota(jnp.int32, (LANES, LANES), 0), SSM_P_LOG2)
        col_g = lax.shift_right_logical(lax.broadcasted_iota(jnp.int32, (LANES, LANES), 1), SSM_P_LOG2)
        same_group = row_g == col_g
        @pl.when(i == N_TILE)
        def _():
            wt_ref[...] = jnp.zeros(wt_ref.shape, wt_ref.dtype)
            ws_ref[...] = jnp.zeros(ws_ref.shape, ws_ref.dtype)

        for k in range(CHUNK):
            rep = jnp.concatenate([tl_ref[g, k * SSM_P:(k + 1) * SSM_P, :] for g in range(GPT)], axis=0)
            blk = jnp.where(same_group, rep, 0.0).astype(BF16)
            for sig in range(CHUNK - k):
                tau = sig + k
                wt_ref[tau * LANES:(tau + 1) * LANES, sig * LANES:(sig + 1) * LANES] = blk
        for g in range(GPT):
            for tau in range(CHUNK):
                r0 = tau * LANES + g * SSM_P
                ws_ref[r0:r0 + SSM_P, g * 2 * SSM_N:(g + 1) * 2 * SSM_N] = cen_ref[g, tau]
        states = jnp.concatenate(
            [s_ref[g, pl.ds(tile, rows, stride=N_TILE), :] for g in range(GPT)], axis=1).astype(BF16)
        u = u_ref[0]
        steps_per_tile = MXU_N // LANES
        for c in range(CW // MXU_N):
            k_hi = (c + 1) * MXU_N
            out_rows = slice(c * MXU_N, k_hi)
            y = _gelu_tanh(
                lax.dot_general(u[:, :k_hi], wt_ref[out_rows, :k_hi], NT_DIMS, preferred_element_type=F32)
                + lax.dot_general(states, ws_ref[out_rows, :], NT_DIMS, preferred_element_type=F32))
            for s in range(steps_per_tile):
                tau = c * steps_per_tile + s
                y_ref[pl.ds(tau, rows, stride=CHUNK), :] = y[:, s * LANES:(s + 1) * LANES]


def _ssm(u, u_meta, w_compact, a1, a2, t_lags, ce_lags, bsz, n_chunks):
    rows = bsz * n_chunks
    in_tile = lambda i: (lax.rem(i, N_TILE), 0, 0)
    out_tile = lambda i: (jnp.maximum(i - N_TILE, 0), 0, 0)
    out_tile4 = lambda i: (jnp.maximum(i - N_TILE, 0), 0, 0, 0)
    whole = lambda i: (0, 0)
    return pl.pallas_call(
        functools.partial(_ssm_kernel, bsz=bsz, n_chunks=n_chunks),
        grid=(2 * N_TILE,),
        in_specs=[pl.BlockSpec((1, rows, CW), in_tile),
                  pl.BlockSpec((1,) + u_meta.shape[1:], in_tile),
                  pl.BlockSpec((GPT, CP, 2 * SSM_N), in_tile),
                  pl.BlockSpec((N_TILE, SW), whole),
                  pl.BlockSpec((N_TILE, SW), whole),
                  pl.BlockSpec((GPT, CP, LANES), out_tile),
                  pl.BlockSpec((GPT, CHUNK, SSM_P, 2 * SSM_N), out_tile4)],
        out_specs=pl.BlockSpec((rows * CHUNK, LANES), lambda i: (0, jnp.maximum(i - N_TILE, 0))),
        out_shape=jax.ShapeDtypeStruct((rows * CHUNK, SSM_WIDTH), F32),
        scratch_shapes=[pltpu.VMEM((GPT, rows * N_TILE, 2 * SSM_N), F32),
                        pltpu.VMEM((GPT, N_TILE, 2 * SSM_N), F32),
                        pltpu.VMEM((CW, SW), BF16),
                        pltpu.VMEM((CW, CW), BF16),
                        pltpu.VMEM((CW, SW), BF16)],
        compiler_params=_cparams("arbitrary"),
        name="ssm",
    )(u, u_meta, w_compact, a1, a2, t_lags, ce_lags)


def _merge_kernel(attn_ref, y_ref, wap_ref, wv_ref, wg_ref, g0_ref, g1_ref, o_ref):
    attn = attn_ref[...]
    y = y_ref[...].astype(BF16)
    for c in range(o_ref.shape[1] // MXU_N):
        cols = slice(c * MXU_N, (c + 1) * MXU_N)
        a = jnp.dot(attn, wap_ref[:, cols], preferred_element_type=F32)
        ssm = (jnp.dot(y, wv_ref[:, cols], preferred_element_type=F32)
               * _sigmoid(jnp.dot(y, wg_ref[:, cols], preferred_element_type=F32)))
        mixed = (_sigmoid(g0_ref[:, cols].astype(F32)) * a
                 + _sigmoid(g1_ref[:, cols].astype(F32)) * ssm)
        o_ref[:, cols] = mixed.astype(o_ref.dtype)


def _merge(attn, y, wap, wv, wg, gates, tm, tn):
    m = attn.shape[0]
    g1_off = D_MODEL // tn
    return pl.pallas_call(
        _merge_kernel,
        grid=(m // tm, D_MODEL // tn),
        in_specs=[
            pl.BlockSpec((tm, attn.shape[1]), lambda i, j: (i, 0)),
            pl.BlockSpec((tm, SSM_WIDTH), lambda i, j: (i, 0)),
            pl.BlockSpec((attn.shape[1], tn), lambda i, j: (0, j)),
            pl.BlockSpec((SSM_WIDTH, tn), lambda i, j: (0, j)),
            pl.BlockSpec((SSM_WIDTH, tn), lambda i, j: (0, j)),
            pl.BlockSpec((tm, tn), lambda i, j: (i, j)),
            pl.BlockSpec((tm, tn), lambda i, j: (i, g1_off + j)),
        ],
        out_specs=pl.BlockSpec((tm, tn), lambda i, j: (i, j)),
        out_shape=jax.ShapeDtypeStruct((m, D_MODEL), BF16),
        compiler_params=_cparams("parallel", "arbitrary"),
        name="gated_merge",
    )(attn, y, wap, wv, wg, gates, gates)


def _out_proj_kernel(x_ref, mx_ref, w_ref, g_ref, h_ref, n_ref):
    mx = mx_ref[...]
    d = h_ref.shape[1]
    sq = None
    for c in range(d // MXU_N):
        cols = slice(c * MXU_N, (c + 1) * MXU_N)
        h = x_ref[:, cols] + jnp.dot(mx, w_ref[:, cols], preferred_element_type=F32)
        h_ref[:, cols] = h
        part = jnp.sum(h * h, axis=-1, keepdims=True)
        sq = part if sq is None else sq + part
    inv = lax.rsqrt(sq * (1.0 / d) + EPS)
    n_ref[...] = (h_ref[...] * inv * g_ref[...]).astype(n_ref.dtype)


def _out_proj(x, mixed, w, g, tm):
    m, d = x.shape
    row = pl.BlockSpec((tm, d), lambda i: (i, 0))
    return pl.pallas_call(
        _out_proj_kernel,
        grid=(m // tm,),
        in_specs=[row, row, pl.BlockSpec((d, d), lambda i: (0, 0)), pl.BlockSpec((1, d), lambda i: (0, 0))],
        out_specs=[row, row],
        out_shape=[jax.ShapeDtypeStruct((m, d), F32), jax.ShapeDtypeStruct((m, d), BF16)],
        compiler_params=_cparams("parallel"),
        name="out_proj",
    )(x, mixed, w, g)


def _ffn_kernel(n_ref, h_hbm, wg_ref, wu_ref, wd_ref, gf_ref, o_ref, h_buf, h_sem):
    i = pl.program_id(0)
    f = pl.program_id(1)
    tm = n_ref.shape[0]
    tf = wg_ref.shape[1]
    d = wd_ref.shape[1]

    def residual_copy():
        rows = pl.ds(pl.multiple_of(i * tm, tm), tm)
        return pltpu.make_async_copy(h_hbm.at[rows], h_buf, h_sem)

    @pl.when(f == 0)
    def _():
        residual_copy().start()
        o_ref[...] = jnp.zeros(o_ref.shape, o_ref.dtype)

    n = n_ref[...]
    acts = []
    for c in range(tf // MXU_N):
        cols = slice(c * MXU_N, (c + 1) * MXU_N)
        gate = jnp.dot(n, wg_ref[:, cols], preferred_element_type=F32)
        up = jnp.dot(n, wu_ref[:, cols], preferred_element_type=F32)
        acts.append((gate * _sigmoid(gate) * up).astype(BF16))

    for c in range(d // FFN_ACC_COLS):
        cols = slice(c * FFN_ACC_COLS, (c + 1) * FFN_ACC_COLS)
        part = None
        for k, a in enumerate(acts):
            term = jnp.dot(a, wd_ref[k * MXU_N:(k + 1) * MXU_N, cols], preferred_element_type=F32)
            part = term if part is None else part + term
        o_ref[:, cols] += part

    @pl.when(f == pl.num_programs(1) - 1)
    def _():
        residual_copy().wait()
        o_ref[...] = _rms(h_buf[...] + o_ref[...], gf_ref[...])


def _ffn(n, h, wg, wu, wd, gf, tm, tf):
    m, d = h.shape
    row = lambda i, f: (i, 0)
    return pl.pallas_call(
        _ffn_kernel,
        grid=(m // tm, D_FF // tf),
        in_specs=[
            pl.BlockSpec((tm, d), row),
            pl.BlockSpec(memory_space=pl.ANY),
            pl.BlockSpec((d, tf), lambda i, f: (0, f)),
            pl.BlockSpec((d, tf), lambda i, f: (0, f)),
            pl.BlockSpec((tf, d), lambda i, f: (f, 0)),
            pl.BlockSpec((1, d), lambda i, f: (0, 0)),
        ],
        out_specs=pl.BlockSpec((tm, d), row),
        out_shape=jax.ShapeDtypeStruct((m, d), F32),
        scratch_shapes=[pltpu.VMEM((tm, d), F32), pltpu.SemaphoreType.DMA(())],
        compiler_params=pltpu.CompilerParams(dimension_semantics=("arbitrary", "arbitrary"),
                                             vmem_limit_bytes=FFN_VMEM_LIMIT),
        name="ffn",
    )(n, h, wg, wu, wd, gf)


def _rope_tables(length):
    half = QK_ROPE // 2
    freqs = np.float32(ROPE_THETA) ** (-np.arange(half, dtype=np.float32) / np.float32(half))
    ang = (np.arange(length, dtype=np.float32)[:, None] * freqs[None, :]).astype(np.float32)
    c, s = np.cos(ang).astype(np.float32), np.sin(ang).astype(np.float32)
    z = np.zeros_like(c)
    cosf = np.concatenate([c, c, z, z], axis=1)
    sin1 = np.concatenate([-s, z, z, z], axis=1)
    sin2 = np.concatenate([z, s, z, z], axis=1)
    return cosf, sin1, sin2


def kernel(x, meta_tokens, norm_mix, w_in, norm_q, w_q_up, norm_kv, w_kv_up, w_attn_proj, ssm_lambda_re, ssm_lambda_im, ssm_log_step, ssm_b_re, ssm_b_im, ssm_c_re, ssm_c_im, ssm_d, w_glu_val, w_glu_gate, w_out, norm_ffn, w_ffn_gate, w_ffn_up, w_ffn_down, norm_final):
    bsz, seq, d = x.shape
    assert d == D_MODEL and w_in.shape[0] == 1 and (bsz * seq) % TM_FFN == 0 and seq % TQ_ATTN == 0 and N_META == CHUNK
    t_rows = bsz * seq
    n_chunks = seq // CHUNK
    bj = bsz * n_chunks
    x2 = x.reshape(t_rows, d)
    meta = jnp.pad(meta_tokens.astype(x.dtype), ((0, META_ROWS - N_META), (0, 0)))

    w_in_t = jnp.swapaxes(w_in[0], 0, 1)
    off_u = Q_LORA + KV_LORA + QK_ROPE
    scale = (QK_NOPE + QK_ROPE) ** -0.5 * math.log2(math.e)
    wq = jnp.pad((w_q_up[0] * scale).reshape(Q_LORA, HEADS, QK_NOPE + QK_ROPE),
                 ((0, 0), (0, 0), (0, HEAD_PAD - QK_NOPE - QK_ROPE))).reshape(Q_LORA, HEADS * HEAD_PAD).astype(BF16)
    wkv = w_kv_up[0].astype(BF16)
    g_mix = norm_mix[0].reshape(1, d)
    g_q = norm_q[0].reshape(1, Q_LORA)
    g_kv = norm_kv[0].reshape(1, KV_LORA)
    cosf, sin1, sin2 = _rope_tables(N_META + seq)

    lat, n_mix = _norm_matmul(x2, g_mix, w_in_t, F32, TM_LAT, LAT_W, LAT_W, 0, emit_norm=True)
    u_seq = _u_proj(n_mix, w_in_t, TM_UG, TN_UG, off_u)
    gates = _gates_proj(n_mix, w_in_t, TM_UG, TN_UG, off_u + SSM_WIDTH)
    lat_m = _norm_matmul(meta, g_mix, w_in_t, F32, META_ROWS, LAT_W, LAT_W, 0)
    u_m = _norm_matmul(meta, g_mix, w_in_t, BF16, META_ROWS, SSM_WIDTH, SSM_WIDTH, off_u)

    q, k, v = _up_rope(lat, g_q, g_kv, wq, wkv, cosf[N_META:], sin1[N_META:], sin2[N_META:], TM_UP)
    _, k_m, v_m = _up_rope(lat_m, g_q, g_kv, wq, wkv, cosf[:META_ROWS], sin1[:META_ROWS],
                            sin2[:META_ROWS], META_ROWS)
    attn, (w_ap, w_gv, w_gg, w_o, w_fg, w_fu, w_fd) = _attention(
        q, k, v, k_m, v_m,
        [w_attn_proj[0], w_glu_val[0], w_glu_gate[0], w_out[0], w_ffn_gate[0], w_ffn_up[0], w_ffn_down[0]],
        bsz, seq, TQ_ATTN)

    w_sin, ce_lags, t_lags, a1, a2 = _ssm_params(
        ssm_lambda_re[0], ssm_lambda_im[0], ssm_log_step[0], ssm_b_re[0], ssm_b_im[0],
        ssm_c_re[0], ssm_c_im[0], ssm_d[0], PARAM_GROUPS)
    u_meta = jnp.pad(u_m[:N_META].reshape(1, CHUNK, N_TILE, LANES).transpose(2, 0, 1, 3).reshape(N_TILE, 1, CW),
                     ((0, 0), (0, 15), (0, 0)))
    y = _ssm(u_seq, u_meta, w_sin, a1.reshape(N_TILE, SW), a2.reshape(N_TILE, SW), t_lags, ce_lags,
             bsz, n_chunks)

    mixed = _merge(attn, y, w_ap, w_gv, w_gg, gates, TM_MERGE, TN_MERGE)
    h1, n_ffn = _out_proj(x2, mixed, w_o, norm_ffn[0].reshape(1, d), TM_OUT)
    out = _ffn(n_ffn, h1, w_fg, w_fu, w_fd, norm_final.reshape(1, d), TM_FFN, TF_FFN)
    return out.reshape(bsz, seq, d)
```
